```python
import math
import jax, jax.numpy as jnp
from jax import lax
import numpy as np

D_MODEL = 1024
BATCH = 8
SEQ = 2048
DEPTH = 2
DEC_BATCH = 128
DEC_SEQ = 4
PAST_LEN = 16384
PAGE_SIZE = 128

A_WIDTH = D_MODEL
A_GROUPS = 8
A_GROUP_DIM = A_WIDTH // A_GROUPS
A_CHUNK = 128
R_HEADS = 4
R_QK_DIM = D_MODEL // R_HEADS
R_V_DIM = 2 * R_QK_DIM
R_QK_WIDTH = R_HEADS * R_QK_DIM
R_V_WIDTH = R_HEADS * R_V_DIM
R_CHUNK = 128
ROPE_BASE = 10000.0
IN_COLS = 2 * A_WIDTH + 2 * R_QK_WIDTH + 2 * R_V_WIDTH + 2 * D_MODEL
N_GROUPS = 4
EXPERTS_PER_GROUP = 8
N_EXPERTS = N_GROUPS * EXPERTS_PER_GROUP
TOP_K = 2
EXPERT_DIM = 256
EPS = 1e-6

kernel_name = "hybrid_gmlp_retention_hmoe_decode_step"


def rmsnorm(x, g):
    xf = x.astype(jnp.float32)
    y = xf * lax.rsqrt(jnp.mean(xf * xf, axis=-1, keepdims=True) + EPS)
    return (y * g.astype(jnp.float32)).astype(x.dtype)


def layernorm(x, g, b):
    xf = x.astype(jnp.float32)
    mu = jnp.mean(xf, axis=-1, keepdims=True)
    var = jnp.mean(jnp.square(xf - mu), axis=-1, keepdims=True)
    y = (xf - mu) * lax.rsqrt(var + EPS) * g.astype(jnp.float32) + b.astype(jnp.float32)
    return y.astype(x.dtype)


def head_groupnorm(y):
    mu = jnp.mean(y, axis=-1, keepdims=True)
    var = jnp.mean(jnp.square(y - mu), axis=-1, keepdims=True)
    return (y - mu) * lax.rsqrt(var + EPS)


def rotary(x, pos):
    half = x.shape[-1] // 2
    inv_freq = jnp.power(ROPE_BASE, -jnp.arange(half, dtype=jnp.float32) / half)
    ang = pos.astype(jnp.float32)[:, None] * inv_freq[None, :]
    cos = jnp.cos(ang)[None, :, None, :]
    sin = jnp.sin(ang)[None, :, None, :]
    xf = x.astype(jnp.float32)
    x1, x2 = xf[..., :half], xf[..., half:]
    return jnp.concatenate([x1 * cos - x2 * sin, x1 * sin + x2 * cos], axis=-1)


def chunk_spatial_gating(u, v, w_s, b_s):
    B, T, _ = v.shape
    t_pad = -(-T // A_CHUNK) * A_CHUNK
    vp = jnp.pad(v, ((0, 0), (0, t_pad - T), (0, 0)))
    n = t_pad // A_CHUNK
    vg = vp.reshape(B, n, A_CHUNK, A_GROUPS, A_GROUP_DIM)
    causal = jnp.tril(jnp.ones((A_CHUNK, A_CHUNK), dtype=bool))
    ws = jnp.where(causal[None], w_s, jnp.zeros_like(w_s))
    mixed = jnp.einsum("gts,bnsgc->bntgc", ws, vg) + b_s.T[None, None, :, :, None]
    mixed = mixed.reshape(B, t_pad, A_WIDTH)[:, :T]
    return u * mixed


def retention(q, k, v, r0, chunk):
    B, T, H, dk = q.shape
    dv = v.shape[-1]
    n = T // chunk
    log_g = jnp.log1p(-jnp.power(2.0, -5.0 - jnp.arange(H, dtype=jnp.float32)))
    idx = jnp.arange(chunk, dtype=jnp.float32)
    diff = idx[:, None] - idx[None, :]
    decay_mask = jnp.where(diff >= 0, jnp.exp(log_g[:, None, None] * jnp.maximum(diff, 0.0)), 0.0)
    xi = jnp.exp(log_g[:, None] * (idx + 1.0))
    zeta = jnp.exp(log_g[:, None] * (chunk - 1.0 - idx))
    chunk_decay = jnp.exp(log_g * chunk)

    def to_chunks(a):
        return a.astype(jnp.float32).reshape(B, n, chunk, H, a.shape[-1]).transpose(1, 0, 3, 2, 4)

    qc, kc, vc = to_chunks(q), to_chunks(k), to_chunks(v)

    def step(r, inp):
        qi, ki, vi = inp
        scores = jnp.einsum("bhtd,bhsd->bhts", qi, ki) * decay_mask[None]
        inner = jnp.einsum("bhts,bhse->bhte", scores, vi)
        cross = jnp.einsum("bhtd,bhde->bhte", qi, r) * xi[None, :, :, None]
        r_new = r * chunk_decay[None, :, None, None] + jnp.einsum("bhsd,bhse->bhde", ki * zeta[None, :, :, None], vi)
        return r_new, inner + cross

    r_final, out = lax.scan(step, r0.astype(jnp.float32), (qc, kc, vc))
    out = out.transpose(1, 0, 3, 2, 4).reshape(B, T, H, dv)
    return out, r_final


def token_mixing(h, pos, r0, p, l):
    B, T, _ = h.shape
    z = h @ p["w_in"][l]
    sizes = [A_WIDTH, A_WIDTH, R_QK_WIDTH, R_QK_WIDTH, R_V_WIDTH, R_V_WIDTH, D_MODEL, D_MODEL]
    u, v, q, k, vr, gr, ga, gb = jnp.split(z, np.cumsum(sizes)[:-1].tolist(), axis=-1)
    u = jax.nn.gelu(u)
    v = layernorm(jax.nn.gelu(v), p["ln_v_g"][l], p["ln_v_b"][l])
    a = chunk_spatial_gating(u, v, p["w_s"][l], p["b_s"][l])
    q = rotary(q.reshape(B, T, R_HEADS, R_QK_DIM), pos)
    k = rotary(k.reshape(B, T, R_HEADS, R_QK_DIM), pos) * (R_QK_DIM ** -0.5)
    vr = vr.reshape(B, T, R_HEADS, R_V_DIM)
    chunk = R_CHUNK if T % R_CHUNK == 0 else T
    y, r_new = retention(q, k, vr, r0, chunk)
    y = head_groupnorm(y).reshape(B, T, R_V_WIDTH).astype(h.dtype)
    b = jax.nn.silu(gr) * y
    merged = jax.nn.sigmoid(ga) * (a @ p["w_a_out"][l]) + jax.nn.sigmoid(gb) * (b @ p["w_b_out"][l])
    return merged @ p["w_o"][l], r_new.astype(r0.dtype), v


def hier_moe(h, p, l):
    B, T, D = h.shape
    N = B * T
    x = h.reshape(N, D)
    g_logits = (x @ p["w_router_group"][l] + p["b_router_group"][l]).astype(jnp.float32)
    g_prob = jax.nn.softmax(g_logits, axis=-1)
    g_w, g_idx = lax.top_k(g_prob, 1)
    e_logits = (x @ p["w_router_expert"][l] + p["b_router_expert"][l]).astype(jnp.float32)
    e_logits = e_logits.reshape(N, N_GROUPS, EXPERTS_PER_GROUP)
    sel = jnp.broadcast_to(g_idx[:, :, None], (N, 1, EXPERTS_PER_GROUP))
    e_in = jnp.take_along_axis(e_logits, sel, axis=1)[:, 0]
    e_val, e_idx = lax.top_k(e_in, TOP_K)
    e_w = jax.nn.softmax(e_val, axis=-1) * g_w
    expert_id = g_idx * EXPERTS_PER_GROUP + e_idx
    combine = jnp.sum(jax.nn.one_hot(expert_id, N_EXPERTS, dtype=jnp.float32) * e_w[..., None], axis=1)
    gate = jnp.einsum("nd,edf->nef", x, p["w_gate"][l])
    up = jnp.einsum("nd,edf->nef", x, p["w_up"][l])
    hid = jax.nn.silu(gate) * up * combine[..., None].astype(x.dtype)
    out = jnp.einsum("nef,efd->nd", hid, p["w_down"][l])
    return out.reshape(B, T, D)


def run_trunk(x, c, pos, r_init, p):
    r_out, v_out = [], []
    silu_c = jax.nn.silu(c)
    for l in range(DEPTH):
        mod = silu_c @ p["w_mod"][l] + p["b_mod"][l]
        sh1, sc1, gt1, sh2, sc2, gt2 = jnp.split(mod[:, None, :], 6, axis=-1)
        h = rmsnorm(x, p["norm1_g"][l]) * (1.0 + sc1) + sh1
        if r_init is None:
            r0 = jnp.zeros((x.shape[0], R_HEADS, R_QK_DIM, R_V_DIM), jnp.float32)
        else:
            r0 = r_init[l]
        mix, r_new, v_rows = token_mixing(h, pos, r0, p, l)
        x = x + gt1 * mix
        h = rmsnorm(x, p["norm2_g"][l]) * (1.0 + sc2) + sh2
        x = x + gt2 * hier_moe(h, p, l)
        r_out.append(r_new)
        v_out.append(v_rows)
    return rmsnorm(x, p["final_g"]), jnp.stack(r_out), jnp.stack(v_out)


def setup_inputs(seed: int = 0) -> dict:
    key = jax.random.key(seed)
    ks = jax.random.split(key, 24)
    nrm = jax.random.normal
    D = D_MODEL
    f32 = jnp.float32
    return {
        "x_prompt": nrm(ks[0], (BATCH, SEQ, D), f32),
        "x_sample": nrm(ks[1], (DEC_BATCH, DEC_SEQ, D), f32),
        "state_ret": 0.05 * nrm(ks[2], (DEPTH, DEC_BATCH, R_HEADS, R_QK_DIM, R_V_DIM), f32),
        "c_prompt": nrm(ks[3], (BATCH, D), f32),
        "c_sample": nrm(ks[4], (DEC_BATCH, D), f32),
        "w_mod": 0.5 * D ** -0.5 * nrm(ks[5], (DEPTH, D, 6 * D), f32),
        "b_mod": 0.02 * nrm(ks[6], (DEPTH, 6 * D), f32),
        "norm1_g": 1.0 + 0.02 * nrm(ks[7], (DEPTH, D), f32),
        "w_in": D ** -0.5 * nrm(ks[8], (DEPTH, D, IN_COLS), f32),
        "ln_v_g": 1.0 + 0.02 * nrm(ks[9], (DEPTH, A_WIDTH), f32),
        "ln_v_b": 0.02 * nrm(ks[10], (DEPTH, A_WIDTH), f32),
        "w_s": A_CHUNK ** -0.5 * nrm(ks[11], (DEPTH, A_GROUPS, A_CHUNK, A_CHUNK), f32),
        "b_s": 1.0 + 0.02 * nrm(ks[12], (DEPTH, A_GROUPS, A_CHUNK), f32),
        "w_a_out": A_WIDTH ** -0.5 * nrm(ks[13], (DEPTH, A_WIDTH, D), f32),
        "w_b_out": R_V_WIDTH ** -0.5 * nrm(ks[14], (DEPTH, R_V_WIDTH, D), f32),
        "w_o": D ** -0.5 * nrm(ks[15], (DEPTH, D, D), f32),
        "norm2_g": 1.0 + 0.02 * nrm(ks[16], (DEPTH, D), f32),
        "w_router_group": D ** -0.5 * nrm(ks[17], (DEPTH, D, N_GROUPS), f32),
        "b_router_group": 0.01 * nrm(ks[18], (DEPTH, N_GROUPS), f32),
        "w_router_expert": D ** -0.5 * nrm(ks[19], (DEPTH, D, N_EXPERTS), f32),
        "b_router_expert": 0.01 * nrm(ks[20], (DEPTH, N_EXPERTS), f32),
        "w_gate": D ** -0.5 * nrm(ks[21], (DEPTH, N_EXPERTS, D, EXPERT_DIM), f32),
        "w_up": D ** -0.5 * nrm(ks[22], (DEPTH, N_EXPERTS, D, EXPERT_DIM), f32),
        "w_down": EXPERT_DIM ** -0.5 * nrm(ks[23], (DEPTH, N_EXPERTS, EXPERT_DIM, D), f32),
        "final_g": 1.0 + 0.02 * nrm(jax.random.fold_in(key, 99), (D,), f32),
    }


def reference(x_prompt, x_sample, state_ret, c_prompt, c_sample, w_mod, b_mod, norm1_g, w_in,
              ln_v_g, ln_v_b, w_s, b_s, w_a_out, w_b_out, w_o, norm2_g, w_router_group,
              b_router_group, w_router_expert, b_router_expert, w_gate, w_up, w_down, final_g):
    p = dict(w_mod=w_mod, b_mod=b_mod, norm1_g=norm1_g, w_in=w_in, ln_v_g=ln_v_g, ln_v_b=ln_v_b,
             w_s=w_s, b_s=b_s, w_a_out=w_a_out, w_b_out=w_b_out, w_o=w_o, norm2_g=norm2_g,
             w_router_group=w_router_group, b_router_group=b_router_group,
             w_router_expert=w_router_expert, b_router_expert=b_router_expert,
             w_gate=w_gate, w_up=w_up, w_down=w_down, final_g=final_g)
    pos_prompt = jnp.arange(x_prompt.shape[1], dtype=jnp.int32)
    pos_sample = PAST_LEN + jnp.arange(x_sample.shape[1], dtype=jnp.int32)
    y_prompt, state_ret_prompt, _ = run_trunk(x_prompt, c_prompt, pos_prompt, None, p)
    y_sample, state_ret_sample, state_chunk_v_sample = run_trunk(x_sample, c_sample, pos_sample, state_ret, p)
    return (y_prompt, y_sample, state_ret_prompt, state_ret_sample, state_chunk_v_sample)
```

```python
import functools

import jax
import jax.numpy as jnp
from jax import lax
from jax.experimental import pallas as pl
from jax.experimental.pallas import tpu as pltpu

D_MODEL = 1024
DEPTH = 2
PAST_LEN = 16384
A_GROUPS = 8
CHUNK = 128
R_HEADS = 4
R_QK_DIM = 256
R_V_DIM = 512
ROPE_BASE = 10000.0
N_GROUPS = 4
EXPERTS_PER_GROUP = 8
N_EXPERTS = 32
EXPERT_DIM = 256
EPS = 1e-6
IN_COLS = 10 * D_MODEL
MOD_COLS = 6 * D_MODEL

OFF_U, OFF_V, OFF_Q, OFF_K, OFF_VR, OFF_GR, OFF_GA, OFF_GB = 0, 1024, 2048, 3072, 4096, 6144, 8192, 9216

LANES = 128
IN_TILE = 256
MOE_TILE = 512
MOD_TILE = 1536
VMEM_LIMIT = 56 * 1024 * 1024
NEG = -1e30

BF16 = jnp.bfloat16
F32 = jnp.float32


def _dot(a, b):
    return jnp.dot(a, b, preferred_element_type=F32)


def _dot_f32(a, b):
    return jnp.dot(a, b, preferred_element_type=F32, precision=lax.Precision.HIGHEST)


def _sigmoid(x):
    return 1.0 / (1.0 + jnp.exp(-x))


def _gelu_tanh(x):
    return x * (0.5 * (1.0 + jnp.tanh(0.7978845608028654 * (x + 0.044715 * (x * x * x)))))


def _rms(x):
    return x * lax.rsqrt(jnp.mean(x * x, axis=-1, keepdims=True) + EPS)


def _mod_kernel(c_ref, w_ref, b_ref, o_ref):
    c = c_ref[...]
    o_ref[...] = _dot_f32(c * _sigmoid(c), w_ref[...]) + b_ref[...]


def _mod_call(c_all, w_mod, b_mod):
    n = c_all.shape[0]
    return pl.pallas_call(
        _mod_kernel,
        grid=(DEPTH, MOD_COLS // MOD_TILE),
        in_specs=[
            pl.BlockSpec((n, D_MODEL), lambda l, j: (0, 0)),
            pl.BlockSpec((None, D_MODEL, MOD_TILE), lambda l, j: (l, 0, j)),
            pl.BlockSpec((None, 1, MOD_TILE), lambda l, j: (l, 0, j)),
        ],
        out_specs=pl.BlockSpec((None, n, MOD_TILE), lambda l, j: (l, 0, j)),
        out_shape=jax.ShapeDtypeStruct((DEPTH, n, MOD_COLS), F32),
        compiler_params=pltpu.CompilerParams(vmem_limit_bytes=VMEM_LIMIT),
        name="adaln_mod",
    )(c_all, w_mod, b_mod.reshape(DEPTH, 1, MOD_COLS))


def _inproj_kernel(x_ref, sh_ref, sc_ref, g_ref, cos_ref, sin_ref, dq_ref, dk_ref, lng_ref, lnb_ref,
                   w_ref, z_ref):
    h = (_rms(x_ref[...]) * g_ref[...]) * (1.0 + sc_ref[...]) + sh_ref[...]
    h = h.astype(BF16)
    cos = cos_ref[...]
    sin = sin_ref[...]

    def proj(off, width=D_MODEL):
        return _dot(h, w_ref[:, off:off + width])

    def put(off, val):
        z_ref[:, off:off + val.shape[1]] = val.astype(z_ref.dtype)

    put(OFF_U, _gelu_tanh(proj(OFF_U)))

    v = _gelu_tanh(proj(OFF_V))
    mu = jnp.mean(v, axis=-1, keepdims=True)
    vc = v - mu
    var = jnp.mean(vc * vc, axis=-1, keepdims=True)
    put(OFF_V, vc * lax.rsqrt(var + EPS) * lng_ref[...] + lnb_ref[...])

    half = R_QK_DIM // 2
    for off, d_ref in ((OFF_Q, dq_ref), (OFF_K, dk_ref)):
        acc = proj(off)
        for hd in range(R_HEADS):
            x1 = acc[:, hd * R_QK_DIM:hd * R_QK_DIM + half]
            x2 = acc[:, hd * R_QK_DIM + half:(hd + 1) * R_QK_DIM]
            dec = d_ref[:, hd * LANES:(hd + 1) * LANES]
            put(off + hd * R_QK_DIM, (x1 * cos - x2 * sin) * dec)
            put(off + hd * R_QK_DIM + half, (x1 * sin + x2 * cos) * dec)

    for j in range(2):
        put(OFF_VR + j * D_MODEL, proj(OFF_VR + j * D_MODEL))
    for j in range(2):
        g = proj(OFF_GR + j * D_MODEL)
        put(OFF_GR + j * D_MODEL, g * _sigmoid(g))
    put(OFF_GA, _sigmoid(proj(OFF_GA)))
    put(OFF_GB, _sigmoid(proj(OFF_GB)))


def _inproj_call(x, mod, g1, cos, sin, dq, dk, lng, lnb, w_in, *, per_row, z_dtype):
    rows = x.shape[0]
    tiles_per_seq = None if per_row else (rows // mod.shape[0]) // IN_TILE
    if per_row:
        sh_spec = pl.BlockSpec((IN_TILE, D_MODEL), lambda i: (i, 0))
        sc_spec = pl.BlockSpec((IN_TILE, D_MODEL), lambda i: (i, 1))
        rope_spec = pl.BlockSpec((IN_TILE, LANES), lambda i: (0, 0))
    else:
        sh_spec = pl.BlockSpec((None, 1, D_MODEL), lambda i: (i // tiles_per_seq, 0, 0))
        sc_spec = pl.BlockSpec((None, 1, D_MODEL), lambda i: (i // tiles_per_seq, 0, 1))
        rope_spec = pl.BlockSpec((IN_TILE, LANES), lambda i: (i % tiles_per_seq, 0))
    const = lambda i: (0, 0)
    return pl.pallas_call(
        _inproj_kernel,
        grid=(rows // IN_TILE,),
        in_specs=[
            pl.BlockSpec((IN_TILE, D_MODEL), lambda i: (i, 0)),
            sh_spec, sc_spec,
            pl.BlockSpec((1, D_MODEL), const),
            rope_spec, rope_spec,
            pl.BlockSpec((IN_TILE, R_HEADS * LANES), const),
            pl.BlockSpec((IN_TILE, R_HEADS * LANES), const),
            pl.BlockSpec((1, D_MODEL), const),
            pl.BlockSpec((1, D_MODEL), const),
            pl.BlockSpec((D_MODEL, IN_COLS), const, pipeline_mode=pl.Buffered(1)),
        ],
        out_specs=pl.BlockSpec((IN_TILE, IN_COLS), lambda i: (i, 0)),
        out_shape=jax.ShapeDtypeStruct((rows, IN_COLS), z_dtype),
        compiler_params=pltpu.CompilerParams(vmem_limit_bytes=VMEM_LIMIT),
        name="in_proj",
    )(x, mod, mod, g1, cos, sin, dq, dk, lng, lnb, w_in)


def _ret_sample_kernel(cd_ref, z_q_ref, z_k_ref, z_v_ref, r0_ref, cross_ref, r_ref):
    q = z_q_ref[...].astype(BF16)
    k = z_k_ref[...].astype(BF16)
    v = z_v_ref[...].astype(BF16)
    for hd in range(R_HEADS):
        r = r0_ref[hd]
        qh = q[:, hd * R_QK_DIM:(hd + 1) * R_QK_DIM]
        kh = k[:, hd * R_QK_DIM:(hd + 1) * R_QK_DIM]
        vh = v[:, hd * R_V_DIM:(hd + 1) * R_V_DIM]
        cross_ref[:, hd * R_V_DIM:(hd + 1) * R_V_DIM] = _dot(qh, r.astype(BF16))
        upd = lax.dot_general(kh, vh, (((0,), (0,)), ((), ())), preferred_element_type=F32)
        r_ref[hd] = cd_ref[hd] * (r + upd)


def _ret_sample_call(cd, z3, r0):
    nseq, t, _ = z3.shape
    qk_w = R_HEADS * R_QK_DIM
    v_w = R_HEADS * R_V_DIM
    state_spec = pl.BlockSpec((None, R_HEADS, R_QK_DIM, R_V_DIM), lambda b: (b, 0, 0, 0))
    return pl.pallas_call(
        _ret_sample_kernel,
        grid=(nseq,),
        in_specs=[
            pl.BlockSpec(memory_space=pltpu.SMEM),
            pl.BlockSpec((None, t, qk_w), lambda b: (b, 0, OFF_Q // qk_w)),
            pl.BlockSpec((None, t, qk_w), lambda b: (b, 0, OFF_K // qk_w)),
            pl.BlockSpec((None, t, v_w), lambda b: (b, 0, OFF_VR // v_w)),
            state_spec,
        ],
        out_specs=[
            pl.BlockSpec((None, t, v_w), lambda b: (b, 0, 0)),
            state_spec,
        ],
        out_shape=[
            jax.ShapeDtypeStruct((nseq, t, v_w), F32),
            jax.ShapeDtypeStruct(r0.shape, F32),
        ],
        compiler_params=pltpu.CompilerParams(vmem_limit_bytes=VMEM_LIMIT),
        name="ret_sample",
    )(cd, z3, z3, z3, r0)


def _mix_kernel(cd_ref, z_ref, x_ref, gt_ref, ws_ref, bs_ref, mask_ref, wa_ref, wb_ref, wo_ref, *rest,
                carried_state):
    if carried_state:
        cross_ref, xo_ref = rest
    else:
        xo_ref, r_ref = rest

        @pl.when(pl.program_id(1) == 0)
        def _():
            r_ref[...] = jnp.zeros_like(r_ref)

    def sec(off, width):
        return z_ref[:, off:off + width]

    v = sec(OFF_V, D_MODEL).astype(BF16)
    gdim = D_MODEL // A_GROUPS
    mixed = jnp.concatenate(
        [_dot(ws_ref[g], v[:, g * gdim:(g + 1) * gdim]) for g in range(A_GROUPS)], axis=1)
    a = sec(OFF_U, D_MODEL).astype(F32) * (mixed + bs_ref[...])
    pa = _dot(a.astype(BF16), wa_ref[...])

    mask = mask_ref[...]
    b_parts = []
    for hd in range(R_HEADS):
        qh = sec(OFF_Q + hd * R_QK_DIM, R_QK_DIM).astype(BF16)
        kh = sec(OFF_K + hd * R_QK_DIM, R_QK_DIM).astype(BF16)
        vh = sec(OFF_VR + hd * R_V_DIM, R_V_DIM).astype(BF16)
        s = lax.dot_general(qh, kh, (((1,), (1,)), ((), ())), preferred_element_type=F32) * mask
        y = _dot(s.astype(BF16), vh)
        if carried_state:
            y = y + cross_ref[:, hd * R_V_DIM:(hd + 1) * R_V_DIM]
        else:
            r = r_ref[hd]
            y = y + _dot(qh, r.astype(BF16))
            upd = lax.dot_general(kh, vh, (((0,), (0,)), ((), ())), preferred_element_type=F32)
            r_ref[hd] = cd_ref[hd] * (r + upd)
        mu = jnp.mean(y, axis=-1, keepdims=True)
        yc = y - mu
        var = jnp.mean(yc * yc, axis=-1, keepdims=True)
        yn = yc * lax.rsqrt(var + EPS)
        b_parts.append((sec(OFF_GR + hd * R_V_DIM, R_V_DIM).astype(F32) * yn).astype(BF16))
    pb = _dot(jnp.concatenate(b_parts, axis=1), wb_ref[...])

    merged = sec(OFF_GA, D_MODEL).astype(F32) * pa + sec(OFF_GB, D_MODEL).astype(F32) * pb
    mix = _dot(merged.astype(BF16), wo_ref[...])
    xo_ref[...] = x_ref[...] + gt_ref[...] * mix


def _mix_call(cd, z, x, mod, ws, bs, mask, wa, wb, wo, *, nseq, nchunk, cross=None):
    rows = x.shape[0]
    carried = cross is not None
    row_map = lambda b, c: (b * nchunk + c, 0)
    const2 = lambda b, c: (0, 0)
    if carried:
        gt_spec = pl.BlockSpec((CHUNK, D_MODEL), lambda b, c: (b * nchunk + c, 2))
    else:
        gt_spec = pl.BlockSpec((None, 1, D_MODEL), lambda b, c: (b, 0, 2))
    in_specs = [
        pl.BlockSpec(memory_space=pltpu.SMEM),
        pl.BlockSpec((CHUNK, IN_COLS), row_map),
        pl.BlockSpec((CHUNK, D_MODEL), row_map),
        gt_spec,
        pl.BlockSpec((A_GROUPS, CHUNK, CHUNK), lambda b, c: (0, 0, 0)),
        pl.BlockSpec((CHUNK, D_MODEL), const2),
        pl.BlockSpec((CHUNK, CHUNK), const2),
        pl.BlockSpec((D_MODEL, D_MODEL), const2),
        pl.BlockSpec((R_HEADS * R_V_DIM, D_MODEL), const2),
        pl.BlockSpec((D_MODEL, D_MODEL), const2),
    ]
    args = [cd, z, x, mod, ws, bs, mask, wa, wb, wo]
    x_out = jax.ShapeDtypeStruct((rows, D_MODEL), F32)
    x_spec = pl.BlockSpec((CHUNK, D_MODEL), row_map)
    if carried:
        in_specs.append(pl.BlockSpec((CHUNK, R_HEADS * R_V_DIM), row_map))
        args.append(cross)
        out_specs, out_shape = x_spec, x_out
    else:
        out_specs = [x_spec, pl.BlockSpec((None, R_HEADS, R_QK_DIM, R_V_DIM), lambda b, c: (b, 0, 0, 0))]
        out_shape = [x_out, jax.ShapeDtypeStruct((nseq, R_HEADS, R_QK_DIM, R_V_DIM), F32)]
    return pl.pallas_call(
        functools.partial(_mix_kernel, carried_state=carried),
        grid=(nseq, nchunk),
        in_specs=in_specs,
        out_specs=out_specs,
        out_shape=out_shape,
        compiler_params=pltpu.CompilerParams(
            dimension_semantics=("arbitrary", "arbitrary"), vmem_limit_bytes=VMEM_LIMIT),
        name="token_mix",
    )(*args)


def _moe_kernel(x_ref, sh_ref, sc_ref, gt_ref, g_ref, wrg_ref, brg_ref, wre_ref, bre_ref,
                wg_ref, wu_ref, wd_ref, fg_ref, o_ref, h_scr, comb_scr, acc_scr, *, final_norm):
    e = pl.program_id(1)
    lane = lax.broadcasted_iota(jnp.int32, (MOE_TILE, LANES), 1)

    @pl.when(e == 0)
    def _():
        h = (_rms(x_ref[...]) * g_ref[...]) * (1.0 + sc_ref[...]) + sh_ref[...]
        h_scr[...] = h.astype(BF16)
        gl = _dot_f32(h, wrg_ref[...]) + brg_ref[...]
        gmax = jnp.max(gl, axis=-1, keepdims=True)
        g_w = 1.0 / jnp.sum(jnp.exp(gl - gmax), axis=-1, keepdims=True)
        g_idx = jnp.min(jnp.where(gl == gmax, lane, LANES), axis=-1, keepdims=True)
        el = _dot_f32(h, wre_ref[...]) + bre_ref[...]
        el = jnp.where((lane >> 3) == g_idx, el, NEG)
        m1 = jnp.max(el, axis=-1, keepdims=True)
        i1 = jnp.min(jnp.where(el == m1, lane, LANES), axis=-1, keepdims=True)
        el2 = jnp.where(lane == i1, NEG, el)
        m2 = jnp.max(el2, axis=-1, keepdims=True)
        i2 = jnp.min(jnp.where(el2 == m2, lane, LANES), axis=-1, keepdims=True)
        t = jnp.exp(m2 - m1)
        w1 = g_w / (1.0 + t)
        w2 = g_w * t / (1.0 + t)
        comb_scr[...] = jnp.where(lane == i1, w1, 0.0) + jnp.where(lane == i2, w2, 0.0)
        acc_scr[...] = jnp.zeros_like(acc_scr)

    h = h_scr[...]
    cw = jnp.sum(jnp.where(lane == e, comb_scr[...], 0.0), axis=-1, keepdims=True)
    gate = _dot(h, wg_ref[...])
    up = _dot(h, wu_ref[...])
    hid = gate * _sigmoid(gate) * up * cw
    acc_scr[...] += _dot(hid.astype(BF16), wd_ref[...])

    @pl.when(e == N_EXPERTS - 1)
    def _():
        y = x_ref[...] + gt_ref[...] * acc_scr[...]
        if final_norm:
            y = _rms(y) * fg_ref[...]
        o_ref[...] = y


def _moe_call(x, mod, g2, wrg, brg, wre, bre, wg, wu, wd, fg, *, per_row, final_norm):
    rows = x.shape[0]
    if per_row:
        mspec = lambda s: pl.BlockSpec((MOE_TILE, D_MODEL), lambda i, e: (i, s))
    else:
        tiles_per_seq = (rows // mod.shape[0]) // MOE_TILE
        mspec = lambda s: pl.BlockSpec((None, 1, D_MODEL), lambda i, e: (i // tiles_per_seq, 0, s))
    const = lambda i, e: (0, 0)
    return pl.pallas_call(
        functools.partial(_moe_kernel, final_norm=final_norm),
        grid=(rows // MOE_TILE, N_EXPERTS),
        in_specs=[
            pl.BlockSpec((MOE_TILE, D_MODEL), lambda i, e: (i, 0)),
            mspec(3), mspec(4), mspec(5),
            pl.BlockSpec((1, D_MODEL), const),
            pl.BlockSpec((D_MODEL, LANES), const),
            pl.BlockSpec((1, LANES), const),
            pl.BlockSpec((D_MODEL, LANES), const),
            pl.BlockSpec((1, LANES), const),
            pl.BlockSpec((None, D_MODEL, EXPERT_DIM), lambda i, e: (e, 0, 0)),
            pl.BlockSpec((None, D_MODEL, EXPERT_DIM), lambda i, e: (e, 0, 0)),
            pl.BlockSpec((None, EXPERT_DIM, D_MODEL), lambda i, e: (e, 0, 0)),
            pl.BlockSpec((1, D_MODEL), const),
        ],
        out_specs=pl.BlockSpec((MOE_TILE, D_MODEL), lambda i, e: (i, 0)),
        out_shape=jax.ShapeDtypeStruct((rows, D_MODEL), F32),
        scratch_shapes=[
            pltpu.VMEM((MOE_TILE, D_MODEL), BF16),
            pltpu.VMEM((MOE_TILE, LANES), F32),
            pltpu.VMEM((MOE_TILE, D_MODEL), F32),
        ],
        compiler_params=pltpu.CompilerParams(
            dimension_semantics=("arbitrary", "arbitrary"), vmem_limit_bytes=VMEM_LIMIT),
        name="hier_moe",
    )(x, mod, mod, mod, g2, wrg, brg, wre, bre, wg, wu, wd, fg)


def _rope_tables(pos):
    half = R_QK_DIM // 2
    inv_freq = jnp.power(ROPE_BASE, -jnp.arange(half, dtype=F32) / half)
    ang = pos.astype(F32)[:, None] * inv_freq[None, :]
    return jnp.cos(ang), jnp.sin(ang)


def _decay_tables(chunk):
    log_g = jnp.log1p(-jnp.power(2.0, -5.0 - jnp.arange(R_HEADS, dtype=F32)))
    t1 = (jnp.arange(IN_TILE) % chunk).astype(F32) + 1.0
    dq = jnp.exp(log_g[None, :] * t1[:, None])
    dk = jnp.exp(-log_g[None, :] * t1[:, None]) * (R_QK_DIM ** -0.5)
    cd = jnp.exp(log_g * chunk)
    rep = lambda a: jnp.repeat(a, LANES, axis=1)
    return rep(dq), rep(dk), cd


def _pad_lanes(w, fill):
    return jnp.pad(w, ((0, 0), (0, LANES - w.shape[1])), constant_values=fill)


def kernel(x_prompt, x_sample, state_ret, c_prompt, c_sample, w_mod, b_mod, norm1_g, w_in, ln_v_g, ln_v_b,
           w_s, b_s, w_a_out, w_b_out, w_o, norm2_g, w_router_group, b_router_group, w_router_expert,
           b_router_expert, w_gate, w_up, w_down, final_g):
    n_p, t_p, _ = x_prompt.shape
    n_s, t_s, _ = x_sample.shape
    assert t_p % CHUNK == 0 and (n_s * t_s) % CHUNK == 0 and CHUNK % t_s == 0
    chunks_p = t_p // CHUNK
    chunks_s = (n_s * t_s) // CHUNK

    mod = _mod_call(jnp.concatenate([c_prompt, c_sample], axis=0), w_mod, b_mod)

    cos_p, sin_p = _rope_tables(jnp.arange(t_p, dtype=jnp.int32))
    cos_s, sin_s = _rope_tables(PAST_LEN + jnp.arange(IN_TILE, dtype=jnp.int32) % t_s)
    dq_p, dk_p, cd_p = _decay_tables(CHUNK)
    dq_s, dk_s, cd_s = _decay_tables(t_s)

    idx = jnp.arange(CHUNK)
    causal = idx[:, None] >= idx[None, :]
    same_seq = (idx[:, None] // t_s) == (idx[None, :] // t_s)
    mask_p = causal.astype(F32)
    mask_s = (causal & same_seq).astype(F32)

    xp = x_prompt.reshape(n_p * t_p, D_MODEL)
    xs = x_sample.reshape(n_s * t_s, D_MODEL)
    fg = final_g.reshape(1, D_MODEL)
    r_prompt, r_sample, v_sample = [], [], []
    for l in range(DEPTH):
        mod_p = mod[l, :n_p].reshape(n_p, 1, MOD_COLS)
        mod_s = jnp.repeat(mod[l, n_p:], t_s, axis=0)
        g1 = norm1_g[l].reshape(1, D_MODEL)
        g2 = norm2_g[l].reshape(1, D_MODEL)
        lng = ln_v_g[l].reshape(1, D_MODEL)
        lnb = ln_v_b[l].reshape(1, D_MODEL)
        w_in_l = w_in[l].astype(BF16)
        wa, wb, wo = w_a_out[l].astype(BF16), w_b_out[l].astype(BF16), w_o[l].astype(BF16)

        ws_p = jnp.where(causal[None], w_s[l], 0.0).astype(BF16)
        bs_p = jnp.repeat(b_s[l].T, D_MODEL // A_GROUPS, axis=1)
        blk = jnp.where(causal[:t_s, :t_s][None], w_s[l][:, :t_s, :t_s], 0.0)
        ws_s = jnp.where(same_seq[None], jnp.tile(blk, (1, CHUNK // t_s, CHUNK // t_s)), 0.0).astype(BF16)
        bs_s = jnp.tile(bs_p[:t_s], (CHUNK // t_s, 1))

        zp = _inproj_call(xp, mod_p, g1, cos_p, sin_p, dq_p, dk_p, lng, lnb, w_in_l,
                          per_row=False, z_dtype=BF16)
        zs = _inproj_call(xs, mod_s, g1, cos_s, sin_s, dq_s, dk_s, lng, lnb, w_in_l,
                          per_row=True, z_dtype=F32)

        xp, r_p = _mix_call(cd_p, zp, xp, mod_p, ws_p, bs_p, mask_p, wa, wb, wo,
                            nseq=n_p, nchunk=chunks_p)
        cross, r_s = _ret_sample_call(cd_s, zs.reshape(n_s, t_s, IN_COLS), state_ret[l])
        xs = _mix_call(cd_s, zs, xs, mod_s, ws_s, bs_s, mask_s, wa, wb, wo,
                       nseq=chunks_s, nchunk=1, cross=cross.reshape(n_s * t_s, R_HEADS * R_V_DIM))

        wrg = _pad_lanes(w_router_group[l], 0.0)
        brg = _pad_lanes(b_router_group[l].reshape(1, N_GROUPS), NEG)
        wre = _pad_lanes(w_router_expert[l], 0.0)
        bre = _pad_lanes(b_router_expert[l].reshape(1, N_EXPERTS), NEG)
        wg, wu, wd = w_gate[l].astype(BF16), w_up[l].astype(BF16), w_down[l].astype(BF16)
        last = l == DEPTH - 1
        xp = _moe_call(xp, mod_p, g2, wrg, brg, wre, bre, wg, wu, wd, fg, per_row=False, final_norm=last)
        xs = _moe_call(xs, mod_s, g2, wrg, brg, wre, bre, wg, wu, wd, fg, per_row=True, final_norm=last)

        r_prompt.append(r_p)
        r_sample.append(r_s)
        v_sample.append(zs[:, OFF_V:OFF_V + D_MODEL].reshape(n_s, t_s, D_MODEL))

    return (xp.reshape(n_p, t_p, D_MODEL), xs.reshape(n_s, t_s, D_MODEL),
            jnp.stack(r_prompt), jnp.stack(r_sample), jnp.stack(v_sample))
```

```python
import functools

import jax
import jax.numpy as jnp
from jax import lax
from jax.experimental import pallas as pl
from jax.experimental.pallas import tpu as pltpu

D_MODEL = 1024
DEPTH = 2
PAST_LEN = 16384
A_GROUPS = 8
CHUNK = 128
R_HEADS = 4
R_QK_DIM = 256
R_V_DIM = 512
ROPE_BASE = 10000.0
N_GROUPS = 4
EXPERTS_PER_GROUP = 8
N_EXPERTS = 32
EXPERT_DIM = 256
EPS = 1e-6
IN_COLS = 10 * D_MODEL
MOD_COLS = 6 * D_MODEL

OFF_U, OFF_V, OFF_Q, OFF_K, OFF_VR, OFF_GR, OFF_GA, OFF_GB = 0, 1024, 2048, 3072, 4096, 6144, 8192, 9216

LANES = 128
IN_TILE = 256
MOE_BLOCK = 1024
EXP_TILE = 128
ROW_ALIGN = 16
GATHER_CHUNK = 256
GROUP_ROWS = 8
MOD_TILE = 1536
VMEM_LIMIT = 56 * 1024 * 1024
NEG = -1e30

BF16 = jnp.bfloat16
F32 = jnp.float32


def _dot(a, b):
    return jnp.dot(a, b, preferred_element_type=F32)


def _dot_f32(a, b):
    return jnp.dot(a, b, preferred_element_type=F32, precision=lax.Precision.HIGHEST)


def _sigmoid(x):
    return 1.0 / (1.0 + jnp.exp(-x))


def _gelu_tanh(x):
    return x * (0.5 * (1.0 + jnp.tanh(0.7978845608028654 * (x + 0.044715 * (x * x * x)))))


def _rms(x):
    return x * lax.rsqrt(jnp.mean(x * x, axis=-1, keepdims=True) + EPS)


def _mod_kernel(c_ref, w_ref, b_ref, o_ref):
    c = c_ref[...]
    o_ref[...] = _dot_f32(c * _sigmoid(c), w_ref[...]) + b_ref[...]


def _mod_call(c_all, w_mod, b_mod):
    n = c_all.shape[0]
    return pl.pallas_call(
        _mod_kernel,
        grid=(DEPTH, MOD_COLS // MOD_TILE),
        in_specs=[
            pl.BlockSpec((n, D_MODEL), lambda l, j: (0, 0)),
            pl.BlockSpec((None, D_MODEL, MOD_TILE), lambda l, j: (l, 0, j)),
            pl.BlockSpec((None, 1, MOD_TILE), lambda l, j: (l, 0, j)),
        ],
        out_specs=pl.BlockSpec((None, n, MOD_TILE), lambda l, j: (l, 0, j)),
        out_shape=jax.ShapeDtypeStruct((DEPTH, n, MOD_COLS), F32),
        compiler_params=pltpu.CompilerParams(vmem_limit_bytes=VMEM_LIMIT),
        name="adaln_mod",
    )(c_all, w_mod, b_mod.reshape(DEPTH, 1, MOD_COLS))


def _inproj_kernel(x_ref, sh_ref, sc_ref, g_ref, cos_ref, sin_ref, dq_ref, dk_ref, lng_ref, lnb_ref,
                   w_ref, z_ref):
    h = (_rms(x_ref[...]) * g_ref[...]) * (1.0 + sc_ref[...]) + sh_ref[...]
    h = h.astype(BF16)
    cos = cos_ref[...]
    sin = sin_ref[...]

    def proj(off, width=D_MODEL):
        return _dot(h, w_ref[:, off:off + width])

    def put(off, val):
        z_ref[:, off:off + val.shape[1]] = val.astype(z_ref.dtype)

    put(OFF_U, _gelu_tanh(proj(OFF_U)))

    v = _gelu_tanh(proj(OFF_V))
    mu = jnp.mean(v, axis=-1, keepdims=True)
    vc = v - mu
    var = jnp.mean(vc * vc, axis=-1, keepdims=True)
    put(OFF_V, vc * lax.rsqrt(var + EPS) * lng_ref[...] + lnb_ref[...])

    half = R_QK_DIM // 2
    for off, d_ref in ((OFF_Q, dq_ref), (OFF_K, dk_ref)):
        acc = proj(off)
        for hd in range(R_HEADS):
            x1 = acc[:, hd * R_QK_DIM:hd * R_QK_DIM + half]
            x2 = acc[:, hd * R_QK_DIM + half:(hd + 1) * R_QK_DIM]
            dec = d_ref[:, hd * LANES:(hd + 1) * LANES]
            put(off + hd * R_QK_DIM, (x1 * cos - x2 * sin) * dec)
            put(off + hd * R_QK_DIM + half, (x1 * sin + x2 * cos) * dec)

    for j in range(2):
        put(OFF_VR + j * D_MODEL, proj(OFF_VR + j * D_MODEL))
    for j in range(2):
        g = proj(OFF_GR + j * D_MODEL)
        put(OFF_GR + j * D_MODEL, g * _sigmoid(g))
    put(OFF_GA, _sigmoid(proj(OFF_GA)))
    put(OFF_GB, _sigmoid(proj(OFF_GB)))


def _inproj_call(x, mod, g1, cos, sin, dq, dk, lng, lnb, w_in, *, per_row, z_dtype):
    rows = x.shape[0]
    tiles_per_seq = None if per_row else (rows // mod.shape[0]) // IN_TILE
    if per_row:
        sh_spec = pl.BlockSpec((IN_TILE, D_MODEL), lambda i: (i, 0))
        sc_spec = pl.BlockSpec((IN_TILE, D_MODEL), lambda i: (i, 1))
        rope_spec = pl.BlockSpec((IN_TILE, LANES), lambda i: (0, 0))
    else:
        sh_spec = pl.BlockSpec((None, 1, D_MODEL), lambda i: (i // tiles_per_seq, 0, 0))
        sc_spec = pl.BlockSpec((None, 1, D_MODEL), lambda i: (i // tiles_per_seq, 0, 1))
        rope_spec = pl.BlockSpec((IN_TILE, LANES), lambda i: (i % tiles_per_seq, 0))
    const = lambda i: (0, 0)
    return pl.pallas_call(
        _inproj_kernel,
        grid=(rows // IN_TILE,),
        in_specs=[
            pl.BlockSpec((IN_TILE, D_MODEL), lambda i: (i, 0)),
            sh_spec, sc_spec,
            pl.BlockSpec((1, D_MODEL), const),
            rope_spec, rope_spec,
            pl.BlockSpec((IN_TILE, R_HEADS * LANES), const),
            pl.BlockSpec((IN_TILE, R_HEADS * LANES), const),
            pl.BlockSpec((1, D_MODEL), const),
            pl.BlockSpec((1, D_MODEL), const),
            pl.BlockSpec((D_MODEL, IN_COLS), const, pipeline_mode=pl.Buffered(1)),
        ],
        out_specs=pl.BlockSpec((IN_TILE, IN_COLS), lambda i: (i, 0)),
        out_shape=jax.ShapeDtypeStruct((rows, IN_COLS), z_dtype),
        compiler_params=pltpu.CompilerParams(vmem_limit_bytes=VMEM_LIMIT),
        name="in_proj",
    )(x, mod, mod, g1, cos, sin, dq, dk, lng, lnb, w_in)


def _ret_sample_kernel(cd_ref, z_q_ref, z_k_ref, z_v_ref, r0_ref, *rest):
    cross_ref, r_ref = rest[-2:]
    q = z_q_ref[...].astype(BF16)
    k = z_k_ref[...].astype(BF16)
    v = z_v_ref[...].astype(BF16)
    for hd in range(R_HEADS):
        r = r0_ref[hd]
        qh = q[:, hd * R_QK_DIM:(hd + 1) * R_QK_DIM]
        kh = k[:, hd * R_QK_DIM:(hd + 1) * R_QK_DIM]
        vh = v[:, hd * R_V_DIM:(hd + 1) * R_V_DIM]
        cross_ref[:, hd * R_V_DIM:(hd + 1) * R_V_DIM] = _dot(qh, r.astype(BF16))
        upd = lax.dot_general(kh, vh, (((0,), (0,)), ((), ())), preferred_element_type=F32)
        r_ref[hd] = cd_ref[hd] * (r + upd)


def _ret_sample_call(cd, z3, state, layer, stacked=None):
    nseq, t, _ = z3.shape
    qk_w = R_HEADS * R_QK_DIM
    v_w = R_HEADS * R_V_DIM
    state_spec = pl.BlockSpec((None, None, R_HEADS, R_QK_DIM, R_V_DIM), lambda b: (layer, b, 0, 0, 0))
    in_specs = [
        pl.BlockSpec(memory_space=pltpu.SMEM),
        pl.BlockSpec((None, t, qk_w), lambda b: (b, 0, OFF_Q // qk_w)),
        pl.BlockSpec((None, t, qk_w), lambda b: (b, 0, OFF_K // qk_w)),
        pl.BlockSpec((None, t, v_w), lambda b: (b, 0, OFF_VR // v_w)),
        state_spec,
    ]
    args = [cd, z3, z3, z3, state]
    aliases = {}
    if stacked is not None:
        aliases = {len(args): 1}
        in_specs.append(pl.BlockSpec(memory_space=pl.ANY))
        args.append(stacked)
    return pl.pallas_call(
        _ret_sample_kernel,
        grid=(nseq,),
        in_specs=in_specs,
        out_specs=[
            pl.BlockSpec((None, t, v_w), lambda b: (b, 0, 0)),
            state_spec,
        ],
        out_shape=[
            jax.ShapeDtypeStruct((nseq, t, v_w), F32),
            jax.ShapeDtypeStruct(state.shape, F32),
        ],
        input_output_aliases=aliases,
        compiler_params=pltpu.CompilerParams(vmem_limit_bytes=VMEM_LIMIT),
        name="ret_sample",
    )(*args)


def _mix_kernel(cd_ref, z_ref, x_ref, gt_ref, ws_ref, bs_ref, mask_ref, wa_ref, wb_ref, wo_ref, *rest,
                carried_state):
    if carried_state:
        cross_ref, xo_ref = rest
    else:
        xo_ref, r_ref = rest[-2:]

        @pl.when(pl.program_id(1) == 0)
        def _():
            r_ref[...] = jnp.zeros_like(r_ref)

    def sec(off, width):
        return z_ref[:, off:off + width]

    v = sec(OFF_V, D_MODEL).astype(BF16)
    gdim = D_MODEL // A_GROUPS
    mixed = jnp.concatenate(
        [_dot(ws_ref[g], v[:, g * gdim:(g + 1) * gdim]) for g in range(A_GROUPS)], axis=1)
    a = sec(OFF_U, D_MODEL).astype(F32) * (mixed + bs_ref[...])
    pa = _dot(a.astype(BF16), wa_ref[...])

    mask = mask_ref[...]
    b_parts = []
    for hd in range(R_HEADS):
        qh = sec(OFF_Q + hd * R_QK_DIM, R_QK_DIM).astype(BF16)
        kh = sec(OFF_K + hd * R_QK_DIM, R_QK_DIM).astype(BF16)
        vh = sec(OFF_VR + hd * R_V_DIM, R_V_DIM).astype(BF16)
        s = lax.dot_general(qh, kh, (((1,), (1,)), ((), ())), preferred_element_type=F32) * mask
        y = _dot(s.astype(BF16), vh)
        if carried_state:
            y = y + cross_ref[:, hd * R_V_DIM:(hd + 1) * R_V_DIM]
        else:
            r = r_ref[hd]
            y = y + _dot(qh, r.astype(BF16))
            upd = lax.dot_general(kh, vh, (((0,), (0,)), ((), ())), preferred_element_type=F32)
            r_ref[hd] = cd_ref[hd] * (r + upd)
        mu = jnp.mean(y, axis=-1, keepdims=True)
        yc = y - mu
        var = jnp.mean(yc * yc, axis=-1, keepdims=True)
        yn = yc * lax.rsqrt(var + EPS)
        b_parts.append((sec(OFF_GR + hd * R_V_DIM, R_V_DIM).astype(F32) * yn).astype(BF16))
    pb = _dot(jnp.concatenate(b_parts, axis=1), wb_ref[...])

    merged = sec(OFF_GA, D_MODEL).astype(F32) * pa + sec(OFF_GB, D_MODEL).astype(F32) * pb
    mix = _dot(merged.astype(BF16), wo_ref[...])
    xo_ref[...] = x_ref[...] + gt_ref[...] * mix


def _mix_call(cd, z, x, mod, ws, bs, mask, wa, wb, wo, *, nseq, nchunk, cross=None, layer=0, stacked=None):
    rows = x.shape[0]
    carried = cross is not None
    aliases = {}
    row_map = lambda b, c: (b * nchunk + c, 0)
    const2 = lambda b, c: (0, 0)
    if carried:
        gt_spec = pl.BlockSpec((CHUNK, D_MODEL), lambda b, c: (b * nchunk + c, 2))
    else:
        gt_spec = pl.BlockSpec((None, 1, D_MODEL), lambda b, c: (b, 0, 2))
    in_specs = [
        pl.BlockSpec(memory_space=pltpu.SMEM),
        pl.BlockSpec((CHUNK, IN_COLS), row_map),
        pl.BlockSpec((CHUNK, D_MODEL), row_map),
        gt_spec,
        pl.BlockSpec((A_GROUPS, CHUNK, CHUNK), lambda b, c: (0, 0, 0)),
        pl.BlockSpec((CHUNK, D_MODEL), const2),
        pl.BlockSpec((CHUNK, CHUNK), const2),
        pl.BlockSpec((D_MODEL, D_MODEL), const2),
        pl.BlockSpec((R_HEADS * R_V_DIM, D_MODEL), const2),
        pl.BlockSpec((D_MODEL, D_MODEL), const2),
    ]
    args = [cd, z, x, mod, ws, bs, mask, wa, wb, wo]
    x_out = jax.ShapeDtypeStruct((rows, D_MODEL), F32)
    x_spec = pl.BlockSpec((CHUNK, D_MODEL), row_map)
    if carried:
        in_specs.append(pl.BlockSpec((CHUNK, R_HEADS * R_V_DIM), row_map))
        args.append(cross)
        out_specs, out_shape = x_spec, x_out
    else:
        out_specs = [x_spec, pl.BlockSpec((None, None, R_HEADS, R_QK_DIM, R_V_DIM),
                                          lambda b, c: (layer, b, 0, 0, 0))]
        out_shape = [x_out, jax.ShapeDtypeStruct((DEPTH, nseq, R_HEADS, R_QK_DIM, R_V_DIM), F32)]
        if stacked is not None:
            aliases = {len(args): 1}
            in_specs.append(pl.BlockSpec(memory_space=pl.ANY))
            args.append(stacked)
    return pl.pallas_call(
        functools.partial(_mix_kernel, carried_state=carried),
        grid=(nseq, nchunk),
        in_specs=in_specs,
        out_specs=out_specs,
        out_shape=out_shape,
        input_output_aliases=aliases,
        compiler_params=pltpu.CompilerParams(
            dimension_semantics=("arbitrary", "arbitrary"), vmem_limit_bytes=VMEM_LIMIT),
        name="token_mix",
    )(*args)


def _sorted_rows(block):
    need = 2 * block + N_EXPERTS * ROW_ALIGN + EXP_TILE
    return -(-need // GATHER_CHUNK) * GATHER_CHUNK


def _route_kernel(x_ref, sh_ref, sc_ref, g_ref, wrg_ref, brg_ref, wre_ref, bre_ref, tri_ref, low_ref,
                  h_ref, route_ref, off_ref, nt_ref):
    blk = x_ref.shape[0]
    h = (_rms(x_ref[...]) * g_ref[...]) * (1.0 + sc_ref[...]) + sh_ref[...]
    h_ref[...] = h.astype(BF16)
    nt_dims = (((1,), (1,)), ((), ()))
    hi = lax.Precision.HIGHEST

    gl = lax.dot_general(wrg_ref[...], h, nt_dims, precision=hi, preferred_element_type=F32) + brg_ref[:, :1]
    grow = lax.broadcasted_iota(jnp.int32, (GROUP_ROWS, blk), 0).astype(F32)
    gmax = jnp.max(gl, axis=0, keepdims=True)
    g_w = 1.0 / jnp.sum(jnp.exp(gl - gmax), axis=0, keepdims=True)
    g_idx = jnp.min(jnp.where(gl == gmax, grow, float(GROUP_ROWS)), axis=0, keepdims=True)

    el = lax.dot_general(wre_ref[...], h, nt_dims, precision=hi, preferred_element_type=F32) + bre_ref[:, :1]
    erow_i = lax.broadcasted_iota(jnp.int32, (N_EXPERTS, blk), 0)
    erow = erow_i.astype(F32)
    el = jnp.where((erow_i >> 3).astype(F32) == g_idx, el, NEG)
    m1 = jnp.max(el, axis=0, keepdims=True)
    i1 = jnp.min(jnp.where(el == m1, erow, float(N_EXPERTS)), axis=0, keepdims=True)
    el2 = jnp.where(erow == i1, NEG, el)
    m2 = jnp.max(el2, axis=0, keepdims=True)
    i2 = jnp.min(jnp.where(el2 == m2, erow, float(N_EXPERTS)), axis=0, keepdims=True)
    t = jnp.exp(m2 - m1)
    w1 = g_w / (1.0 + t)
    w2 = g_w * t / (1.0 + t)

    s1 = erow == i1
    s2 = erow == i2
    onehot = jnp.where(s1 | s2, 1.0, 0.0)
    rank = _dot(onehot.astype(BF16), tri_ref[...])
    cnt = jnp.sum(onehot, axis=1, keepdims=True)
    units = jnp.floor((cnt + (ROW_ALIGN - 1.0)) * (1.0 / ROW_ALIGN))
    units = jnp.broadcast_to(units, (N_EXPERTS, LANES))
    off = float(ROW_ALIGN) * _dot(low_ref[...], units.astype(BF16))
    base = off[:, :1] + rank
    pos1 = jnp.sum(jnp.where(s1, base, 0.0), axis=0, keepdims=True)
    pos2 = jnp.sum(jnp.where(s2, base, 0.0), axis=0, keepdims=True)

    r8 = lax.broadcasted_iota(jnp.int32, (8, blk), 0)
    route_ref[...] = jnp.where(r8 == 0, pos1, jnp.where(r8 == 1, pos2, jnp.where(r8 == 2, w1,
                               jnp.where(r8 == 3, w2, 0.0))))
    off_ref[...] = off.astype(jnp.int32)
    tiles = jnp.floor((cnt + (EXP_TILE - 1.0)) * (1.0 / EXP_TILE))
    nt_ref[...] = jnp.broadcast_to(tiles, (N_EXPERTS, LANES)).astype(jnp.int32)


def _experts_kernel(off_sm, nt_sm, x_ref, h_ref, route_ref, gt_ref, wg_ref, wu_ref, wd_ref, fg_ref, o_ref,
                    p_scr, xs_scr, ys_scr, cw_scr, *, final_norm):
    i = pl.program_id(0)
    e = pl.program_id(1)
    nrows, blk = p_scr.shape

    @pl.when(e == 0)
    def _():
        pos1 = route_ref[0:1, :]
        pos2 = route_ref[1:2, :]
        w1 = route_ref[2:3, :]
        w2 = route_ref[3:4, :]
        h = h_ref[...]
        for r0 in range(0, nrows, GATHER_CHUNK):
            prow = (lax.broadcasted_iota(jnp.int32, (GATHER_CHUNK, blk), 0) + r0).astype(F32)
            m1 = prow == pos1
            m2 = prow == pos2
            perm = jnp.where(m1 | m2, 1.0, 0.0).astype(BF16)
            p_scr[r0:r0 + GATHER_CHUNK, :] = perm
            xs_scr[r0:r0 + GATHER_CHUNK, :] = _dot(perm, h).astype(BF16)
            cw = jnp.sum(jnp.where(m1, w1, 0.0) + jnp.where(m2, w2, 0.0), axis=1, keepdims=True)
            cw_scr[r0:r0 + GATHER_CHUNK, :] = jnp.broadcast_to(cw, (GATHER_CHUNK, LANES))
        ys_scr[...] = jnp.zeros_like(ys_scr)

    start = off_sm[i, e]

    def tile(t, carry):
        s = pl.multiple_of(start + t * EXP_TILE, ROW_ALIGN)
        rows = xs_scr[pl.ds(s, EXP_TILE), :]
        gate = _dot(rows, wg_ref[...])
        up = _dot(rows, wu_ref[...])
        cw = cw_scr[pl.ds(s, EXP_TILE), :][:, :1]
        hid = gate * _sigmoid(gate) * up * cw
        ys_scr[pl.ds(s, EXP_TILE), :] = _dot(hid.astype(BF16), wd_ref[...]).astype(BF16)
        return carry

    lax.fori_loop(0, nt_sm[i, e], tile, 0)

    @pl.when(e == N_EXPERTS - 1)
    def _():
        moe = lax.dot_general(p_scr[...], ys_scr[...], (((0,), (0,)), ((), ())), preferred_element_type=F32)
        y = x_ref[...] + gt_ref[...] * moe
        if final_norm:
            y = _rms(y) * fg_ref[...]
        o_ref[...] = y


def _moe_call(x, mod, g2, wrg, brg, wre, bre, wg, wu, wd, fg, *, block, per_row, final_norm):
    rows = x.shape[0]
    nblk = rows // block
    nsorted = _sorted_rows(block)
    if per_row:
        mspec = lambda s: pl.BlockSpec((block, D_MODEL), lambda i, *_: (i, s))
    else:
        blocks_per_seq = (rows // mod.shape[0]) // block
        mspec = lambda s: pl.BlockSpec((None, 1, D_MODEL), lambda i, *_: (i // blocks_per_seq, 0, s))
    const = lambda i: (0, 0)
    ridx = jnp.arange(block)
    tri = (ridx[:, None] < ridx[None, :]).astype(BF16)
    eidx = jnp.arange(N_EXPERTS)
    low = (eidx[None, :] < eidx[:, None]).astype(BF16)
    h, route, off, nt = pl.pallas_call(
        _route_kernel,
        grid=(nblk,),
        in_specs=[
            pl.BlockSpec((block, D_MODEL), lambda i: (i, 0)),
            mspec(3), mspec(4),
            pl.BlockSpec((1, D_MODEL), const),
            pl.BlockSpec((GROUP_ROWS, D_MODEL), const),
            pl.BlockSpec((GROUP_ROWS, LANES), const),
            pl.BlockSpec((N_EXPERTS, D_MODEL), const),
            pl.BlockSpec((N_EXPERTS, LANES), const),
            pl.BlockSpec((block, block), const),
            pl.BlockSpec((N_EXPERTS, N_EXPERTS), const),
        ],
        out_specs=[
            pl.BlockSpec((block, D_MODEL), lambda i: (i, 0)),
            pl.BlockSpec((None, 8, block), lambda i: (i, 0, 0)),
            pl.BlockSpec((None, N_EXPERTS, LANES), lambda i: (i, 0, 0)),
            pl.BlockSpec((None, N_EXPERTS, LANES), lambda i: (i, 0, 0)),
        ],
        out_shape=[
            jax.ShapeDtypeStruct((rows, D_MODEL), BF16),
            jax.ShapeDtypeStruct((nblk, 8, block), F32),
            jax.ShapeDtypeStruct((nblk, N_EXPERTS, LANES), jnp.int32),
            jax.ShapeDtypeStruct((nblk, N_EXPERTS, LANES), jnp.int32),
        ],
        compiler_params=pltpu.CompilerParams(vmem_limit_bytes=VMEM_LIMIT),
        name="moe_route",
    )(x, mod, mod, g2, wrg, brg, wre, bre, tri, low)

    const2 = lambda i, e, *_: (0, 0)
    return pl.pallas_call(
        functools.partial(_experts_kernel, final_norm=final_norm),
        grid_spec=pltpu.PrefetchScalarGridSpec(
            num_scalar_prefetch=2,
            grid=(nblk, N_EXPERTS),
            in_specs=[
                pl.BlockSpec((block, D_MODEL), lambda i, e, *_: (i, 0)),
                pl.BlockSpec((block, D_MODEL), lambda i, e, *_: (i, 0)),
                pl.BlockSpec((None, 8, block), lambda i, e, *_: (i, 0, 0)),
                mspec(5),
                pl.BlockSpec((None, D_MODEL, EXPERT_DIM), lambda i, e, *_: (e, 0, 0)),
                pl.BlockSpec((None, D_MODEL, EXPERT_DIM), lambda i, e, *_: (e, 0, 0)),
                pl.BlockSpec((None, EXPERT_DIM, D_MODEL), lambda i, e, *_: (e, 0, 0)),
                pl.BlockSpec((1, D_MODEL), const2),
            ],
            out_specs=pl.BlockSpec((block, D_MODEL), lambda i, e, *_: (i, 0)),
            scratch_shapes=[
                pltpu.VMEM((nsorted, block), BF16),
                pltpu.VMEM((nsorted, D_MODEL), BF16),
                pltpu.VMEM((nsorted, D_MODEL), BF16),
                pltpu.VMEM((nsorted, LANES), F32),
            ],
        ),
        out_shape=jax.ShapeDtypeStruct((rows, D_MODEL), F32),
        compiler_params=pltpu.CompilerParams(
            dimension_semantics=("arbitrary", "arbitrary"), vmem_limit_bytes=VMEM_LIMIT),
        name="moe_experts",
    )(off[:, :, 0], nt[:, :, 0], x, h, route, mod, wg, wu, wd, fg)


def _rope_tables(pos):
    half = R_QK_DIM // 2
    inv_freq = jnp.power(ROPE_BASE, -jnp.arange(half, dtype=F32) / half)
    ang = pos.astype(F32)[:, None] * inv_freq[None, :]
    return jnp.cos(ang), jnp.sin(ang)


def _decay_tables(chunk):
    log_g = jnp.log1p(-jnp.power(2.0, -5.0 - jnp.arange(R_HEADS, dtype=F32)))
    t1 = (jnp.arange(IN_TILE) % chunk).astype(F32) + 1.0
    dq = jnp.exp(log_g[None, :] * t1[:, None])
    dk = jnp.exp(-log_g[None, :] * t1[:, None]) * (R_QK_DIM ** -0.5)
    cd = jnp.exp(log_g * chunk)
    rep = lambda a: jnp.repeat(a, LANES, axis=1)
    return rep(dq), rep(dk), cd


def kernel(x_prompt, x_sample, state_ret, c_prompt, c_sample, w_mod, b_mod, norm1_g, w_in, ln_v_g, ln_v_b,
           w_s, b_s, w_a_out, w_b_out, w_o, norm2_g, w_router_group, b_router_group, w_router_expert,
           b_router_expert, w_gate, w_up, w_down, final_g):
    n_p, t_p, _ = x_prompt.shape
    n_s, t_s, _ = x_sample.shape
    assert t_p % CHUNK == 0 and (n_s * t_s) % CHUNK == 0 and CHUNK % t_s == 0
    chunks_p = t_p // CHUNK
    chunks_s = (n_s * t_s) // CHUNK

    mod = _mod_call(jnp.concatenate([c_prompt, c_sample], axis=0), w_mod, b_mod)

    cos_p, sin_p = _rope_tables(jnp.arange(t_p, dtype=jnp.int32))
    cos_s, sin_s = _rope_tables(PAST_LEN + jnp.arange(IN_TILE, dtype=jnp.int32) % t_s)
    dq_p, dk_p, cd_p = _decay_tables(CHUNK)
    dq_s, dk_s, cd_s = _decay_tables(t_s)

    idx = jnp.arange(CHUNK)
    causal = idx[:, None] >= idx[None, :]
    same_seq = (idx[:, None] // t_s) == (idx[None, :] // t_s)
    mask_p = causal.astype(F32)
    mask_s = (causal & same_seq).astype(F32)

    xp = x_prompt.reshape(n_p * t_p, D_MODEL)
    xs = x_sample.reshape(n_s * t_s, D_MODEL)
    fg = final_g.reshape(1, D_MODEL)
    r_prompt, r_sample, v_sample = None, None, []
    for l in range(DEPTH):
        mod_p = mod[l, :n_p].reshape(n_p, 1, MOD_COLS)
        mod_s = jnp.repeat(mod[l, n_p:], t_s, axis=0)
        g1 = norm1_g[l].reshape(1, D_MODEL)
        g2 = norm2_g[l].reshape(1, D_MODEL)
        lng = ln_v_g[l].reshape(1, D_MODEL)
        lnb = ln_v_b[l].reshape(1, D_MODEL)
        w_in_l = w_in[l].astype(BF16)
        wa, wb, wo = w_a_out[l].astype(BF16), w_b_out[l].astype(BF16), w_o[l].astype(BF16)

        ws_p = jnp.where(causal[None], w_s[l], 0.0).astype(BF16)
        bs_p = jnp.repeat(b_s[l].T, D_MODEL // A_GROUPS, axis=1)
        blk = jnp.where(causal[:t_s, :t_s][None], w_s[l][:, :t_s, :t_s], 0.0)
        ws_s = jnp.where(same_seq[None], jnp.tile(blk, (1, CHUNK // t_s, CHUNK // t_s)), 0.0).astype(BF16)
        bs_s = jnp.tile(bs_p[:t_s], (CHUNK // t_s, 1))

        zp = _inproj_call(xp, mod_p, g1, cos_p, sin_p, dq_p, dk_p, lng, lnb, w_in_l,
                          per_row=False, z_dtype=BF16)
        zs = _inproj_call(xs, mod_s, g1, cos_s, sin_s, dq_s, dk_s, lng, lnb, w_in_l,
                          per_row=True, z_dtype=F32)

        xp, r_prompt = _mix_call(cd_p, zp, xp, mod_p, ws_p, bs_p, mask_p, wa, wb, wo,
                                 nseq=n_p, nchunk=chunks_p, layer=l, stacked=r_prompt)
        cross, r_sample = _ret_sample_call(cd_s, zs.reshape(n_s, t_s, IN_COLS), state_ret, l, r_sample)
        xs = _mix_call(cd_s, zs, xs, mod_s, ws_s, bs_s, mask_s, wa, wb, wo,
                       nseq=chunks_s, nchunk=1, cross=cross.reshape(n_s * t_s, R_HEADS * R_V_DIM))

        wrg = jnp.pad(w_router_group[l].T, ((0, GROUP_ROWS - N_GROUPS), (0, 0)))
        brg = jnp.pad(b_router_group[l], (0, GROUP_ROWS - N_GROUPS), constant_values=NEG)
        brg = jnp.broadcast_to(brg[:, None], (GROUP_ROWS, LANES))
        wre = w_router_expert[l].T
        bre = jnp.broadcast_to(b_router_expert[l][:, None], (N_EXPERTS, LANES))
        wg, wu, wd = w_gate[l].astype(BF16), w_up[l].astype(BF16), w_down[l].astype(BF16)
        last = l == DEPTH - 1
        xp = _moe_call(xp, mod_p, g2, wrg, brg, wre, bre, wg, wu, wd, fg,
                       block=MOE_BLOCK, per_row=False, final_norm=last)
        xs = _moe_call(xs, mod_s, g2, wrg, brg, wre, bre, wg, wu, wd, fg,
                       block=n_s * t_s, per_row=True, final_norm=last)

        v_sample.append(zs[:, OFF_V:OFF_V + D_MODEL].reshape(n_s, t_s, D_MODEL))

    return (xp.reshape(n_p, t_p, D_MODEL), xs.reshape(n_s, t_s, D_MODEL),
            r_prompt, r_sample, jnp.stack(v_sample))
```

```python
import functools

import jax
import jax.numpy as jnp
from jax import lax
from jax.experimental import pallas as pl
from jax.experimental.pallas import tpu as pltpu

D_MODEL = 1024
DEPTH = 2
PAST_LEN = 16384
A_GROUPS = 8
CHUNK = 128
R_HEADS = 4
R_QK_DIM = 256
R_V_DIM = 512
ROPE_BASE = 10000.0
N_GROUPS = 4
EXPERTS_PER_GROUP = 8
N_EXPERTS = 32
EXPERT_DIM = 256
EPS = 1e-6
IN_COLS = 10 * D_MODEL
MOD_COLS = 6 * D_MODEL

OFF_U, OFF_V, OFF_Q, OFF_K, OFF_VR, OFF_GR, OFF_GA, OFF_GB = 0, 1024, 2048, 3072, 4096, 6144, 8192, 9216

LANES = 128
IN_TILE = 256
MOE_BLOCK = 1024
EXP_TILE = 128
ROW_ALIGN = 16
GATHER_CHUNK = 256
GROUP_ROWS = 8
EXPERTS_PER_STEP = 4
MOD_TILE = 1536
VMEM_LIMIT = 56 * 1024 * 1024
NEG = -1e30

BF16 = jnp.bfloat16
F32 = jnp.float32


def _dot(a, b):
    return jnp.dot(a, b, preferred_element_type=F32)


def _dot_f32(a, b):
    return jnp.dot(a, b, preferred_element_type=F32, precision=lax.Precision.HIGHEST)


def _sigmoid(x):
    return 1.0 / (1.0 + jnp.exp(-x))


def _gelu_tanh(x):
    return x * (0.5 * (1.0 + jnp.tanh(0.7978845608028654 * (x + 0.044715 * (x * x * x)))))


def _rms(x):
    return x * lax.rsqrt(jnp.mean(x * x, axis=-1, keepdims=True) + EPS)


def _mod_kernel(c_ref, w_ref, b_ref, o_ref):
    c = c_ref[...]
    o_ref[...] = _dot_f32(c * _sigmoid(c), w_ref[...]) + b_ref[...]


def _mod_call(c_all, w_mod, b_mod):
    n = c_all.shape[0]
    return pl.pallas_call(
        _mod_kernel,
        grid=(DEPTH, MOD_COLS // MOD_TILE),
        in_specs=[
            pl.BlockSpec((n, D_MODEL), lambda l, j: (0, 0)),
            pl.BlockSpec((None, D_MODEL, MOD_TILE), lambda l, j: (l, 0, j)),
            pl.BlockSpec((None, 1, MOD_TILE), lambda l, j: (l, 0, j)),
        ],
        out_specs=pl.BlockSpec((None, n, MOD_TILE), lambda l, j: (l, 0, j)),
        out_shape=jax.ShapeDtypeStruct((DEPTH, n, MOD_COLS), F32),
        compiler_params=pltpu.CompilerParams(vmem_limit_bytes=VMEM_LIMIT),
        name="adaln_mod",
    )(c_all, w_mod, b_mod.reshape(DEPTH, 1, MOD_COLS))


def _inproj_kernel(x_ref, sh_ref, sc_ref, g_ref, cos_ref, sin_ref, dq_ref, dk_ref, lng_ref, lnb_ref,
                   w_ref, z_ref):
    h = (_rms(x_ref[...]) * g_ref[...]) * (1.0 + sc_ref[...]) + sh_ref[...]
    h = h.astype(BF16)
    cos = cos_ref[...]
    sin = sin_ref[...]

    def proj(off, width=D_MODEL):
        return _dot(h, w_ref[:, off:off + width])

    def put(off, val):
        z_ref[:, off:off + val.shape[1]] = val.astype(z_ref.dtype)

    put(OFF_U, _gelu_tanh(proj(OFF_U)))

    v = _gelu_tanh(proj(OFF_V))
    mu = jnp.mean(v, axis=-1, keepdims=True)
    vc = v - mu
    var = jnp.mean(vc * vc, axis=-1, keepdims=True)
    put(OFF_V, vc * lax.rsqrt(var + EPS) * lng_ref[...] + lnb_ref[...])

    half = R_QK_DIM // 2
    for off, d_ref in ((OFF_Q, dq_ref), (OFF_K, dk_ref)):
        acc = proj(off)
        for hd in range(R_HEADS):
            x1 = acc[:, hd * R_QK_DIM:hd * R_QK_DIM + half]
            x2 = acc[:, hd * R_QK_DIM + half:(hd + 1) * R_QK_DIM]
            dec = d_ref[:, hd * LANES:(hd + 1) * LANES]
            put(off + hd * R_QK_DIM, (x1 * cos - x2 * sin) * dec)
            put(off + hd * R_QK_DIM + half, (x1 * sin + x2 * cos) * dec)

    for j in range(2):
        put(OFF_VR + j * D_MODEL, proj(OFF_VR + j * D_MODEL))
    for j in range(2):
        g = proj(OFF_GR + j * D_MODEL)
        put(OFF_GR + j * D_MODEL, g * _sigmoid(g))
    put(OFF_GA, _sigmoid(proj(OFF_GA)))
    put(OFF_GB, _sigmoid(proj(OFF_GB)))


def _inproj_call(x, mod, g1, cos, sin, dq, dk, lng, lnb, w_in, *, per_row, z_dtype):
    rows = x.shape[0]
    tiles_per_seq = None if per_row else (rows // mod.shape[0]) // IN_TILE
    if per_row:
        sh_spec = pl.BlockSpec((IN_TILE, D_MODEL), lambda i: (i, 0))
        sc_spec = pl.BlockSpec((IN_TILE, D_MODEL), lambda i: (i, 1))
        rope_spec = pl.BlockSpec((IN_TILE, LANES), lambda i: (0, 0))
    else:
        sh_spec = pl.BlockSpec((None, 1, D_MODEL), lambda i: (i // tiles_per_seq, 0, 0))
        sc_spec = pl.BlockSpec((None, 1, D_MODEL), lambda i: (i // tiles_per_seq, 0, 1))
        rope_spec = pl.BlockSpec((IN_TILE, LANES), lambda i: (i % tiles_per_seq, 0))
    const = lambda i: (0, 0)
    return pl.pallas_call(
        _inproj_kernel,
        grid=(rows // IN_TILE,),
        in_specs=[
            pl.BlockSpec((IN_TILE, D_MODEL), lambda i: (i, 0)),
            sh_spec, sc_spec,
            pl.BlockSpec((1, D_MODEL), const),
            rope_spec, rope_spec,
            pl.BlockSpec((IN_TILE, R_HEADS * LANES), const),
            pl.BlockSpec((IN_TILE, R_HEADS * LANES), const),
            pl.BlockSpec((1, D_MODEL), const),
            pl.BlockSpec((1, D_MODEL), const),
            pl.BlockSpec((D_MODEL, IN_COLS), const, pipeline_mode=pl.Buffered(1)),
        ],
        out_specs=pl.BlockSpec((IN_TILE, IN_COLS), lambda i: (i, 0)),
        out_shape=jax.ShapeDtypeStruct((rows, IN_COLS), z_dtype),
        compiler_params=pltpu.CompilerParams(vmem_limit_bytes=VMEM_LIMIT),
        name="in_proj",
    )(x, mod, mod, g1, cos, sin, dq, dk, lng, lnb, w_in)


def _ret_sample_kernel(cd_ref, z_q_ref, z_k_ref, z_v_ref, r0_ref, *rest):
    cross_ref, r_ref = rest[-2:]
    q = z_q_ref[...].astype(BF16)
    k = z_k_ref[...].astype(BF16)
    v = z_v_ref[...].astype(BF16)
    for hd in range(R_HEADS):
        r = r0_ref[hd]
        qh = q[:, hd * R_QK_DIM:(hd + 1) * R_QK_DIM]
        kh = k[:, hd * R_QK_DIM:(hd + 1) * R_QK_DIM]
        vh = v[:, hd * R_V_DIM:(hd + 1) * R_V_DIM]
        cross_ref[:, hd * R_V_DIM:(hd + 1) * R_V_DIM] = _dot(qh, r.astype(BF16))
        upd = lax.dot_general(kh, vh, (((0,), (0,)), ((), ())), preferred_element_type=F32)
        r_ref[hd] = cd_ref[hd] * (r + upd)


def _ret_sample_call(cd, z3, state, layer, stacked=None):
    nseq, t, _ = z3.shape
    qk_w = R_HEADS * R_QK_DIM
    v_w = R_HEADS * R_V_DIM
    state_spec = pl.BlockSpec((None, None, R_HEADS, R_QK_DIM, R_V_DIM), lambda b: (layer, b, 0, 0, 0))
    in_specs = [
        pl.BlockSpec(memory_space=pltpu.SMEM),
        pl.BlockSpec((None, t, qk_w), lambda b: (b, 0, OFF_Q // qk_w)),
        pl.BlockSpec((None, t, qk_w), lambda b: (b, 0, OFF_K // qk_w)),
        pl.BlockSpec((None, t, v_w), lambda b: (b, 0, OFF_VR // v_w)),
        state_spec,
    ]
    args = [cd, z3, z3, z3, state]
    aliases = {}
    if stacked is not None:
        aliases = {len(args): 1}
        in_specs.append(pl.BlockSpec(memory_space=pl.ANY))
        args.append(stacked)
    return pl.pallas_call(
        _ret_sample_kernel,
        grid=(nseq,),
        in_specs=in_specs,
        out_specs=[
            pl.BlockSpec((None, t, v_w), lambda b: (b, 0, 0)),
            state_spec,
        ],
        out_shape=[
            jax.ShapeDtypeStruct((nseq, t, v_w), F32),
            jax.ShapeDtypeStruct(state.shape, F32),
        ],
        input_output_aliases=aliases,
        compiler_params=pltpu.CompilerParams(vmem_limit_bytes=VMEM_LIMIT),
        name="ret_sample",
    )(*args)


def _mix_kernel(cd_ref, z_ref, x_ref, gt_ref, ws_ref, bs_ref, mask_ref, wa_ref, wb_ref, wo_ref, *rest,
                carried_state):
    if carried_state:
        cross_ref, xo_ref = rest
    else:
        xo_ref, r_ref = rest[-2:]

        @pl.when(pl.program_id(1) == 0)
        def _():
            r_ref[...] = jnp.zeros_like(r_ref)

    def sec(off, width):
        return z_ref[:, off:off + width]

    v = sec(OFF_V, D_MODEL).astype(BF16)
    gdim = D_MODEL // A_GROUPS
    mixed = jnp.concatenate(
        [_dot(ws_ref[g], v[:, g * gdim:(g + 1) * gdim]) for g in range(A_GROUPS)], axis=1)
    a = sec(OFF_U, D_MODEL).astype(F32) * (mixed + bs_ref[...])
    pa = _dot(a.astype(BF16), wa_ref[...])

    mask = mask_ref[...]
    b_parts = []
    for hd in range(R_HEADS):
        qh = sec(OFF_Q + hd * R_QK_DIM, R_QK_DIM).astype(BF16)
        kh = sec(OFF_K + hd * R_QK_DIM, R_QK_DIM).astype(BF16)
        vh = sec(OFF_VR + hd * R_V_DIM, R_V_DIM).astype(BF16)
        s = lax.dot_general(qh, kh, (((1,), (1,)), ((), ())), preferred_element_type=F32) * mask
        y = _dot(s.astype(BF16), vh)
        if carried_state:
            y = y + cross_ref[:, hd * R_V_DIM:(hd + 1) * R_V_DIM]
        else:
            r = r_ref[hd]
            y = y + _dot(qh, r.astype(BF16))
            upd = lax.dot_general(kh, vh, (((0,), (0,)), ((), ())), preferred_element_type=F32)
            r_ref[hd] = cd_ref[hd] * (r + upd)
        mu = jnp.mean(y, axis=-1, keepdims=True)
        yc = y - mu
        var = jnp.mean(yc * yc, axis=-1, keepdims=True)
        yn = yc * lax.rsqrt(var + EPS)
        b_parts.append((sec(OFF_GR + hd * R_V_DIM, R_V_DIM).astype(F32) * yn).astype(BF16))
    pb = _dot(jnp.concatenate(b_parts, axis=1), wb_ref[...])

    merged = sec(OFF_GA, D_MODEL).astype(F32) * pa + sec(OFF_GB, D_MODEL).astype(F32) * pb
    mix = _dot(merged.astype(BF16), wo_ref[...])
    xo_ref[...] = x_ref[...] + gt_ref[...] * mix


def _mix_call(cd, z, x, mod, ws, bs, mask, wa, wb, wo, *, nseq, nchunk, cross=None, layer=0, stacked=None):
    rows = x.shape[0]
    carried = cross is not None
    aliases = {}
    row_map = lambda b, c: (b * nchunk + c, 0)
    const2 = lambda b, c: (0, 0)
    if carried:
        gt_spec = pl.BlockSpec((CHUNK, D_MODEL), lambda b, c: (b * nchunk + c, 2))
    else:
        gt_spec = pl.BlockSpec((None, 1, D_MODEL), lambda b, c: (b, 0, 2))
    in_specs = [
        pl.BlockSpec(memory_space=pltpu.SMEM),
        pl.BlockSpec((CHUNK, IN_COLS), row_map),
        pl.BlockSpec((CHUNK, D_MODEL), row_map),
        gt_spec,
        pl.BlockSpec((A_GROUPS, CHUNK, CHUNK), lambda b, c: (0, 0, 0)),
        pl.BlockSpec((CHUNK, D_MODEL), const2),
        pl.BlockSpec((CHUNK, CHUNK), const2),
        pl.BlockSpec((D_MODEL, D_MODEL), const2),
        pl.BlockSpec((R_HEADS * R_V_DIM, D_MODEL), const2),
        pl.BlockSpec((D_MODEL, D_MODEL), const2),
    ]
    args = [cd, z, x, mod, ws, bs, mask, wa, wb, wo]
    x_out = jax.ShapeDtypeStruct((rows, D_MODEL), F32)
    x_spec = pl.BlockSpec((CHUNK, D_MODEL), row_map)
    if carried:
        in_specs.append(pl.BlockSpec((CHUNK, R_HEADS * R_V_DIM), row_map))
        args.append(cross)
        out_specs, out_shape = x_spec, x_out
    else:
        out_specs = [x_spec, pl.BlockSpec((None, None, R_HEADS, R_QK_DIM, R_V_DIM),
                                          lambda b, c: (layer, b, 0, 0, 0))]
        out_shape = [x_out, jax.ShapeDtypeStruct((DEPTH, nseq, R_HEADS, R_QK_DIM, R_V_DIM), F32)]
        if stacked is not None:
            aliases = {len(args): 1}
            in_specs.append(pl.BlockSpec(memory_space=pl.ANY))
            args.append(stacked)
    return pl.pallas_call(
        functools.partial(_mix_kernel, carried_state=carried),
        grid=(nseq, nchunk),
        in_specs=in_specs,
        out_specs=out_specs,
        out_shape=out_shape,
        input_output_aliases=aliases,
        compiler_params=pltpu.CompilerParams(
            dimension_semantics=("arbitrary", "arbitrary"), vmem_limit_bytes=VMEM_LIMIT),
        name="token_mix",
    )(*args)


def _sorted_rows(block):
    need = 2 * block + N_EXPERTS * ROW_ALIGN + EXP_TILE
    return -(-need // GATHER_CHUNK) * GATHER_CHUNK


def _route_kernel(x_ref, sh_ref, sc_ref, g_ref, wrg_ref, brg_ref, wre_ref, bre_ref, tri_ref, low_ref,
                  h_ref, route_ref, off_ref, nt_ref):
    blk = x_ref.shape[0]
    h = (_rms(x_ref[...]) * g_ref[...]) * (1.0 + sc_ref[...]) + sh_ref[...]
    h_ref[...] = h.astype(BF16)
    nt_dims = (((1,), (1,)), ((), ()))
    hi = lax.Precision.HIGHEST

    gl = lax.dot_general(wrg_ref[...], h, nt_dims, precision=hi, preferred_element_type=F32) + brg_ref[:, :1]
    grow = lax.broadcasted_iota(jnp.int32, (GROUP_ROWS, blk), 0).astype(F32)
    gmax = jnp.max(gl, axis=0, keepdims=True)
    g_w = 1.0 / jnp.sum(jnp.exp(gl - gmax), axis=0, keepdims=True)
    g_idx = jnp.min(jnp.where(gl == gmax, grow, float(GROUP_ROWS)), axis=0, keepdims=True)

    el = lax.dot_general(wre_ref[...], h, nt_dims, precision=hi, preferred_element_type=F32) + bre_ref[:, :1]
    erow_i = lax.broadcasted_iota(jnp.int32, (N_EXPERTS, blk), 0)
    erow = erow_i.astype(F32)
    el = jnp.where((erow_i >> 3).astype(F32) == g_idx, el, NEG)
    m1 = jnp.max(el, axis=0, keepdims=True)
    i1 = jnp.min(jnp.where(el == m1, erow, float(N_EXPERTS)), axis=0, keepdims=True)
    el2 = jnp.where(erow == i1, NEG, el)
    m2 = jnp.max(el2, axis=0, keepdims=True)
    i2 = jnp.min(jnp.where(el2 == m2, erow, float(N_EXPERTS)), axis=0, keepdims=True)
    t = jnp.exp(m2 - m1)
    w1 = g_w / (1.0 + t)
    w2 = g_w * t / (1.0 + t)

    s1 = erow == i1
    s2 = erow == i2
    onehot = jnp.where(s1 | s2, 1.0, 0.0)
    rank = _dot(onehot.astype(BF16), tri_ref[...])
    cnt = jnp.sum(onehot, axis=1, keepdims=True)
    units = jnp.floor((cnt + (ROW_ALIGN - 1.0)) * (1.0 / ROW_ALIGN))
    units = jnp.broadcast_to(units, (N_EXPERTS, LANES))
    off = float(ROW_ALIGN) * _dot(low_ref[...], units.astype(BF16))
    base = off[:, :1] + rank
    pos1 = jnp.sum(jnp.where(s1, base, 0.0), axis=0, keepdims=True)
    pos2 = jnp.sum(jnp.where(s2, base, 0.0), axis=0, keepdims=True)

    r8 = lax.broadcasted_iota(jnp.int32, (8, blk), 0)
    route_ref[...] = jnp.where(r8 == 0, pos1, jnp.where(r8 == 1, pos2, jnp.where(r8 == 2, w1,
                               jnp.where(r8 == 3, w2, 0.0))))
    off_ref[...] = off.astype(jnp.int32)
    tiles = jnp.floor((cnt + (EXP_TILE - 1.0)) * (1.0 / EXP_TILE))
    nt_ref[...] = jnp.broadcast_to(tiles, (N_EXPERTS, LANES)).astype(jnp.int32)


def _pack_expert_weights(w_gate, w_up, w_down):
    half = D_MODEL // 2
    gu = jnp.concatenate([w_gate, w_up], axis=2)
    dn = jnp.concatenate([w_down[:, :, :half], w_down[:, :, half:]], axis=1)
    return jnp.concatenate([gu, dn], axis=1).astype(BF16)


def _experts_kernel(off_sm, nt_sm, x_ref, h_ref, route_ref, gt_ref, w_ref, fg_ref, o_ref,
                    p_scr, xs_scr, ys_scr, cw_scr, *, final_norm):
    i = pl.program_id(0)
    step = pl.program_id(1)
    nrows, blk = p_scr.shape

    @pl.when(step == 0)
    def _():
        pos1 = route_ref[0:1, :]
        pos2 = route_ref[1:2, :]
        w1 = route_ref[2:3, :]
        w2 = route_ref[3:4, :]
        row_iota = lax.broadcasted_iota(jnp.int32, (GATHER_CHUNK, blk), 0)

        def gather(c, carry):
            r0 = pl.multiple_of(c * GATHER_CHUNK, GATHER_CHUNK)
            prow = (row_iota + r0).astype(F32)
            m1 = prow == pos1
            m2 = prow == pos2
            perm = jnp.where(m1 | m2, 1.0, 0.0).astype(BF16)
            p_scr[pl.ds(r0, GATHER_CHUNK), :] = perm
            xs_scr[pl.ds(r0, GATHER_CHUNK), :] = _dot(perm, h_ref[...]).astype(BF16)
            cw = jnp.sum(jnp.where(m1, w1, 0.0) + jnp.where(m2, w2, 0.0), axis=1, keepdims=True)
            cw_scr[pl.ds(r0, GATHER_CHUNK), :] = jnp.broadcast_to(cw, (GATHER_CHUNK, LANES))
            ys_scr[pl.ds(r0, GATHER_CHUNK), :] = jnp.zeros((GATHER_CHUNK, D_MODEL), BF16)
            return carry

        lax.fori_loop(0, nrows // GATHER_CHUNK, gather, 0)

    half = D_MODEL // 2

    def tile(g, s):
        s = pl.multiple_of(s, ROW_ALIGN)
        rows = xs_scr[pl.ds(s, EXP_TILE), :]
        gu = _dot(rows, w_ref[g, 0:D_MODEL, :])
        gate = gu[:, :EXPERT_DIM]
        up = gu[:, EXPERT_DIM:]
        cw = cw_scr[pl.ds(s, EXP_TILE), :][:, :1]
        hid = (gate * _sigmoid(gate) * up * cw).astype(BF16)
        lo = _dot(hid, w_ref[g, D_MODEL:D_MODEL + EXPERT_DIM, :]).astype(BF16)
        hi = _dot(hid, w_ref[g, D_MODEL + EXPERT_DIM:, :]).astype(BF16)
        return s, lo, hi

    def put(s, lo, hi):
        ys_scr[pl.ds(s, EXP_TILE), 0:half] = lo
        ys_scr[pl.ds(s, EXP_TILE), half:D_MODEL] = hi

    starts = [off_sm[i, step * EXPERTS_PER_STEP + g] for g in range(EXPERTS_PER_STEP)]
    first = [tile(g, starts[g]) for g in range(EXPERTS_PER_STEP)]
    for g in range(EXPERTS_PER_STEP):
        put(*first[g])

        def more(t, carry, g=g):
            put(*tile(g, starts[g] + t * EXP_TILE))
            return carry

        lax.fori_loop(1, nt_sm[i, step * EXPERTS_PER_STEP + g], more, 0)

    @pl.when(step == N_EXPERTS // EXPERTS_PER_STEP - 1)
    def _():
        tn_dims = (((0,), (0,)), ((), ()))
        for r0 in range(0, blk, GATHER_CHUNK):
            rs = slice(r0, r0 + GATHER_CHUNK)
            moe = lax.dot_general(p_scr[:, rs], ys_scr[...], tn_dims, preferred_element_type=F32)
            gt = gt_ref[...] if gt_ref.shape[0] == 1 else gt_ref[rs, :]
            y = x_ref[rs, :] + gt * moe
            if final_norm:
                y = _rms(y) * fg_ref[...]
            o_ref[rs, :] = y


def _moe_call(x, mod, g2, wrg, brg, wre, bre, w_exp, fg, *, block, per_row, final_norm):
    assert D_MODEL // 2 == 2 * EXPERT_DIM
    rows = x.shape[0]
    nblk = rows // block
    nsorted = _sorted_rows(block)
    if per_row:
        mspec = lambda s: pl.BlockSpec((block, D_MODEL), lambda i, *_: (i, s))
    else:
        blocks_per_seq = (rows // mod.shape[0]) // block
        mspec = lambda s: pl.BlockSpec((None, 1, D_MODEL), lambda i, *_: (i // blocks_per_seq, 0, s))
    const = lambda i: (0, 0)
    ridx = jnp.arange(block)
    tri = (ridx[:, None] < ridx[None, :]).astype(BF16)
    eidx = jnp.arange(N_EXPERTS)
    low = (eidx[None, :] < eidx[:, None]).astype(BF16)
    h, route, off, nt = pl.pallas_call(
        _route_kernel,
        grid=(nblk,),
        in_specs=[
            pl.BlockSpec((block, D_MODEL), lambda i: (i, 0)),
            mspec(3), mspec(4),
            pl.BlockSpec((1, D_MODEL), const),
            pl.BlockSpec((GROUP_ROWS, D_MODEL), const),
            pl.BlockSpec((GROUP_ROWS, LANES), const),
            pl.BlockSpec((N_EXPERTS, D_MODEL), const),
            pl.BlockSpec((N_EXPERTS, LANES), const),
            pl.BlockSpec((block, block), const),
            pl.BlockSpec((N_EXPERTS, N_EXPERTS), const),
        ],
        out_specs=[
            pl.BlockSpec((block, D_MODEL), lambda i: (i, 0)),
            pl.BlockSpec((None, 8, block), lambda i: (i, 0, 0)),
            pl.BlockSpec((None, N_EXPERTS, LANES), lambda i: (i, 0, 0)),
            pl.BlockSpec((None, N_EXPERTS, LANES), lambda i: (i, 0, 0)),
        ],
        out_shape=[
            jax.ShapeDtypeStruct((rows, D_MODEL), BF16),
            jax.ShapeDtypeStruct((nblk, 8, block), F32),
            jax.ShapeDtypeStruct((nblk, N_EXPERTS, LANES), jnp.int32),
            jax.ShapeDtypeStruct((nblk, N_EXPERTS, LANES), jnp.int32),
        ],
        compiler_params=pltpu.CompilerParams(vmem_limit_bytes=VMEM_LIMIT),
        name="moe_route",
    )(x, mod, mod, g2, wrg, brg, wre, bre, tri, low)

    const2 = lambda i, e, *_: (0, 0)
    return pl.pallas_call(
        functools.partial(_experts_kernel, final_norm=final_norm),
        grid_spec=pltpu.PrefetchScalarGridSpec(
            num_scalar_prefetch=2,
            grid=(nblk, N_EXPERTS // EXPERTS_PER_STEP),
            in_specs=[
                pl.BlockSpec((block, D_MODEL), lambda i, e, *_: (i, 0)),
                pl.BlockSpec((block, D_MODEL), lambda i, e, *_: (i, 0), pipeline_mode=pl.Buffered(1)),
                pl.BlockSpec((None, 8, block), lambda i, e, *_: (i, 0, 0)),
                mspec(5),
                pl.BlockSpec((EXPERTS_PER_STEP, D_MODEL + 2 * EXPERT_DIM, 2 * EXPERT_DIM),
                             lambda i, s, *_: (s, 0, 0)),
                pl.BlockSpec((1, D_MODEL), const2),
            ],
            out_specs=pl.BlockSpec((block, D_MODEL), lambda i, e, *_: (i, 0)),
            scratch_shapes=[
                pltpu.VMEM((nsorted, block), BF16),
                pltpu.VMEM((nsorted, D_MODEL), BF16),
                pltpu.VMEM((nsorted, D_MODEL), BF16),
                pltpu.VMEM((nsorted, LANES), F32),
            ],
        ),
        out_shape=jax.ShapeDtypeStruct((rows, D_MODEL), F32),
        compiler_params=pltpu.CompilerParams(
            dimension_semantics=("arbitrary", "arbitrary"), vmem_limit_bytes=VMEM_LIMIT),
        name="moe_experts",
    )(off[:, :, 0], nt[:, :, 0], x, h, route, mod, w_exp, fg)


def _rope_tables(pos):
    half = R_QK_DIM // 2
    inv_freq = jnp.power(ROPE_BASE, -jnp.arange(half, dtype=F32) / half)
    ang = pos.astype(F32)[:, None] * inv_freq[None, :]
    return jnp.cos(ang), jnp.sin(ang)


def _decay_tables(chunk):
    log_g = jnp.log1p(-jnp.power(2.0, -5.0 - jnp.arange(R_HEADS, dtype=F32)))
    t1 = (jnp.arange(IN_TILE) % chunk).astype(F32) + 1.0
    dq = jnp.exp(log_g[None, :] * t1[:, None])
    dk = jnp.exp(-log_g[None, :] * t1[:, None]) * (R_QK_DIM ** -0.5)
    cd = jnp.exp(log_g * chunk)
    rep = lambda a: jnp.repeat(a, LANES, axis=1)
    return rep(dq), rep(dk), cd


def kernel(x_prompt, x_sample, state_ret, c_prompt, c_sample, w_mod, b_mod, norm1_g, w_in, ln_v_g, ln_v_b,
           w_s, b_s, w_a_out, w_b_out, w_o, norm2_g, w_router_group, b_router_group, w_router_expert,
           b_router_expert, w_gate, w_up, w_down, final_g):
    n_p, t_p, _ = x_prompt.shape
    n_s, t_s, _ = x_sample.shape
    assert t_p % CHUNK == 0 and (n_s * t_s) % CHUNK == 0 and CHUNK % t_s == 0
    chunks_p = t_p // CHUNK
    chunks_s = (n_s * t_s) // CHUNK

    mod = _mod_call(jnp.concatenate([c_prompt, c_sample], axis=0), w_mod, b_mod)

    cos_p, sin_p = _rope_tables(jnp.arange(t_p, dtype=jnp.int32))
    cos_s, sin_s = _rope_tables(PAST_LEN + jnp.arange(IN_TILE, dtype=jnp.int32) % t_s)
    dq_p, dk_p, cd_p = _decay_tables(CHUNK)
    dq_s, dk_s, cd_s = _decay_tables(t_s)

    idx = jnp.arange(CHUNK)
    causal = idx[:, None] >= idx[None, :]
    same_seq = (idx[:, None] // t_s) == (idx[None, :] // t_s)
    mask_p = causal.astype(F32)
    mask_s = (causal & same_seq).astype(F32)

    xp = x_prompt.reshape(n_p * t_p, D_MODEL)
    xs = x_sample.reshape(n_s * t_s, D_MODEL)
    fg = final_g.reshape(1, D_MODEL)
    r_prompt, r_sample, v_sample = None, None, []
    for l in range(DEPTH):
        mod_p = mod[l, :n_p].reshape(n_p, 1, MOD_COLS)
        mod_s = jnp.repeat(mod[l, n_p:], t_s, axis=0)
        g1 = norm1_g[l].reshape(1, D_MODEL)
        g2 = norm2_g[l].reshape(1, D_MODEL)
        lng = ln_v_g[l].reshape(1, D_MODEL)
        lnb = ln_v_b[l].reshape(1, D_MODEL)
        w_in_l = w_in[l].astype(BF16)
        wa, wb, wo = w_a_out[l].astype(BF16), w_b_out[l].astype(BF16), w_o[l].astype(BF16)

        ws_p = jnp.where(causal[None], w_s[l], 0.0).astype(BF16)
        bs_p = jnp.repeat(b_s[l].T, D_MODEL // A_GROUPS, axis=1)
        blk = jnp.where(causal[:t_s, :t_s][None], w_s[l][:, :t_s, :t_s], 0.0)
        ws_s = jnp.where(same_seq[None], jnp.tile(blk, (1, CHUNK // t_s, CHUNK // t_s)), 0.0).astype(BF16)
        bs_s = jnp.tile(bs_p[:t_s], (CHUNK // t_s, 1))

        zp = _inproj_call(xp, mod_p, g1, cos_p, sin_p, dq_p, dk_p, lng, lnb, w_in_l,
                          per_row=False, z_dtype=BF16)
        zs = _inproj_call(xs, mod_s, g1, cos_s, sin_s, dq_s, dk_s, lng, lnb, w_in_l,
                          per_row=True, z_dtype=F32)

        xp, r_prompt = _mix_call(cd_p, zp, xp, mod_p, ws_p, bs_p, mask_p, wa, wb, wo,
                                 nseq=n_p, nchunk=chunks_p, layer=l, stacked=r_prompt)
        cross, r_sample = _ret_sample_call(cd_s, zs.reshape(n_s, t_s, IN_COLS), state_ret, l, r_sample)
        xs = _mix_call(cd_s, zs, xs, mod_s, ws_s, bs_s, mask_s, wa, wb, wo,
                       nseq=chunks_s, nchunk=1, cross=cross.reshape(n_s * t_s, R_HEADS * R_V_DIM))

        wrg = jnp.pad(w_router_group[l].T, ((0, GROUP_ROWS - N_GROUPS), (0, 0)))
        brg = jnp.pad(b_router_group[l], (0, GROUP_ROWS - N_GROUPS), constant_values=NEG)
        brg = jnp.broadcast_to(brg[:, None], (GROUP_ROWS, LANES))
        wre = w_router_expert[l].T
        bre = jnp.broadcast_to(b_router_expert[l][:, None], (N_EXPERTS, LANES))
        w_exp = _pack_expert_weights(w_gate[l], w_up[l], w_down[l])
        last = l == DEPTH - 1
        xp = _moe_call(xp, mod_p, g2, wrg, brg, wre, bre, w_exp, fg,
                       block=MOE_BLOCK, per_row=False, final_norm=last)
        xs = _moe_call(xs, mod_s, g2, wrg, brg, wre, bre, w_exp, fg,
                       block=n_s * t_s, per_row=True, final_norm=last)

        v_sample.append(zs[:, OFF_V:OFF_V + D_MODEL].reshape(n_s, t_s, D_MODEL))

    return (xp.reshape(n_p, t_p, D_MODEL), xs.reshape(n_s, t_s, D_MODEL),
            r_prompt, r_sample, jnp.stack(v_sample))
```

```python
import functools

import jax
import jax.numpy as jnp
from jax import lax
from jax.experimental import pallas as pl
from jax.experimental.pallas import tpu as pltpu

D_MODEL = 1024
DEPTH = 2
PAST_LEN = 16384
A_GROUPS = 8
CHUNK = 128
R_HEADS = 4
R_QK_DIM = 256
R_V_DIM = 512
ROPE_BASE = 10000.0
N_GROUPS = 4
EXPERTS_PER_GROUP = 8
N_EXPERTS = 32
EXPERT_DIM = 256
EPS = 1e-6
IN_COLS = 10 * D_MODEL
MOD_COLS = 6 * D_MODEL

OFF_U, OFF_V, OFF_Q, OFF_K, OFF_VR, OFF_GR, OFF_GA, OFF_GB = 0, 1024, 2048, 3072, 4096, 6144, 8192, 9216

LANES = 128
IN_TILE = 256
MIX_SEQS = 2
RET_ROWS = 8
MOE_BLOCK = 1024
EXP_TILE = 128
ROW_ALIGN = 16
GATHER_CHUNK = 256
GROUP_ROWS = 8
ROUTER_ROWS = 48
EXPERTS_PER_STEP = 4
MOD_TILE = 1536
VMEM_LIMIT = 56 * 1024 * 1024
NEG = -1e30

BF16 = jnp.bfloat16
F32 = jnp.float32


def _dot(a, b):
    return jnp.dot(a, b, preferred_element_type=F32)


def _dot_f32(a, b):
    return jnp.dot(a, b, preferred_element_type=F32, precision=lax.Precision.HIGHEST)


def _sigmoid(x):
    return 1.0 / (1.0 + jnp.exp(-x))


def _gelu_tanh(x):
    return x * (0.5 * (1.0 + jnp.tanh(0.7978845608028654 * (x + 0.044715 * (x * x * x)))))


def _rms(x):
    return x * lax.rsqrt(jnp.mean(x * x, axis=-1, keepdims=True) + EPS)


def _mod_kernel(c_ref, w_ref, b_ref, o_ref):
    c = c_ref[...]
    o_ref[...] = _dot_f32(c * _sigmoid(c), w_ref[...]) + b_ref[...]


def _mod_call(c_all, w_mod, b_mod):
    n = c_all.shape[0]
    return pl.pallas_call(
        _mod_kernel,
        grid=(DEPTH, MOD_COLS // MOD_TILE),
        in_specs=[
            pl.BlockSpec((n, D_MODEL), lambda l, j: (0, 0)),
            pl.BlockSpec((None, D_MODEL, MOD_TILE), lambda l, j: (l, 0, j)),
            pl.BlockSpec((None, 1, MOD_TILE), lambda l, j: (l, 0, j)),
        ],
        out_specs=pl.BlockSpec((None, n, MOD_TILE), lambda l, j: (l, 0, j)),
        out_shape=jax.ShapeDtypeStruct((DEPTH, n, MOD_COLS), F32),
        compiler_params=pltpu.CompilerParams(vmem_limit_bytes=VMEM_LIMIT),
        name="adaln_mod",
    )(c_all, w_mod, b_mod.reshape(DEPTH, 1, MOD_COLS))


def _inproj_kernel(x_ref, sh_ref, sc_ref, g_ref, cos_ref, sin_ref, dq_ref, dk_ref, lng_ref, lnb_ref,
                   w_ref, z_ref):
    h = (_rms(x_ref[...]) * g_ref[...]) * (1.0 + sc_ref[...]) + sh_ref[...]
    h = h.astype(BF16)
    cos = cos_ref[...]
    sin = sin_ref[...]

    def proj(off, width=D_MODEL):
        return _dot(h, w_ref[:, off:off + width])

    def put(off, val):
        z_ref[:, off:off + val.shape[1]] = val.astype(z_ref.dtype)

    put(OFF_U, _gelu_tanh(proj(OFF_U)))

    v = _gelu_tanh(proj(OFF_V))
    mu = jnp.mean(v, axis=-1, keepdims=True)
    vc = v - mu
    var = jnp.mean(vc * vc, axis=-1, keepdims=True)
    put(OFF_V, vc * lax.rsqrt(var + EPS) * lng_ref[...] + lnb_ref[...])

    half = R_QK_DIM // 2
    for off, d_ref in ((OFF_Q, dq_ref), (OFF_K, dk_ref)):
        acc = proj(off)
        for hd in range(R_HEADS):
            x1 = acc[:, hd * R_QK_DIM:hd * R_QK_DIM + half]
            x2 = acc[:, hd * R_QK_DIM + half:(hd + 1) * R_QK_DIM]
            dec = d_ref[:, hd * LANES:(hd + 1) * LANES]
            put(off + hd * R_QK_DIM, (x1 * cos - x2 * sin) * dec)
            put(off + hd * R_QK_DIM + half, (x1 * sin + x2 * cos) * dec)

    for j in range(2):
        put(OFF_VR + j * D_MODEL, proj(OFF_VR + j * D_MODEL))
    for j in range(2):
        g = proj(OFF_GR + j * D_MODEL)
        put(OFF_GR + j * D_MODEL, g * _sigmoid(g))
    put(OFF_GA, _sigmoid(proj(OFF_GA)))
    put(OFF_GB, _sigmoid(proj(OFF_GB)))


def _inproj_call(x, mod, g1, cos, sin, dq, dk, lng, lnb, w_in, *, per_row, z_dtype):
    rows = x.shape[0]
    tiles_per_seq = None if per_row else (rows // mod.shape[0]) // IN_TILE
    if per_row:
        sh_spec = pl.BlockSpec((IN_TILE, D_MODEL), lambda i: (i, 0))
        sc_spec = pl.BlockSpec((IN_TILE, D_MODEL), lambda i: (i, 1))
        rope_spec = pl.BlockSpec((IN_TILE, LANES), lambda i: (0, 0))
    else:
        sh_spec = pl.BlockSpec((None, 1, D_MODEL), lambda i: (i // tiles_per_seq, 0, 0))
        sc_spec = pl.BlockSpec((None, 1, D_MODEL), lambda i: (i // tiles_per_seq, 0, 1))
        rope_spec = pl.BlockSpec((IN_TILE, LANES), lambda i: (i % tiles_per_seq, 0))
    const = lambda i: (0, 0)
    return pl.pallas_call(
        _inproj_kernel,
        grid=(rows // IN_TILE,),
        in_specs=[
            pl.BlockSpec((IN_TILE, D_MODEL), lambda i: (i, 0)),
            sh_spec, sc_spec,
            pl.BlockSpec((1, D_MODEL), const),
            rope_spec, rope_spec,
            pl.BlockSpec((IN_TILE, R_HEADS * LANES), const),
            pl.BlockSpec((IN_TILE, R_HEADS * LANES), const),
            pl.BlockSpec((1, D_MODEL), const),
            pl.BlockSpec((1, D_MODEL), const),
            pl.BlockSpec((D_MODEL, IN_COLS), const, pipeline_mode=pl.Buffered(1)),
        ],
        out_specs=pl.BlockSpec((IN_TILE, IN_COLS), lambda i: (i, 0)),
        out_shape=jax.ShapeDtypeStruct((rows, IN_COLS), z_dtype),
        compiler_params=pltpu.CompilerParams(vmem_limit_bytes=VMEM_LIMIT),
        name="in_proj",
    )(x, mod, mod, g1, cos, sin, dq, dk, lng, lnb, w_in)


def _ret_sample_kernel(cd_ref, z_q_ref, z_k_ref, z_v_ref, r0_ref, *rest, t_s):
    cross_ref, r_ref = rest[-2:]
    q = z_q_ref[...].astype(BF16)
    k = z_k_ref[...]
    v = z_v_ref[...].astype(BF16)
    row_k = lax.broadcasted_iota(jnp.int32, (RET_ROWS, R_QK_DIM), 0)
    row_v = lax.broadcasted_iota(jnp.int32, (RET_ROWS, R_V_DIM), 0)
    for hd in range(R_HEADS):
        qh = q[:, hd * R_QK_DIM:(hd + 1) * R_QK_DIM]
        kh = k[:, hd * R_QK_DIM:(hd + 1) * R_QK_DIM]
        vh = v[:, hd * R_V_DIM:(hd + 1) * R_V_DIM]
        cross = jnp.zeros((RET_ROWS, R_V_DIM), F32)
        for s in range(RET_ROWS // t_s):
            r = r0_ref[s, hd]
            lo, hi = s * t_s, (s + 1) * t_s
            cross = jnp.where((row_v >= lo) & (row_v < hi), _dot(qh, r.astype(BF16)), cross)
            k_seq = jnp.where((row_k >= lo) & (row_k < hi), kh, 0.0).astype(BF16)
            upd = lax.dot_general(k_seq, vh, (((0,), (0,)), ((), ())), preferred_element_type=F32)
            r_ref[s, hd] = cd_ref[hd] * (r + upd)
        cross_ref[:, hd * R_V_DIM:(hd + 1) * R_V_DIM] = cross


def _ret_sample_call(cd, z, state, layer, t_s, stacked=None):
    rows = z.shape[0]
    seqs = RET_ROWS // t_s
    qk_w = R_HEADS * R_QK_DIM
    v_w = R_HEADS * R_V_DIM
    state_spec = pl.BlockSpec((None, seqs, R_HEADS, R_QK_DIM, R_V_DIM), lambda b: (layer, b, 0, 0, 0))
    in_specs = [
        pl.BlockSpec(memory_space=pltpu.SMEM),
        pl.BlockSpec((RET_ROWS, qk_w), lambda b: (b, OFF_Q // qk_w)),
        pl.BlockSpec((RET_ROWS, qk_w), lambda b: (b, OFF_K // qk_w)),
        pl.BlockSpec((RET_ROWS, v_w), lambda b: (b, OFF_VR // v_w)),
        state_spec,
    ]
    args = [cd, z, z, z, state]
    aliases = {}
    if stacked is not None:
        aliases = {len(args): 1}
        in_specs.append(pl.BlockSpec(memory_space=pl.ANY))
        args.append(stacked)
    return pl.pallas_call(
        functools.partial(_ret_sample_kernel, t_s=t_s),
        grid=(rows // RET_ROWS,),
        in_specs=in_specs,
        out_specs=[
            pl.BlockSpec((RET_ROWS, v_w), lambda b: (b, 0)),
            state_spec,
        ],
        out_shape=[
            jax.ShapeDtypeStruct((rows, v_w), F32),
            jax.ShapeDtypeStruct(state.shape, F32),
        ],
        input_output_aliases=aliases,
        compiler_params=pltpu.CompilerParams(vmem_limit_bytes=VMEM_LIMIT),
        name="ret_sample",
    )(*args)


def _mix_kernel(cd_ref, z_ref, x_ref, gt_ref, ws_ref, bs_ref, mask_ref, wa_ref, wb_ref, wo_ref, *rest,
                carried_state):
    nseq = z_ref.shape[0]
    if carried_state:
        cross_ref, xo_ref = rest
    else:
        xo_ref, r_ref = rest[-2:]

        @pl.when(pl.program_id(1) == 0)
        def _():
            r_ref[...] = jnp.zeros_like(r_ref)

    mask = mask_ref[...]
    gdim = D_MODEL // A_GROUPS
    a_rows, b_rows = [], []
    for sq in range(nseq):
        def sec(off, width, sq=sq):
            return z_ref[sq, :, off:off + width]

        v = sec(OFF_V, D_MODEL).astype(BF16)
        mixed = jnp.concatenate(
            [_dot(ws_ref[g], v[:, g * gdim:(g + 1) * gdim]) for g in range(A_GROUPS)], axis=1)
        a_rows.append((sec(OFF_U, D_MODEL).astype(F32) * (mixed + bs_ref[...])).astype(BF16))

        b_parts = []
        for hd in range(R_HEADS):
            qh = sec(OFF_Q + hd * R_QK_DIM, R_QK_DIM).astype(BF16)
            kh = sec(OFF_K + hd * R_QK_DIM, R_QK_DIM).astype(BF16)
            vh = sec(OFF_VR + hd * R_V_DIM, R_V_DIM).astype(BF16)
            s = lax.dot_general(qh, kh, (((1,), (1,)), ((), ())), preferred_element_type=F32) * mask
            y = _dot(s.astype(BF16), vh)
            if carried_state:
                y = y + cross_ref[sq, :, hd * R_V_DIM:(hd + 1) * R_V_DIM]
            else:
                r = r_ref[sq, hd]
                y = y + _dot(qh, r.astype(BF16))
                upd = lax.dot_general(kh, vh, (((0,), (0,)), ((), ())), preferred_element_type=F32)
                r_ref[sq, hd] = cd_ref[hd] * (r + upd)
            mu = jnp.mean(y, axis=-1, keepdims=True)
            yc = y - mu
            var = jnp.mean(yc * yc, axis=-1, keepdims=True)
            yn = yc * lax.rsqrt(var + EPS)
            b_parts.append((sec(OFF_GR + hd * R_V_DIM, R_V_DIM).astype(F32) * yn).astype(BF16))
        b_rows.append(jnp.concatenate(b_parts, axis=1))

    pa = _dot(jnp.concatenate(a_rows, axis=0), wa_ref[...])
    pb = _dot(jnp.concatenate(b_rows, axis=0), wb_ref[...])
    ga = z_ref[:, :, OFF_GA:OFF_GA + D_MODEL].reshape(nseq * CHUNK, D_MODEL).astype(F32)
    gb = z_ref[:, :, OFF_GB:OFF_GB + D_MODEL].reshape(nseq * CHUNK, D_MODEL).astype(F32)
    mix = _dot((ga * pa + gb * pb).astype(BF16), wo_ref[...])
    xo_ref[...] = x_ref[...] + gt_ref[...] * mix.reshape(nseq, CHUNK, D_MODEL)


def _mix_call(cd, z, x, mod, ws, bs, mask, wa, wb, wo, *, nseq, nchunk, cross=None, layer=0, stacked=None):
    rows = x.shape[0]
    carried = cross is not None
    aliases = {}
    assert nseq % MIX_SEQS == 0
    seq_len = nchunk * CHUNK
    view = lambda a: a.reshape(nseq, seq_len, a.shape[-1])
    row_map = lambda b, c: (b, c, 0)
    const2 = lambda b, c: (0, 0)
    if carried:
        mod = view(mod)
        gt_spec = pl.BlockSpec((MIX_SEQS, CHUNK, D_MODEL), lambda b, c: (b, c, 2))
    else:
        gt_spec = pl.BlockSpec((MIX_SEQS, 1, D_MODEL), lambda b, c: (b, 0, 2))
    in_specs = [
        pl.BlockSpec(memory_space=pltpu.SMEM),
        pl.BlockSpec((MIX_SEQS, CHUNK, IN_COLS), row_map),
        pl.BlockSpec((MIX_SEQS, CHUNK, D_MODEL), row_map),
        gt_spec,
        pl.BlockSpec((A_GROUPS, CHUNK, CHUNK), lambda b, c: (0, 0, 0)),
        pl.BlockSpec((CHUNK, D_MODEL), const2),
        pl.BlockSpec((CHUNK, CHUNK), const2),
        pl.BlockSpec((D_MODEL, D_MODEL), const2),
        pl.BlockSpec((R_HEADS * R_V_DIM, D_MODEL), const2),
        pl.BlockSpec((D_MODEL, D_MODEL), const2),
    ]
    args = [cd, view(z), view(x), mod, ws, bs, mask, wa, wb, wo]
    x_out = jax.ShapeDtypeStruct((nseq, seq_len, D_MODEL), F32)
    x_spec = pl.BlockSpec((MIX_SEQS, CHUNK, D_MODEL), row_map)
    if carried:
        in_specs.append(pl.BlockSpec((MIX_SEQS, CHUNK, R_HEADS * R_V_DIM), row_map))
        args.append(view(cross))
        out_specs, out_shape = x_spec, x_out
    else:
        out_specs = [x_spec, pl.BlockSpec((None, MIX_SEQS, R_HEADS, R_QK_DIM, R_V_DIM),
                                          lambda b, c: (layer, b, 0, 0, 0))]
        out_shape = [x_out, jax.ShapeDtypeStruct((DEPTH, nseq, R_HEADS, R_QK_DIM, R_V_DIM), F32)]
        if stacked is not None:
            aliases = {len(args): 1}
            in_specs.append(pl.BlockSpec(memory_space=pl.ANY))
            args.append(stacked)
    out = pl.pallas_call(
        functools.partial(_mix_kernel, carried_state=carried),
        grid=(nseq // MIX_SEQS, nchunk),
        in_specs=in_specs,
        out_specs=out_specs,
        out_shape=out_shape,
        input_output_aliases=aliases,
        compiler_params=pltpu.CompilerParams(
            dimension_semantics=("arbitrary", "arbitrary"), vmem_limit_bytes=VMEM_LIMIT),
        name="token_mix",
    )(*args)
    if carried:
        return out.reshape(rows, D_MODEL)
    return out[0].reshape(rows, D_MODEL), out[1]


def _sorted_rows(block):
    need = 2 * block + N_EXPERTS * ROW_ALIGN + EXP_TILE
    return -(-need // GATHER_CHUNK) * GATHER_CHUNK


def _split_bf16(w):
    hi = w.astype(BF16)
    return hi, (w - hi.astype(F32)).astype(BF16)


def _route_kernel(x_ref, sh_ref, sc_ref, g_ref, wr2_ref, br_ref, tri_ref, low_ref,
                  h_ref, route_ref, off_ref, nt_ref):
    blk = x_ref.shape[0]
    h = (_rms(x_ref[...]) * g_ref[...]) * (1.0 + sc_ref[...]) + sh_ref[...]
    h_hi, h_lo = _split_bf16(h)
    h_ref[...] = h_hi

    nt_dims = (((1,), (1,)), ((), ()))
    nr = ROUTER_ROWS
    r_hi = lax.dot_general(wr2_ref[...], h_hi, nt_dims, preferred_element_type=F32)
    r_lo = lax.dot_general(wr2_ref[0:nr, :], h_lo, nt_dims, preferred_element_type=F32)
    logits = r_hi[0:nr] + r_hi[nr:2 * nr] + r_lo + br_ref[:, :1]

    gl = logits[0:GROUP_ROWS]
    grow = lax.broadcasted_iota(jnp.int32, (GROUP_ROWS, blk), 0).astype(F32)
    gmax = jnp.max(gl, axis=0, keepdims=True)
    g_w = 1.0 / jnp.sum(jnp.exp(gl - gmax), axis=0, keepdims=True)
    g_idx = jnp.min(jnp.where(gl == gmax, grow, float(GROUP_ROWS)), axis=0, keepdims=True)

    el = logits[GROUP_ROWS:GROUP_ROWS + N_EXPERTS]
    erow_i = lax.broadcasted_iota(jnp.int32, (N_EXPERTS, blk), 0)
    erow = erow_i.astype(F32)
    el = jnp.where((erow_i >> 3).astype(F32) == g_idx, el, NEG)
    m1 = jnp.max(el, axis=0, keepdims=True)
    i1 = jnp.min(jnp.where(el == m1, erow, float(N_EXPERTS)), axis=0, keepdims=True)
    el2 = jnp.where(erow == i1, NEG, el)
    m2 = jnp.max(el2, axis=0, keepdims=True)
    i2 = jnp.min(jnp.where(el2 == m2, erow, float(N_EXPERTS)), axis=0, keepdims=True)
    t = jnp.exp(m2 - m1)
    w1 = g_w / (1.0 + t)
    w2 = g_w * t / (1.0 + t)

    s1 = erow == i1
    s2 = erow == i2
    onehot = jnp.where(s1 | s2, 1.0, 0.0)
    rank = _dot(onehot.astype(BF16), tri_ref[...])
    cnt = jnp.sum(onehot, axis=1, keepdims=True)
    units = jnp.floor((cnt + (ROW_ALIGN - 1.0)) * (1.0 / ROW_ALIGN))
    units = jnp.broadcast_to(units, (N_EXPERTS, LANES))
    off = float(ROW_ALIGN) * _dot(low_ref[...], units.astype(BF16))
    base = off[:, :1] + rank
    pos1 = jnp.sum(jnp.where(s1, base, 0.0), axis=0, keepdims=True)
    pos2 = jnp.sum(jnp.where(s2, base, 0.0), axis=0, keepdims=True)

    r8 = lax.broadcasted_iota(jnp.int32, (8, blk), 0)
    route_ref[...] = jnp.where(r8 == 0, pos1, jnp.where(r8 == 1, pos2, jnp.where(r8 == 2, w1,
                               jnp.where(r8 == 3, w2, 0.0))))
    off_ref[...] = off.astype(jnp.int32)
    tiles = jnp.floor((cnt + (EXP_TILE - 1.0)) * (1.0 / EXP_TILE))
    nt_ref[...] = jnp.broadcast_to(tiles, (N_EXPERTS, LANES)).astype(jnp.int32)


def _pack_expert_weights(w_gate, w_up, w_down):
    half = D_MODEL // 2
    gu = jnp.concatenate([w_gate, w_up], axis=2)
    dn = jnp.concatenate([w_down[:, :, :half], w_down[:, :, half:]], axis=1)
    return jnp.concatenate([gu, dn], axis=1).astype(BF16)


def _experts_kernel(off_sm, nt_sm, x_ref, h_ref, route_ref, gt_ref, w_ref, fg_ref, o_ref,
                    p_scr, xs_scr, ys_scr, cw_scr, *, final_norm):
    i = pl.program_id(0)
    step = pl.program_id(1)
    nrows, blk = p_scr.shape

    @pl.when(step == 0)
    def _():
        pos1 = route_ref[0:1, :]
        pos2 = route_ref[1:2, :]
        w1 = route_ref[2:3, :]
        w2 = route_ref[3:4, :]
        row_iota = lax.broadcasted_iota(jnp.int32, (GATHER_CHUNK, blk), 0)

        def gather(c, carry):
            r0 = pl.multiple_of(c * GATHER_CHUNK, GATHER_CHUNK)
            prow = (row_iota + r0).astype(F32)
            m1 = prow == pos1
            m2 = prow == pos2
            perm = jnp.where(m1 | m2, 1.0, 0.0).astype(BF16)
            p_scr[pl.ds(r0, GATHER_CHUNK), :] = perm
            xs_scr[pl.ds(r0, GATHER_CHUNK), :] = _dot(perm, h_ref[...]).astype(BF16)
            cw = jnp.sum(jnp.where(m1, w1, 0.0) + jnp.where(m2, w2, 0.0), axis=1, keepdims=True)
            cw_scr[pl.ds(r0, GATHER_CHUNK), :] = jnp.broadcast_to(cw, (GATHER_CHUNK, LANES))
            ys_scr[pl.ds(r0, GATHER_CHUNK), :] = jnp.zeros((GATHER_CHUNK, D_MODEL), BF16)
            return carry

        lax.fori_loop(0, nrows // GATHER_CHUNK, gather, 0)

    half = D_MODEL // 2

    def tile(g, s):
        s = pl.multiple_of(s, ROW_ALIGN)
        rows = xs_scr[pl.ds(s, EXP_TILE), :]
        gu = _dot(rows, w_ref[g, 0:D_MODEL, :])
        gate = gu[:, :EXPERT_DIM]
        up = gu[:, EXPERT_DIM:]
        cw = cw_scr[pl.ds(s, EXP_TILE), :][:, :1]
        hid = (gate * _sigmoid(gate) * up * cw).astype(BF16)
        lo = _dot(hid, w_ref[g, D_MODEL:D_MODEL + EXPERT_DIM, :]).astype(BF16)
        hi = _dot(hid, w_ref[g, D_MODEL + EXPERT_DIM:, :]).astype(BF16)
        return s, lo, hi

    def put(s, lo, hi):
        ys_scr[pl.ds(s, EXP_TILE), 0:half] = lo
        ys_scr[pl.ds(s, EXP_TILE), half:D_MODEL] = hi

    starts = [off_sm[i, step * EXPERTS_PER_STEP + g] for g in range(EXPERTS_PER_STEP)]
    first = [tile(g, starts[g]) for g in range(EXPERTS_PER_STEP)]
    for g in range(EXPERTS_PER_STEP):
        put(*first[g])

        def more(t, carry, g=g):
            put(*tile(g, starts[g] + t * EXP_TILE))
            return carry

        lax.fori_loop(1, nt_sm[i, step * EXPERTS_PER_STEP + g], more, 0)

    @pl.when(step == N_EXPERTS // EXPERTS_PER_STEP - 1)
    def _():
        tn_dims = (((0,), (0,)), ((), ()))
        for r0 in range(0, blk, GATHER_CHUNK):
            rs = slice(r0, r0 + GATHER_CHUNK)
            moe = lax.dot_general(p_scr[:, rs], ys_scr[...], tn_dims, preferred_element_type=F32)
            gt = gt_ref[...] if gt_ref.shape[0] == 1 else gt_ref[rs, :]
            y = x_ref[rs, :] + gt * moe
            if final_norm:
                y = _rms(y) * fg_ref[...]
            o_ref[rs, :] = y


def _moe_call(x, mod, g2, wr2, br, w_exp, fg, *, block, per_row, final_norm):
    assert D_MODEL // 2 == 2 * EXPERT_DIM
    rows = x.shape[0]
    nblk = rows // block
    nsorted = _sorted_rows(block)
    if per_row:
        mspec = lambda s: pl.BlockSpec((block, D_MODEL), lambda i, *_: (i, s))
    else:
        blocks_per_seq = (rows // mod.shape[0]) // block
        mspec = lambda s: pl.BlockSpec((None, 1, D_MODEL), lambda i, *_: (i // blocks_per_seq, 0, s))
    const = lambda i: (0, 0)
    ridx = jnp.arange(block)
    tri = (ridx[:, None] < ridx[None, :]).astype(BF16)
    eidx = jnp.arange(N_EXPERTS)
    low = (eidx[None, :] < eidx[:, None]).astype(BF16)
    h, route, off, nt = pl.pallas_call(
        _route_kernel,
        grid=(nblk,),
        in_specs=[
            pl.BlockSpec((block, D_MODEL), lambda i: (i, 0)),
            mspec(3), mspec(4),
            pl.BlockSpec((1, D_MODEL), const),
            pl.BlockSpec((2 * ROUTER_ROWS, D_MODEL), const),
            pl.BlockSpec((ROUTER_ROWS, LANES), const),
            pl.BlockSpec((block, block), const),
            pl.BlockSpec((N_EXPERTS, N_EXPERTS), const),
        ],
        out_specs=[
            pl.BlockSpec((block, D_MODEL), lambda i: (i, 0)),
            pl.BlockSpec((None, 8, block), lambda i: (i, 0, 0)),
            pl.BlockSpec((None, N_EXPERTS, LANES), lambda i: (i, 0, 0)),
            pl.BlockSpec((None, N_EXPERTS, LANES), lambda i: (i, 0, 0)),
        ],
        out_shape=[
            jax.ShapeDtypeStruct((rows, D_MODEL), BF16),
            jax.ShapeDtypeStruct((nblk, 8, block), F32),
            jax.ShapeDtypeStruct((nblk, N_EXPERTS, LANES), jnp.int32),
            jax.ShapeDtypeStruct((nblk, N_EXPERTS, LANES), jnp.int32),
        ],
        compiler_params=pltpu.CompilerParams(vmem_limit_bytes=VMEM_LIMIT),
        name="moe_route",
    )(x, mod, mod, g2, wr2, br, tri, low)

    const2 = lambda i, e, *_: (0, 0)
    return pl.pallas_call(
        functools.partial(_experts_kernel, final_norm=final_norm),
        grid_spec=pltpu.PrefetchScalarGridSpec(
            num_scalar_prefetch=2,
            grid=(nblk, N_EXPERTS // EXPERTS_PER_STEP),
            in_specs=[
                pl.BlockSpec((block, D_MODEL), lambda i, e, *_: (i, 0)),
                pl.BlockSpec((block, D_MODEL), lambda i, e, *_: (i, 0), pipeline_mode=pl.Buffered(1)),
                pl.BlockSpec((None, 8, block), lambda i, e, *_: (i, 0, 0)),
                mspec(5),
                pl.BlockSpec((EXPERTS_PER_STEP, D_MODEL + 2 * EXPERT_DIM, 2 * EXPERT_DIM),
                             lambda i, s, *_: (s, 0, 0)),
                pl.BlockSpec((1, D_MODEL), const2),
            ],
            out_specs=pl.BlockSpec((block, D_MODEL), lambda i, e, *_: (i, 0)),
            scratch_shapes=[
                pltpu.VMEM((nsorted, block), BF16),
                pltpu.VMEM((nsorted, D_MODEL), BF16),
                pltpu.VMEM((nsorted, D_MODEL), BF16),
                pltpu.VMEM((nsorted, LANES), F32),
            ],
        ),
        out_shape=jax.ShapeDtypeStruct((rows, D_MODEL), F32),
        compiler_params=pltpu.CompilerParams(
            dimension_semantics=("arbitrary", "arbitrary"), vmem_limit_bytes=VMEM_LIMIT),
        name="moe_experts",
    )(off[:, :, 0], nt[:, :, 0], x, h, route, mod, w_exp, fg)


def _rope_tables(pos):
    half = R_QK_DIM // 2
    inv_freq = jnp.power(ROPE_BASE, -jnp.arange(half, dtype=F32) / half)
    ang = pos.astype(F32)[:, None] * inv_freq[None, :]
    return jnp.cos(ang), jnp.sin(ang)


def _decay_tables(chunk):
    log_g = jnp.log1p(-jnp.power(2.0, -5.0 - jnp.arange(R_HEADS, dtype=F32)))
    t1 = (jnp.arange(IN_TILE) % chunk).astype(F32) + 1.0
    dq = jnp.exp(log_g[None, :] * t1[:, None])
    dk = jnp.exp(-log_g[None, :] * t1[:, None]) * (R_QK_DIM ** -0.5)
    cd = jnp.exp(log_g * chunk)
    rep = lambda a: jnp.repeat(a, LANES, axis=1)
    return rep(dq), rep(dk), cd


def kernel(x_prompt, x_sample, state_ret, c_prompt, c_sample, w_mod, b_mod, norm1_g, w_in, ln_v_g, ln_v_b,
           w_s, b_s, w_a_out, w_b_out, w_o, norm2_g, w_router_group, b_router_group, w_router_expert,
           b_router_expert, w_gate, w_up, w_down, final_g):
    n_p, t_p, _ = x_prompt.shape
    n_s, t_s, _ = x_sample.shape
    assert t_p % CHUNK == 0 and (n_s * t_s) % CHUNK == 0 and CHUNK % t_s == 0 and RET_ROWS % t_s == 0
    chunks_p = t_p // CHUNK
    chunks_s = (n_s * t_s) // CHUNK

    mod = _mod_call(jnp.concatenate([c_prompt, c_sample], axis=0), w_mod, b_mod)

    cos_p, sin_p = _rope_tables(jnp.arange(t_p, dtype=jnp.int32))
    cos_s, sin_s = _rope_tables(PAST_LEN + jnp.arange(IN_TILE, dtype=jnp.int32) % t_s)
    dq_p, dk_p, cd_p = _decay_tables(CHUNK)
    dq_s, dk_s, cd_s = _decay_tables(t_s)

    idx = jnp.arange(CHUNK)
    causal = idx[:, None] >= idx[None, :]
    same_seq = (idx[:, None] // t_s) == (idx[None, :] // t_s)
    mask_p = causal.astype(F32)
    mask_s = (causal & same_seq).astype(F32)

    xp = x_prompt.reshape(n_p * t_p, D_MODEL)
    xs = x_sample.reshape(n_s * t_s, D_MODEL)
    fg = final_g.reshape(1, D_MODEL)
    r_prompt, r_sample, v_sample = None, None, []
    for l in range(DEPTH):
        mod_p = mod[l, :n_p].reshape(n_p, 1, MOD_COLS)
        mod_s = jnp.repeat(mod[l, n_p:], t_s, axis=0)
        g1 = norm1_g[l].reshape(1, D_MODEL)
        g2 = norm2_g[l].reshape(1, D_MODEL)
        lng = ln_v_g[l].reshape(1, D_MODEL)
        lnb = ln_v_b[l].reshape(1, D_MODEL)
        w_in_l = w_in[l].astype(BF16)
        wa, wb, wo = w_a_out[l].astype(BF16), w_b_out[l].astype(BF16), w_o[l].astype(BF16)

        ws_p = jnp.where(causal[None], w_s[l], 0.0).astype(BF16)
        bs_p = jnp.repeat(b_s[l].T, D_MODEL // A_GROUPS, axis=1)
        blk = jnp.where(causal[:t_s, :t_s][None], w_s[l][:, :t_s, :t_s], 0.0)
        ws_s = jnp.where(same_seq[None], jnp.tile(blk, (1, CHUNK // t_s, CHUNK // t_s)), 0.0).astype(BF16)
        bs_s = jnp.tile(bs_p[:t_s], (CHUNK // t_s, 1))

        zp = _inproj_call(xp, mod_p, g1, cos_p, sin_p, dq_p, dk_p, lng, lnb, w_in_l,
                          per_row=False, z_dtype=BF16)
        zs = _inproj_call(xs, mod_s, g1, cos_s, sin_s, dq_s, dk_s, lng, lnb, w_in_l,
                          per_row=True, z_dtype=F32)

        xp, r_prompt = _mix_call(cd_p, zp, xp, mod_p, ws_p, bs_p, mask_p, wa, wb, wo,
                                 nseq=n_p, nchunk=chunks_p, layer=l, stacked=r_prompt)
        cross, r_sample = _ret_sample_call(cd_s, zs, state_ret, l, t_s, r_sample)
        xs = _mix_call(cd_s, zs, xs, mod_s, ws_s, bs_s, mask_s, wa, wb, wo,
                       nseq=chunks_s, nchunk=1, cross=cross)

        tail = ROUTER_ROWS - GROUP_ROWS - N_EXPERTS
        wr = jnp.concatenate([jnp.pad(w_router_group[l].T, ((0, GROUP_ROWS - N_GROUPS), (0, 0))),
                              jnp.pad(w_router_expert[l].T, ((0, tail), (0, 0)))], axis=0)
        br = jnp.concatenate([jnp.pad(b_router_group[l], (0, GROUP_ROWS - N_GROUPS), constant_values=NEG),
                              jnp.pad(b_router_expert[l], (0, tail))])
        wr2 = jnp.concatenate(_split_bf16(wr), axis=0)
        br = jnp.broadcast_to(br[:, None], (ROUTER_ROWS, LANES))
        w_exp = _pack_expert_weights(w_gate[l], w_up[l], w_down[l])
        last = l == DEPTH - 1
        xp = _moe_call(xp, mod_p, g2, wr2, br, w_exp, fg, block=MOE_BLOCK, per_row=False, final_norm=last)
        xs = _moe_call(xs, mod_s, g2, wr2, br, w_exp, fg, block=n_s * t_s, per_row=True, final_norm=last)

        v_sample.append(zs[:, OFF_V:OFF_V + D_MODEL].reshape(n_s, t_s, D_MODEL))

    return (xp.reshape(n_p, t_p, D_MODEL), xs.reshape(n_s, t_s, D_MODEL),
            r_prompt, r_sample, jnp.stack(v_sample))
```

```python
import functools

import jax
import jax.numpy as jnp
from jax import lax
from jax.experimental import pallas as pl
from jax.experimental.pallas import tpu as pltpu

D_MODEL = 1024
DEPTH = 2
PAST_LEN = 16384
A_GROUPS = 8
CHUNK = 128
R_HEADS = 4
R_QK_DIM = 256
R_V_DIM = 512
ROPE_BASE = 10000.0
N_GROUPS = 4
EXPERTS_PER_GROUP = 8
N_EXPERTS = 32
EXPERT_DIM = 256
EPS = 1e-6
IN_COLS = 10 * D_MODEL
MOD_COLS = 6 * D_MODEL

OFF_U, OFF_V, OFF_Q, OFF_K, OFF_VR, OFF_GR, OFF_GA, OFF_GB = 0, 1024, 2048, 3072, 4096, 6144, 8192, 9216

LANES = 128
IN_TILE_BF16 = 256
IN_TILE_F32 = 256
MIX_SEQS = 2
RET_ROWS = 8
MOE_BLOCK = 1024
EXP_TILE = 128
ROW_ALIGN = 16
GATHER_CHUNK = 256
GROUP_ROWS = 8
ROUTER_ROWS = 48
EXPERTS_PER_STEP = 4
MOD_TILE = 1536
VMEM_LIMIT = 56 * 1024 * 1024
NEG = -1e30

BF16 = jnp.bfloat16
F32 = jnp.float32


def _dot(a, b):
    return jnp.dot(a, b, preferred_element_type=F32)


def _dot_f32(a, b):
    return jnp.dot(a, b, preferred_element_type=F32, precision=lax.Precision.HIGHEST)


def _sigmoid(x):
    return 1.0 / (1.0 + jnp.exp(-x))


def _gelu_tanh(x):
    return x * (0.5 * (1.0 + jnp.tanh(0.7978845608028654 * (x + 0.044715 * (x * x * x)))))


def _rms(x):
    return x * lax.rsqrt(jnp.mean(x * x, axis=-1, keepdims=True) + EPS)


def _mod_kernel(c_ref, w_ref, b_ref, o_ref):
    c = c_ref[...]
    o_ref[...] = _dot_f32(c * _sigmoid(c), w_ref[...]) + b_ref[...]


def _mod_call(c_all, w_mod, b_mod):
    n = c_all.shape[0]
    return pl.pallas_call(
        _mod_kernel,
        grid=(DEPTH, MOD_COLS // MOD_TILE),
        in_specs=[
            pl.BlockSpec((n, D_MODEL), lambda l, j: (0, 0)),
            pl.BlockSpec((None, D_MODEL, MOD_TILE), lambda l, j: (l, 0, j)),
            pl.BlockSpec((None, 1, MOD_TILE), lambda l, j: (l, 0, j)),
        ],
        out_specs=pl.BlockSpec((None, n, MOD_TILE), lambda l, j: (l, 0, j)),
        out_shape=jax.ShapeDtypeStruct((DEPTH, n, MOD_COLS), F32),
        compiler_params=pltpu.CompilerParams(vmem_limit_bytes=VMEM_LIMIT),
        name="adaln_mod",
    )(c_all, w_mod, b_mod.reshape(DEPTH, 1, MOD_COLS))


def _inproj_kernel(x_ref, sh_ref, sc_ref, g_ref, cos_ref, sin_ref, dq_ref, dk_ref, lng_ref, lnb_ref,
                   w_ref, z_ref):
    h = (_rms(x_ref[...]) * g_ref[...]) * (1.0 + sc_ref[...]) + sh_ref[...]
    h = h.astype(BF16)
    cos = cos_ref[...]
    sin = sin_ref[...]

    def proj(off, width=D_MODEL):
        return _dot(h, w_ref[:, off:off + width])

    def put(off, val):
        z_ref[:, off:off + val.shape[1]] = val.astype(z_ref.dtype)

    put(OFF_U, _gelu_tanh(proj(OFF_U)))

    v = _gelu_tanh(proj(OFF_V))
    mu = jnp.mean(v, axis=-1, keepdims=True)
    vc = v - mu
    var = jnp.mean(vc * vc, axis=-1, keepdims=True)
    put(OFF_V, vc * lax.rsqrt(var + EPS) * lng_ref[...] + lnb_ref[...])

    half = R_QK_DIM // 2
    for off, d_ref in ((OFF_Q, dq_ref), (OFF_K, dk_ref)):
        acc = proj(off)
        for hd in range(R_HEADS):
            x1 = acc[:, hd * R_QK_DIM:hd * R_QK_DIM + half]
            x2 = acc[:, hd * R_QK_DIM + half:(hd + 1) * R_QK_DIM]
            dec = d_ref[:, hd * LANES:(hd + 1) * LANES]
            put(off + hd * R_QK_DIM, (x1 * cos - x2 * sin) * dec)
            put(off + hd * R_QK_DIM + half, (x1 * sin + x2 * cos) * dec)

    for j in range(2):
        put(OFF_VR + j * D_MODEL, proj(OFF_VR + j * D_MODEL))
    for j in range(2):
        g = proj(OFF_GR + j * D_MODEL)
        put(OFF_GR + j * D_MODEL, g * _sigmoid(g))
    put(OFF_GA, _sigmoid(proj(OFF_GA)))
    put(OFF_GB, _sigmoid(proj(OFF_GB)))


def _inproj_call(x, mod, g1, cos, sin, dq, dk, lng, lnb, w_in, *, layer, per_row, z_dtype):
    rows = x.shape[0]
    tile = dq.shape[0]
    tiles_per_seq = None if per_row else (rows // mod.shape[0]) // tile
    if per_row:
        sh_spec = pl.BlockSpec((tile, D_MODEL), lambda i: (i, 0))
        sc_spec = pl.BlockSpec((tile, D_MODEL), lambda i: (i, 1))
        rope_spec = pl.BlockSpec((tile, LANES), lambda i: (0, 0))
    else:
        sh_spec = pl.BlockSpec((None, 1, D_MODEL), lambda i: (i // tiles_per_seq, 0, 0))
        sc_spec = pl.BlockSpec((None, 1, D_MODEL), lambda i: (i // tiles_per_seq, 0, 1))
        rope_spec = pl.BlockSpec((tile, LANES), lambda i: (i % tiles_per_seq, 0))
    const = lambda i: (0, 0)
    return pl.pallas_call(
        _inproj_kernel,
        grid=(rows // tile,),
        in_specs=[
            pl.BlockSpec((tile, D_MODEL), lambda i: (i, 0)),
            sh_spec, sc_spec,
            pl.BlockSpec((1, D_MODEL), const),
            rope_spec, rope_spec,
            pl.BlockSpec((tile, R_HEADS * LANES), const),
            pl.BlockSpec((tile, R_HEADS * LANES), const),
            pl.BlockSpec((1, D_MODEL), const),
            pl.BlockSpec((1, D_MODEL), const),
            pl.BlockSpec((None, D_MODEL, IN_COLS), lambda i: (layer, 0, 0), pipeline_mode=pl.Buffered(1)),
        ],
        out_specs=pl.BlockSpec((tile, IN_COLS), lambda i: (i, 0)),
        out_shape=jax.ShapeDtypeStruct((rows, IN_COLS), z_dtype),
        compiler_params=pltpu.CompilerParams(vmem_limit_bytes=VMEM_LIMIT),
        name="in_proj",
    )(x, mod, mod, g1, cos, sin, dq, dk, lng, lnb, w_in)


def _ret_sample_kernel(cd_ref, z_q_ref, z_k_ref, z_v_ref, r0_ref, *rest, t_s):
    cross_ref, r_ref = rest[-2:]
    q = z_q_ref[...].astype(BF16)
    k = z_k_ref[...]
    v = z_v_ref[...].astype(BF16)
    row_k = lax.broadcasted_iota(jnp.int32, (RET_ROWS, R_QK_DIM), 0)
    row_v = lax.broadcasted_iota(jnp.int32, (RET_ROWS, R_V_DIM), 0)
    for hd in range(R_HEADS):
        qh = q[:, hd * R_QK_DIM:(hd + 1) * R_QK_DIM]
        kh = k[:, hd * R_QK_DIM:(hd + 1) * R_QK_DIM]
        vh = v[:, hd * R_V_DIM:(hd + 1) * R_V_DIM]
        cross = jnp.zeros((RET_ROWS, R_V_DIM), F32)
        for s in range(RET_ROWS // t_s):
            r = r0_ref[s, hd]
            lo, hi = s * t_s, (s + 1) * t_s
            cross = jnp.where((row_v >= lo) & (row_v < hi), _dot(qh, r.astype(BF16)), cross)
            k_seq = jnp.where((row_k >= lo) & (row_k < hi), kh, 0.0).astype(BF16)
            upd = lax.dot_general(k_seq, vh, (((0,), (0,)), ((), ())), preferred_element_type=F32)
            r_ref[s, hd] = cd_ref[hd] * (r + upd)
        cross_ref[:, hd * R_V_DIM:(hd + 1) * R_V_DIM] = cross


def _ret_sample_call(cd, z, state, layer, t_s, stacked=None):
    rows = z.shape[0]
    seqs = RET_ROWS // t_s
    qk_w = R_HEADS * R_QK_DIM
    v_w = R_HEADS * R_V_DIM
    state_spec = pl.BlockSpec((None, seqs, R_HEADS, R_QK_DIM, R_V_DIM), lambda b: (layer, b, 0, 0, 0))
    in_specs = [
        pl.BlockSpec(memory_space=pltpu.SMEM),
        pl.BlockSpec((RET_ROWS, qk_w), lambda b: (b, OFF_Q // qk_w)),
        pl.BlockSpec((RET_ROWS, qk_w), lambda b: (b, OFF_K // qk_w)),
        pl.BlockSpec((RET_ROWS, v_w), lambda b: (b, OFF_VR // v_w)),
        state_spec,
    ]
    args = [cd, z, z, z, state]
    aliases = {}
    if stacked is not None:
        aliases = {len(args): 1}
        in_specs.append(pl.BlockSpec(memory_space=pl.ANY))
        args.append(stacked)
    return pl.pallas_call(
        functools.partial(_ret_sample_kernel, t_s=t_s),
        grid=(rows // RET_ROWS,),
        in_specs=in_specs,
        out_specs=[
            pl.BlockSpec((RET_ROWS, v_w), lambda b: (b, 0)),
            state_spec,
        ],
        out_shape=[
            jax.ShapeDtypeStruct((rows, v_w), F32),
            jax.ShapeDtypeStruct(state.shape, F32),
        ],
        input_output_aliases=aliases,
        compiler_params=pltpu.CompilerParams(vmem_limit_bytes=VMEM_LIMIT),
        name="ret_sample",
    )(*args)


def _mix_kernel(cd_ref, z_ref, x_ref, gt_ref, ws_ref, bs_ref, mask_ref, wa_ref, wb_ref, wo_ref, *rest,
                carried_state):
    nsb, nrows = z_ref.shape[:2]
    per_sb = nrows // CHUNK
    nseq = nsb * per_sb
    if carried_state:
        cross_ref, xo_ref = rest
    else:
        xo_ref, r_ref = rest[-2:]

        @pl.when(pl.program_id(1) == 0)
        def _():
            r_ref[...] = jnp.zeros_like(r_ref)

    mask = mask_ref[...]
    gdim = D_MODEL // A_GROUPS
    a_rows, b_rows = [], []
    for sq in range(nseq):
        sb, chunk = divmod(sq, per_sb)
        rs = slice(chunk * CHUNK, (chunk + 1) * CHUNK)

        def sec(off, width, sb=sb, rs=rs):
            return z_ref[sb, rs, off:off + width]

        v = sec(OFF_V, D_MODEL).astype(BF16)
        mixed = jnp.concatenate(
            [_dot(ws_ref[g], v[:, g * gdim:(g + 1) * gdim]) for g in range(A_GROUPS)], axis=1)
        a_rows.append((sec(OFF_U, D_MODEL).astype(F32) * (mixed + bs_ref[...])).astype(BF16))

        b_parts = []
        for hd in range(R_HEADS):
            qh = sec(OFF_Q + hd * R_QK_DIM, R_QK_DIM).astype(BF16)
            kh = sec(OFF_K + hd * R_QK_DIM, R_QK_DIM).astype(BF16)
            vh = sec(OFF_VR + hd * R_V_DIM, R_V_DIM).astype(BF16)
            s = lax.dot_general(qh, kh, (((1,), (1,)), ((), ())), preferred_element_type=F32) * mask
            y = _dot(s.astype(BF16), vh)
            if carried_state:
                y = y + cross_ref[sb, rs, hd * R_V_DIM:(hd + 1) * R_V_DIM]
            else:
                r = r_ref[sq, hd]
                y = y + _dot(qh, r.astype(BF16))
                upd = lax.dot_general(kh, vh, (((0,), (0,)), ((), ())), preferred_element_type=F32)
                r_ref[sq, hd] = cd_ref[hd] * (r + upd)
            mu = jnp.mean(y, axis=-1, keepdims=True)
            yc = y - mu
            var = jnp.mean(yc * yc, axis=-1, keepdims=True)
            yn = yc * lax.rsqrt(var + EPS)
            b_parts.append((sec(OFF_GR + hd * R_V_DIM, R_V_DIM).astype(F32) * yn).astype(BF16))
        b_rows.append(jnp.concatenate(b_parts, axis=1))

    pa = _dot(jnp.concatenate(a_rows, axis=0), wa_ref[...])
    pb = _dot(jnp.concatenate(b_rows, axis=0), wb_ref[...])
    ga = z_ref[:, :, OFF_GA:OFF_GA + D_MODEL].reshape(nseq * CHUNK, D_MODEL).astype(F32)
    gb = z_ref[:, :, OFF_GB:OFF_GB + D_MODEL].reshape(nseq * CHUNK, D_MODEL).astype(F32)
    mix = _dot((ga * pa + gb * pb).astype(BF16), wo_ref[...])
    xo_ref[...] = x_ref[...] + gt_ref[...] * mix.reshape(nsb, nrows, D_MODEL)


def _mix_call(cd, z, x, mod, ws, bs, mask, wa, wb, wo, *, nseq, nchunk, cross=None, layer=0, stacked=None):
    rows = x.shape[0]
    carried = cross is not None
    aliases = {}
    if carried:
        vseq, vlen, sb, rb = 1, rows, 1, MIX_SEQS * CHUNK
    else:
        vseq, vlen, sb, rb = nseq, nchunk * CHUNK, MIX_SEQS, CHUNK
    assert vseq % sb == 0 and vlen % rb == 0
    view = lambda a: a.reshape(vseq, vlen, a.shape[-1])
    row_map = lambda b, c: (b, c, 0)
    const2 = lambda b, c: (0, 0)
    wmap = lambda b, c: (layer, 0, 0)
    if carried:
        mod = view(mod)
        gt_spec = pl.BlockSpec((sb, rb, D_MODEL), lambda b, c: (b, c, 2))
    else:
        gt_spec = pl.BlockSpec((sb, 1, D_MODEL), lambda b, c: (b, 0, 2))
    in_specs = [
        pl.BlockSpec(memory_space=pltpu.SMEM),
        pl.BlockSpec((sb, rb, IN_COLS), row_map),
        pl.BlockSpec((sb, rb, D_MODEL), row_map),
        gt_spec,
        pl.BlockSpec((A_GROUPS, CHUNK, CHUNK), lambda b, c: (0, 0, 0)),
        pl.BlockSpec((CHUNK, D_MODEL), const2),
        pl.BlockSpec((CHUNK, CHUNK), const2),
        pl.BlockSpec((None, D_MODEL, D_MODEL), wmap),
        pl.BlockSpec((None, R_HEADS * R_V_DIM, D_MODEL), wmap),
        pl.BlockSpec((None, D_MODEL, D_MODEL), wmap),
    ]
    args = [cd, view(z), view(x), mod, ws, bs, mask, wa, wb, wo]
    x_out = jax.ShapeDtypeStruct((vseq, vlen, D_MODEL), F32)
    x_spec = pl.BlockSpec((sb, rb, D_MODEL), row_map)
    if carried:
        in_specs.append(pl.BlockSpec((sb, rb, R_HEADS * R_V_DIM), row_map))
        args.append(view(cross))
        out_specs, out_shape = x_spec, x_out
    else:
        out_specs = [x_spec, pl.BlockSpec((None, sb, R_HEADS, R_QK_DIM, R_V_DIM),
                                          lambda b, c: (layer, b, 0, 0, 0))]
        out_shape = [x_out, jax.ShapeDtypeStruct((DEPTH, nseq, R_HEADS, R_QK_DIM, R_V_DIM), F32)]
        if stacked is not None:
            aliases = {len(args): 1}
            in_specs.append(pl.BlockSpec(memory_space=pl.ANY))
            args.append(stacked)
    out = pl.pallas_call(
        functools.partial(_mix_kernel, carried_state=carried),
        grid=(vseq // sb, vlen // rb),
        in_specs=in_specs,
        out_specs=out_specs,
        out_shape=out_shape,
        input_output_aliases=aliases,
        compiler_params=pltpu.CompilerParams(
            dimension_semantics=("arbitrary", "arbitrary"), vmem_limit_bytes=VMEM_LIMIT),
        name="token_mix",
    )(*args)
    if carried:
        return out.reshape(rows, D_MODEL)
    return out[0].reshape(rows, D_MODEL), out[1]


def _sorted_rows(block):
    need = 2 * block + N_EXPERTS * ROW_ALIGN + EXP_TILE
    return -(-need // GATHER_CHUNK) * GATHER_CHUNK


def _split_bf16(w):
    hi = w.astype(BF16)
    return hi, (w - hi.astype(F32)).astype(BF16)


def _route_kernel(x_ref, sh_ref, sc_ref, g_ref, wr2_ref, br_ref, tri_ref, low_ref,
                  h_ref, route_ref, off_ref, nt_ref):
    blk = x_ref.shape[0]
    h = (_rms(x_ref[...]) * g_ref[...]) * (1.0 + sc_ref[...]) + sh_ref[...]
    h_hi, h_lo = _split_bf16(h)
    h_ref[...] = h_hi

    nt_dims = (((1,), (1,)), ((), ()))
    nr = ROUTER_ROWS
    r_hi = lax.dot_general(wr2_ref[...], h_hi, nt_dims, preferred_element_type=F32)
    r_lo = lax.dot_general(wr2_ref[0:nr, :], h_lo, nt_dims, preferred_element_type=F32)
    logits = r_hi[0:nr] + r_hi[nr:2 * nr] + r_lo + br_ref[:, :1]

    gl = logits[0:GROUP_ROWS]
    grow = lax.broadcasted_iota(jnp.int32, (GROUP_ROWS, blk), 0).astype(F32)
    gmax = jnp.max(gl, axis=0, keepdims=True)
    g_w = 1.0 / jnp.sum(jnp.exp(gl - gmax), axis=0, keepdims=True)
    g_idx = jnp.min(jnp.where(gl == gmax, grow, float(GROUP_ROWS)), axis=0, keepdims=True)

    el = logits[GROUP_ROWS:GROUP_ROWS + N_EXPERTS]
    erow_i = lax.broadcasted_iota(jnp.int32, (N_EXPERTS, blk), 0)
    erow = erow_i.astype(F32)
    el = jnp.where((erow_i >> 3).astype(F32) == g_idx, el, NEG)
    m1 = jnp.max(el, axis=0, keepdims=True)
    i1 = jnp.min(jnp.where(el == m1, erow, float(N_EXPERTS)), axis=0, keepdims=True)
    el2 = jnp.where(erow == i1, NEG, el)
    m2 = jnp.max(el2, axis=0, keepdims=True)
    i2 = jnp.min(jnp.where(el2 == m2, erow, float(N_EXPERTS)), axis=0, keepdims=True)
    t = jnp.exp(m2 - m1)
    w1 = g_w / (1.0 + t)
    w2 = g_w * t / (1.0 + t)

    s1 = erow == i1
    s2 = erow == i2
    onehot = jnp.where(s1 | s2, 1.0, 0.0)
    rank = _dot(onehot.astype(BF16), tri_ref[...])
    cnt = jnp.sum(onehot, axis=1, keepdims=True)
    units = jnp.floor((cnt + (ROW_ALIGN - 1.0)) * (1.0 / ROW_ALIGN))
    units = jnp.broadcast_to(units, (N_EXPERTS, LANES))
    off = float(ROW_ALIGN) * _dot(low_ref[...], units.astype(BF16))
    base = off[:, :1] + rank
    pos1 = jnp.sum(jnp.where(s1, base, 0.0), axis=0, keepdims=True)
    pos2 = jnp.sum(jnp.where(s2, base, 0.0), axis=0, keepdims=True)

    r8 = lax.broadcasted_iota(jnp.int32, (8, blk), 0)
    route_ref[...] = jnp.where(r8 == 0, pos1, jnp.where(r8 == 1, pos2, jnp.where(r8 == 2, w1,
                               jnp.where(r8 == 3, w2, 0.0))))
    off_ref[...] = off.astype(jnp.int32)
    tiles = jnp.floor((cnt + (EXP_TILE - 1.0)) * (1.0 / EXP_TILE))
    nt_ref[...] = jnp.broadcast_to(tiles, (N_EXPERTS, LANES)).astype(jnp.int32)


def _pack_expert_weights(w_gate, w_up, w_down):
    half = D_MODEL // 2
    gu = jnp.concatenate([w_gate, w_up], axis=-1)
    dn = jnp.concatenate([w_down[..., :half], w_down[..., half:]], axis=-2)
    return jnp.concatenate([gu, dn], axis=-2).astype(BF16)


def _experts_kernel(off_sm, nt_sm, x_ref, h_ref, route_ref, gt_ref, w_ref, fg_ref, o_ref,
                    p_scr, xs_scr, ys_scr, cw_scr, *, final_norm):
    i = pl.program_id(0)
    step = pl.program_id(1)
    nrows, blk = p_scr.shape

    @pl.when(step == 0)
    def _():
        pos1 = route_ref[0:1, :]
        pos2 = route_ref[1:2, :]
        w1 = route_ref[2:3, :]
        w2 = route_ref[3:4, :]
        row_iota = lax.broadcasted_iota(jnp.int32, (GATHER_CHUNK, blk), 0)

        def gather(c, carry):
            r0 = pl.multiple_of(c * GATHER_CHUNK, GATHER_CHUNK)
            prow = (row_iota + r0).astype(F32)
            m1 = prow == pos1
            m2 = prow == pos2
            perm = jnp.where(m1 | m2, 1.0, 0.0).astype(BF16)
            p_scr[pl.ds(r0, GATHER_CHUNK), :] = perm
            xs_scr[pl.ds(r0, GATHER_CHUNK), :] = _dot(perm, h_ref[...]).astype(BF16)
            cw = jnp.sum(jnp.where(m1, w1, 0.0) + jnp.where(m2, w2, 0.0), axis=1, keepdims=True)
            cw_scr[pl.ds(r0, GATHER_CHUNK), :] = jnp.broadcast_to(cw, (GATHER_CHUNK, LANES))
            ys_scr[pl.ds(r0, GATHER_CHUNK), :] = jnp.zeros((GATHER_CHUNK, D_MODEL), BF16)
            return carry

        lax.fori_loop(0, nrows // GATHER_CHUNK, gather, 0)

    half = D_MODEL // 2

    def tile(g, s):
        s = pl.multiple_of(s, ROW_ALIGN)
        rows = xs_scr[pl.ds(s, EXP_TILE), :]
        gu = _dot(rows, w_ref[g, 0:D_MODEL, :])
        gate = gu[:, :EXPERT_DIM]
        up = gu[:, EXPERT_DIM:]
        cw = cw_scr[pl.ds(s, EXP_TILE), :][:, :1]
        hid = (gate * _sigmoid(gate) * up * cw).astype(BF16)
        lo = _dot(hid, w_ref[g, D_MODEL:D_MODEL + EXPERT_DIM, :]).astype(BF16)
        hi = _dot(hid, w_ref[g, D_MODEL + EXPERT_DIM:, :]).astype(BF16)
        return s, lo, hi

    def put(s, lo, hi):
        ys_scr[pl.ds(s, EXP_TILE), 0:half] = lo
        ys_scr[pl.ds(s, EXP_TILE), half:D_MODEL] = hi

    starts = [off_sm[i, step * EXPERTS_PER_STEP + g] for g in range(EXPERTS_PER_STEP)]
    first = [tile(g, starts[g]) for g in range(EXPERTS_PER_STEP)]
    for g in range(EXPERTS_PER_STEP):
        put(*first[g])

        def more(t, carry, g=g):
            put(*tile(g, starts[g] + t * EXP_TILE))
            return carry

        lax.fori_loop(1, nt_sm[i, step * EXPERTS_PER_STEP + g], more, 0)

    @pl.when(step == N_EXPERTS // EXPERTS_PER_STEP - 1)
    def _():
        tn_dims = (((0,), (0,)), ((), ()))
        for r0 in range(0, blk, GATHER_CHUNK):
            rs = slice(r0, r0 + GATHER_CHUNK)
            moe = lax.dot_general(p_scr[:, rs], ys_scr[...], tn_dims, preferred_element_type=F32)
            gt = gt_ref[...] if gt_ref.shape[0] == 1 else gt_ref[rs, :]
            y = x_ref[rs, :] + gt * moe
            if final_norm:
                y = _rms(y) * fg_ref[...]
            o_ref[rs, :] = y


def _moe_call(x, mod, g2, wr2, br, w_exp, fg, *, layer, block, per_row, final_norm):
    assert D_MODEL // 2 == 2 * EXPERT_DIM
    rows = x.shape[0]
    nblk = rows // block
    nsorted = _sorted_rows(block)
    if per_row:
        mspec = lambda s: pl.BlockSpec((block, D_MODEL), lambda i, *_: (i, s))
    else:
        blocks_per_seq = (rows // mod.shape[0]) // block
        mspec = lambda s: pl.BlockSpec((None, 1, D_MODEL), lambda i, *_: (i // blocks_per_seq, 0, s))
    const = lambda i: (0, 0)
    ridx = jnp.arange(block)
    tri = (ridx[:, None] < ridx[None, :]).astype(BF16)
    eidx = jnp.arange(N_EXPERTS)
    low = (eidx[None, :] < eidx[:, None]).astype(BF16)
    h, route, off, nt = pl.pallas_call(
        _route_kernel,
        grid=(nblk,),
        in_specs=[
            pl.BlockSpec((block, D_MODEL), lambda i: (i, 0)),
            mspec(3), mspec(4),
            pl.BlockSpec((1, D_MODEL), const),
            pl.BlockSpec((2 * ROUTER_ROWS, D_MODEL), const),
            pl.BlockSpec((ROUTER_ROWS, LANES), const),
            pl.BlockSpec((block, block), const),
            pl.BlockSpec((N_EXPERTS, N_EXPERTS), const),
        ],
        out_specs=[
            pl.BlockSpec((block, D_MODEL), lambda i: (i, 0)),
            pl.BlockSpec((None, 8, block), lambda i: (i, 0, 0)),
            pl.BlockSpec((None, N_EXPERTS, LANES), lambda i: (i, 0, 0)),
            pl.BlockSpec((None, N_EXPERTS, LANES), lambda i: (i, 0, 0)),
        ],
        out_shape=[
            jax.ShapeDtypeStruct((rows, D_MODEL), BF16),
            jax.ShapeDtypeStruct((nblk, 8, block), F32),
            jax.ShapeDtypeStruct((nblk, N_EXPERTS, LANES), jnp.int32),
            jax.ShapeDtypeStruct((nblk, N_EXPERTS, LANES), jnp.int32),
        ],
        compiler_params=pltpu.CompilerParams(vmem_limit_bytes=VMEM_LIMIT),
        name="moe_route",
    )(x, mod, mod, g2, wr2, br, tri, low)

    const2 = lambda i, e, *_: (0, 0)
    return pl.pallas_call(
        functools.partial(_experts_kernel, final_norm=final_norm),
        grid_spec=pltpu.PrefetchScalarGridSpec(
            num_scalar_prefetch=2,
            grid=(nblk, N_EXPERTS // EXPERTS_PER_STEP),
            in_specs=[
                pl.BlockSpec((block, D_MODEL), lambda i, e, *_: (i, 0)),
                pl.BlockSpec((block, D_MODEL), lambda i, e, *_: (i, 0), pipeline_mode=pl.Buffered(1)),
                pl.BlockSpec((None, 8, block), lambda i, e, *_: (i, 0, 0)),
                mspec(5),
                pl.BlockSpec((None, EXPERTS_PER_STEP, D_MODEL + 2 * EXPERT_DIM, 2 * EXPERT_DIM),
                             lambda i, s, *_: (layer, s, 0, 0)),
                pl.BlockSpec((1, D_MODEL), const2),
            ],
            out_specs=pl.BlockSpec((block, D_MODEL), lambda i, e, *_: (i, 0)),
            scratch_shapes=[
                pltpu.VMEM((nsorted, block), BF16),
                pltpu.VMEM((nsorted, D_MODEL), BF16),
                pltpu.VMEM((nsorted, D_MODEL), BF16),
                pltpu.VMEM((nsorted, LANES), F32),
            ],
        ),
        out_shape=jax.ShapeDtypeStruct((rows, D_MODEL), F32),
        compiler_params=pltpu.CompilerParams(
            dimension_semantics=("arbitrary", "arbitrary"), vmem_limit_bytes=VMEM_LIMIT),
        name="moe_experts",
    )(off[:, :, 0], nt[:, :, 0], x, h, route, mod, w_exp, fg)


def _rope_tables(pos):
    half = R_QK_DIM // 2
    inv_freq = jnp.power(ROPE_BASE, -jnp.arange(half, dtype=F32) / half)
    ang = pos.astype(F32)[:, None] * inv_freq[None, :]
    return jnp.cos(ang), jnp.sin(ang)


def _decay_tables(chunk, rows):
    log_g = jnp.log1p(-jnp.power(2.0, -5.0 - jnp.arange(R_HEADS, dtype=F32)))
    t1 = (jnp.arange(rows) % chunk).astype(F32) + 1.0
    dq = jnp.exp(log_g[None, :] * t1[:, None])
    dk = jnp.exp(-log_g[None, :] * t1[:, None]) * (R_QK_DIM ** -0.5)
    cd = jnp.exp(log_g * chunk)
    rep = lambda a: jnp.repeat(a, LANES, axis=1)
    return rep(dq), rep(dk), cd


def kernel(x_prompt, x_sample, state_ret, c_prompt, c_sample, w_mod, b_mod, norm1_g, w_in, ln_v_g, ln_v_b,
           w_s, b_s, w_a_out, w_b_out, w_o, norm2_g, w_router_group, b_router_group, w_router_expert,
           b_router_expert, w_gate, w_up, w_down, final_g):
    n_p, t_p, _ = x_prompt.shape
    n_s, t_s, _ = x_sample.shape
    assert t_p % CHUNK == 0 and (n_s * t_s) % CHUNK == 0 and CHUNK % t_s == 0 and RET_ROWS % t_s == 0
    chunks_p = t_p // CHUNK
    chunks_s = (n_s * t_s) // CHUNK

    mod = _mod_call(jnp.concatenate([c_prompt, c_sample], axis=0), w_mod, b_mod)

    cos_p, sin_p = _rope_tables(jnp.arange(t_p, dtype=jnp.int32))
    cos_s, sin_s = _rope_tables(PAST_LEN + jnp.arange(IN_TILE_F32, dtype=jnp.int32) % t_s)
    dq_p, dk_p, cd_p = _decay_tables(CHUNK, IN_TILE_BF16)
    dq_s, dk_s, cd_s = _decay_tables(t_s, IN_TILE_F32)

    idx = jnp.arange(CHUNK)
    causal = idx[:, None] >= idx[None, :]
    same_seq = (idx[:, None] // t_s) == (idx[None, :] // t_s)
    mask_p = causal.astype(F32)
    mask_s = (causal & same_seq).astype(F32)

    xp = x_prompt.reshape(n_p * t_p, D_MODEL)
    xs = x_sample.reshape(n_s * t_s, D_MODEL)
    fg = final_g.reshape(1, D_MODEL)
    w_in_b = w_in.astype(BF16)
    wa, wb, wo = w_a_out.astype(BF16), w_b_out.astype(BF16), w_o.astype(BF16)
    w_exp = _pack_expert_weights(w_gate, w_up, w_down)
    r_prompt, r_sample, v_sample = None, None, []
    for l in range(DEPTH):
        mod_p = mod[l, :n_p].reshape(n_p, 1, MOD_COLS)
        mod_s = jnp.repeat(mod[l, n_p:], t_s, axis=0)
        g1 = norm1_g[l].reshape(1, D_MODEL)
        g2 = norm2_g[l].reshape(1, D_MODEL)
        lng = ln_v_g[l].reshape(1, D_MODEL)
        lnb = ln_v_b[l].reshape(1, D_MODEL)

        ws_p = jnp.where(causal[None], w_s[l], 0.0).astype(BF16)
        bs_p = jnp.repeat(b_s[l].T, D_MODEL // A_GROUPS, axis=1)
        blk = jnp.where(causal[:t_s, :t_s][None], w_s[l][:, :t_s, :t_s], 0.0)
        ws_s = jnp.where(same_seq[None], jnp.tile(blk, (1, CHUNK // t_s, CHUNK // t_s)), 0.0).astype(BF16)
        bs_s = jnp.tile(bs_p[:t_s], (CHUNK // t_s, 1))

        zp = _inproj_call(xp, mod_p, g1, cos_p, sin_p, dq_p, dk_p, lng, lnb, w_in_b,
                          layer=l, per_row=False, z_dtype=BF16)
        zs = _inproj_call(xs, mod_s, g1, cos_s, sin_s, dq_s, dk_s, lng, lnb, w_in_b,
                          layer=l, per_row=True, z_dtype=F32)

        xp, r_prompt = _mix_call(cd_p, zp, xp, mod_p, ws_p, bs_p, mask_p, wa, wb, wo,
                                 nseq=n_p, nchunk=chunks_p, layer=l, stacked=r_prompt)
        cross, r_sample = _ret_sample_call(cd_s, zs, state_ret, l, t_s, r_sample)
        xs = _mix_call(cd_s, zs, xs, mod_s, ws_s, bs_s, mask_s, wa, wb, wo,
                       nseq=chunks_s, nchunk=1, layer=l, cross=cross)

        tail = ROUTER_ROWS - GROUP_ROWS - N_EXPERTS
        wr = jnp.concatenate([jnp.pad(w_router_group[l].T, ((0, GROUP_ROWS - N_GROUPS), (0, 0))),
                              jnp.pad(w_router_expert[l].T, ((0, tail), (0, 0)))], axis=0)
        br = jnp.concatenate([jnp.pad(b_router_group[l], (0, GROUP_ROWS - N_GROUPS), constant_values=NEG),
                              jnp.pad(b_router_expert[l], (0, tail))])
        wr2 = jnp.concatenate(_split_bf16(wr), axis=0)
        br = jnp.broadcast_to(br[:, None], (ROUTER_ROWS, LANES))
        last = l == DEPTH - 1
        xp = _moe_call(xp, mod_p, g2, wr2, br, w_exp, fg,
                       layer=l, block=MOE_BLOCK, per_row=False, final_norm=last)
        xs = _moe_call(xs, mod_s, g2, wr2, br, w_exp, fg,
                       layer=l, block=n_s * t_s, per_row=True, final_norm=last)

        v_sample.append(zs[:, OFF_V:OFF_V + D_MODEL].reshape(n_s, t_s, D_MODEL))

    return (xp.reshape(n_p, t_p, D_MODEL), xs.reshape(n_s, t_s, D_MODEL),
            r_prompt, r_sample, jnp.stack(v_sample))
```

```python
import functools

import jax
import jax.numpy as jnp
from jax import lax
from jax.experimental import pallas as pl
from jax.experimental.pallas import tpu as pltpu

D_MODEL = 1024
DEPTH = 2
PAST_LEN = 16384
A_GROUPS = 8
CHUNK = 128
R_HEADS = 4
R_QK_DIM = 256
R_V_DIM = 512
ROPE_BASE = 10000.0
N_GROUPS = 4
EXPERTS_PER_GROUP = 8
N_EXPERTS = 32
EXPERT_DIM = 256
EPS = 1e-6
IN_COLS = 10 * D_MODEL
MOD_COLS = 6 * D_MODEL

OFF_U, OFF_V, OFF_Q, OFF_K, OFF_VR, OFF_GR, OFF_GA, OFF_GB = 0, 1024, 2048, 3072, 4096, 6144, 8192, 9216

LANES = 128
IN_TILE_BF16 = 256
IN_TILE_F32 = 256
MIX_SEQS = 2
RET_ROWS = 8
MOE_BLOCK = 1024
EXP_TILE = 128
ROW_ALIGN = 16
GATHER_CHUNK = 256
GROUP_ROWS = 8
ROUTER_ROWS = 48
EXPERTS_PER_STEP = 4
MOD_TILE = 1536
VMEM_LIMIT = 56 * 1024 * 1024
NEG = -1e30

BF16 = jnp.bfloat16
F32 = jnp.float32


def _dot(a, b):
    return jnp.dot(a, b, preferred_element_type=F32)


def _dot_f32(a, b):
    return jnp.dot(a, b, preferred_element_type=F32, precision=lax.Precision.HIGHEST)


def _sigmoid(x):
    return 1.0 / (1.0 + jnp.exp(-x))


def _gelu_tanh(x):
    return x * (0.5 * (1.0 + jnp.tanh(0.7978845608028654 * (x + 0.044715 * (x * x * x)))))


def _rms(x):
    return x * lax.rsqrt(jnp.mean(x * x, axis=-1, keepdims=True) + EPS)


def _mod_kernel(c_ref, w_ref, b_ref, o_ref):
    c = c_ref[...]
    o_ref[...] = _dot_f32(c * _sigmoid(c), w_ref[...]) + b_ref[...]


def _mod_call(c_all, w_mod, b_mod):
    n = c_all.shape[0]
    return pl.pallas_call(
        _mod_kernel,
        grid=(DEPTH, MOD_COLS // MOD_TILE),
        in_specs=[
            pl.BlockSpec((n, D_MODEL), lambda l, j: (0, 0)),
            pl.BlockSpec((None, D_MODEL, MOD_TILE), lambda l, j: (l, 0, j)),
            pl.BlockSpec((None, 1, MOD_TILE), lambda l, j: (l, 0, j)),
        ],
        out_specs=pl.BlockSpec((None, n, MOD_TILE), lambda l, j: (l, 0, j)),
        out_shape=jax.ShapeDtypeStruct((DEPTH, n, MOD_COLS), F32),
        compiler_params=pltpu.CompilerParams(vmem_limit_bytes=VMEM_LIMIT),
        name="adaln_mod",
    )(c_all, w_mod, b_mod.reshape(DEPTH, 1, MOD_COLS))


def _inproj_kernel(x_ref, sh_ref, sc_ref, g_ref, cos_ref, sin_ref, dq_ref, dk_ref, lng_ref, lnb_ref,
                   w_ref, z_ref):
    h = (_rms(x_ref[...]) * g_ref[...]) * (1.0 + sc_ref[...]) + sh_ref[...]
    h = h.astype(BF16)
    cos = cos_ref[...]
    sin = sin_ref[...]

    def proj(off, width=D_MODEL):
        return _dot(h, w_ref[:, off:off + width])

    def put(off, val):
        z_ref[:, off:off + val.shape[1]] = val.astype(z_ref.dtype)

    put(OFF_U, _gelu_tanh(proj(OFF_U)))

    v = _gelu_tanh(proj(OFF_V))
    mu = jnp.mean(v, axis=-1, keepdims=True)
    vc = v - mu
    var = jnp.mean(vc * vc, axis=-1, keepdims=True)
    put(OFF_V, vc * lax.rsqrt(var + EPS) * lng_ref[...] + lnb_ref[...])

    half = R_QK_DIM // 2
    for off, d_ref in ((OFF_Q, dq_ref), (OFF_K, dk_ref)):
        acc = proj(off)
        for hd in range(R_HEADS):
            x1 = acc[:, hd * R_QK_DIM:hd * R_QK_DIM + half]
            x2 = acc[:, hd * R_QK_DIM + half:(hd + 1) * R_QK_DIM]
            dec = d_ref[:, hd * LANES:(hd + 1) * LANES]
            put(off + hd * R_QK_DIM, (x1 * cos - x2 * sin) * dec)
            put(off + hd * R_QK_DIM + half, (x1 * sin + x2 * cos) * dec)

    for j in range(2):
        put(OFF_VR + j * D_MODEL, proj(OFF_VR + j * D_MODEL))
    for j in range(2):
        g = proj(OFF_GR + j * D_MODEL)
        put(OFF_GR + j * D_MODEL, g * _sigmoid(g))
    put(OFF_GA, _sigmoid(proj(OFF_GA)))
    put(OFF_GB, _sigmoid(proj(OFF_GB)))


def _inproj_call(x, mod, g1, cos, sin, dq, dk, lng, lnb, w_in, *, layer, per_row, z_dtype):
    rows = x.shape[0]
    tile = dq.shape[0]
    tiles_per_seq = None if per_row else (rows // mod.shape[0]) // tile
    if per_row:
        sh_spec = pl.BlockSpec((None, tile, D_MODEL), lambda i: (layer, i, 0))
        sc_spec = pl.BlockSpec((None, tile, D_MODEL), lambda i: (layer, i, 1))
        rope_spec = pl.BlockSpec((tile, LANES), lambda i: (0, 0))
    else:
        sh_spec = pl.BlockSpec((None, 1, D_MODEL), lambda i: (i // tiles_per_seq, 0, 0))
        sc_spec = pl.BlockSpec((None, 1, D_MODEL), lambda i: (i // tiles_per_seq, 0, 1))
        rope_spec = pl.BlockSpec((tile, LANES), lambda i: (i % tiles_per_seq, 0))
    const = lambda i: (0, 0)
    return pl.pallas_call(
        _inproj_kernel,
        grid=(rows // tile,),
        in_specs=[
            pl.BlockSpec((tile, D_MODEL), lambda i: (i, 0)),
            sh_spec, sc_spec,
            pl.BlockSpec((1, D_MODEL), const),
            rope_spec, rope_spec,
            pl.BlockSpec((tile, R_HEADS * LANES), const),
            pl.BlockSpec((tile, R_HEADS * LANES), const),
            pl.BlockSpec((1, D_MODEL), const),
            pl.BlockSpec((1, D_MODEL), const),
            pl.BlockSpec((None, D_MODEL, IN_COLS), lambda i: (layer, 0, 0), pipeline_mode=pl.Buffered(1)),
        ],
        out_specs=pl.BlockSpec((tile, IN_COLS), lambda i: (i, 0)),
        out_shape=jax.ShapeDtypeStruct((rows, IN_COLS), z_dtype),
        compiler_params=pltpu.CompilerParams(vmem_limit_bytes=VMEM_LIMIT),
        name="in_proj",
    )(x, mod, mod, g1, cos, sin, dq, dk, lng, lnb, w_in)


def _ret_sample_kernel(cd_ref, z_q_ref, z_k_ref, z_v_ref, r0_ref, *rest, t_s):
    cross_ref, r_ref = rest[-2:]
    q = z_q_ref[...].astype(BF16)
    k = z_k_ref[...]
    v = z_v_ref[...].astype(BF16)
    row_k = lax.broadcasted_iota(jnp.int32, (RET_ROWS, R_QK_DIM), 0)
    row_v = lax.broadcasted_iota(jnp.int32, (RET_ROWS, R_V_DIM), 0)
    for hd in range(R_HEADS):
        qh = q[:, hd * R_QK_DIM:(hd + 1) * R_QK_DIM]
        kh = k[:, hd * R_QK_DIM:(hd + 1) * R_QK_DIM]
        vh = v[:, hd * R_V_DIM:(hd + 1) * R_V_DIM]
        cross = jnp.zeros((RET_ROWS, R_V_DIM), F32)
        for s in range(RET_ROWS // t_s):
            r = r0_ref[s, hd]
            lo, hi = s * t_s, (s + 1) * t_s
            cross = jnp.where((row_v >= lo) & (row_v < hi), _dot(qh, r.astype(BF16)), cross)
            k_seq = jnp.where((row_k >= lo) & (row_k < hi), kh, 0.0).astype(BF16)
            upd = lax.dot_general(k_seq, vh, (((0,), (0,)), ((), ())), preferred_element_type=F32)
            r_ref[s, hd] = cd_ref[hd] * (r + upd)
        cross_ref[:, hd * R_V_DIM:(hd + 1) * R_V_DIM] = cross


def _ret_sample_call(cd, z, state, layer, t_s, stacked=None):
    rows = z.shape[0]
    seqs = RET_ROWS // t_s
    qk_w = R_HEADS * R_QK_DIM
    v_w = R_HEADS * R_V_DIM
    state_spec = pl.BlockSpec((None, seqs, R_HEADS, R_QK_DIM, R_V_DIM), lambda b: (layer, b, 0, 0, 0))
    in_specs = [
        pl.BlockSpec(memory_space=pltpu.SMEM),
        pl.BlockSpec((RET_ROWS, qk_w), lambda b: (b, OFF_Q // qk_w)),
        pl.BlockSpec((RET_ROWS, qk_w), lambda b: (b, OFF_K // qk_w)),
        pl.BlockSpec((RET_ROWS, v_w), lambda b: (b, OFF_VR // v_w)),
        state_spec,
    ]
    args = [cd, z, z, z, state]
    aliases = {}
    if stacked is not None:
        aliases = {len(args): 1}
        in_specs.append(pl.BlockSpec(memory_space=pl.ANY))
        args.append(stacked)
    return pl.pallas_call(
        functools.partial(_ret_sample_kernel, t_s=t_s),
        grid=(rows // RET_ROWS,),
        in_specs=in_specs,
        out_specs=[
            pl.BlockSpec((RET_ROWS, v_w), lambda b: (b, 0)),
            state_spec,
        ],
        out_shape=[
            jax.ShapeDtypeStruct((rows, v_w), F32),
            jax.ShapeDtypeStruct(state.shape, F32),
        ],
        input_output_aliases=aliases,
        compiler_params=pltpu.CompilerParams(vmem_limit_bytes=VMEM_LIMIT),
        name="ret_sample",
    )(*args)


def _mix_kernel(cd_ref, z_ref, x_ref, gt_ref, ws_ref, bs_ref, mask_ref, wa_ref, wb_ref, wo_ref, *rest,
                carried_state):
    nsb, nrows = z_ref.shape[:2]
    per_sb = nrows // CHUNK
    nseq = nsb * per_sb
    if carried_state:
        cross_ref, xo_ref = rest
    else:
        xo_ref, r_ref = rest[-2:]

        @pl.when(pl.program_id(1) == 0)
        def _():
            r_ref[...] = jnp.zeros_like(r_ref)

    mask = mask_ref[...]
    gdim = D_MODEL // A_GROUPS
    a_rows, b_rows = [], []
    for sq in range(nseq):
        sb, chunk = divmod(sq, per_sb)
        rs = slice(chunk * CHUNK, (chunk + 1) * CHUNK)

        def sec(off, width, sb=sb, rs=rs):
            return z_ref[sb, rs, off:off + width]

        v = sec(OFF_V, D_MODEL).astype(BF16)
        mixed = jnp.concatenate(
            [_dot(ws_ref[g], v[:, g * gdim:(g + 1) * gdim]) for g in range(A_GROUPS)], axis=1)
        a_rows.append((sec(OFF_U, D_MODEL).astype(F32) * (mixed + bs_ref[...])).astype(BF16))

        b_parts = []
        for hd in range(R_HEADS):
            qh = sec(OFF_Q + hd * R_QK_DIM, R_QK_DIM).astype(BF16)
            kh = sec(OFF_K + hd * R_QK_DIM, R_QK_DIM).astype(BF16)
            vh = sec(OFF_VR + hd * R_V_DIM, R_V_DIM).astype(BF16)
            s = lax.dot_general(qh, kh, (((1,), (1,)), ((), ())), preferred_element_type=F32) * mask
            y = _dot(s.astype(BF16), vh)
            if carried_state:
                y = y + cross_ref[sb, rs, hd * R_V_DIM:(hd + 1) * R_V_DIM]
            else:
                r = r_ref[sq, hd]
                y = y + _dot(qh, r.astype(BF16))
                upd = lax.dot_general(kh, vh, (((0,), (0,)), ((), ())), preferred_element_type=F32)
                r_ref[sq, hd] = cd_ref[hd] * (r + upd)
            mu = jnp.mean(y, axis=-1, keepdims=True)
            yc = y - mu
            var = jnp.mean(yc * yc, axis=-1, keepdims=True)
            yn = yc * lax.rsqrt(var + EPS)
            b_parts.append((sec(OFF_GR + hd * R_V_DIM, R_V_DIM).astype(F32) * yn).astype(BF16))
        b_rows.append(jnp.concatenate(b_parts, axis=1))

    pa = _dot(jnp.concatenate(a_rows, axis=0), wa_ref[...])
    pb = _dot(jnp.concatenate(b_rows, axis=0), wb_ref[...])
    ga = z_ref[:, :, OFF_GA:OFF_GA + D_MODEL].reshape(nseq * CHUNK, D_MODEL).astype(F32)
    gb = z_ref[:, :, OFF_GB:OFF_GB + D_MODEL].reshape(nseq * CHUNK, D_MODEL).astype(F32)
    mix = _dot((ga * pa + gb * pb).astype(BF16), wo_ref[...])
    xo_ref[...] = x_ref[...] + gt_ref[...] * mix.reshape(nsb, nrows, D_MODEL)


def _mix_call(cd, z, x, mod, ws, bs, mask, wa, wb, wo, *, nseq, nchunk, cross=None, layer=0, stacked=None):
    rows = x.shape[0]
    carried = cross is not None
    aliases = {}
    if carried:
        vseq, vlen, sb, rb = 1, rows, 1, MIX_SEQS * CHUNK
    else:
        vseq, vlen, sb, rb = nseq, nchunk * CHUNK, MIX_SEQS, CHUNK
    assert vseq % sb == 0 and vlen % rb == 0
    view = lambda a: a.reshape(vseq, vlen, a.shape[-1])
    row_map = lambda b, c: (b, c, 0)
    const2 = lambda b, c: (0, 0)
    wmap = lambda b, c: (layer, 0, 0)
    if carried:
        mod = mod.reshape(mod.shape[0], vseq, vlen, MOD_COLS)
        gt_spec = pl.BlockSpec((None, sb, rb, D_MODEL), lambda b, c: (layer, b, c, 2))
    else:
        gt_spec = pl.BlockSpec((sb, 1, D_MODEL), lambda b, c: (b, 0, 2))
    in_specs = [
        pl.BlockSpec(memory_space=pltpu.SMEM),
        pl.BlockSpec((sb, rb, IN_COLS), row_map),
        pl.BlockSpec((sb, rb, D_MODEL), row_map),
        gt_spec,
        pl.BlockSpec((A_GROUPS, CHUNK, CHUNK), lambda b, c: (0, 0, 0)),
        pl.BlockSpec((CHUNK, D_MODEL), const2),
        pl.BlockSpec((CHUNK, CHUNK), const2),
        pl.BlockSpec((None, D_MODEL, D_MODEL), wmap),
        pl.BlockSpec((None, R_HEADS * R_V_DIM, D_MODEL), wmap),
        pl.BlockSpec((None, D_MODEL, D_MODEL), wmap),
    ]
    args = [cd, view(z), view(x), mod, ws, bs, mask, wa, wb, wo]
    x_out = jax.ShapeDtypeStruct((vseq, vlen, D_MODEL), F32)
    x_spec = pl.BlockSpec((sb, rb, D_MODEL), row_map)
    if carried:
        in_specs.append(pl.BlockSpec((sb, rb, R_HEADS * R_V_DIM), row_map))
        args.append(view(cross))
        out_specs, out_shape = x_spec, x_out
    else:
        out_specs = [x_spec, pl.BlockSpec((None, sb, R_HEADS, R_QK_DIM, R_V_DIM),
                                          lambda b, c: (layer, b, 0, 0, 0))]
        out_shape = [x_out, jax.ShapeDtypeStruct((DEPTH, nseq, R_HEADS, R_QK_DIM, R_V_DIM), F32)]
        if stacked is not None:
            aliases = {len(args): 1}
            in_specs.append(pl.BlockSpec(memory_space=pl.ANY))
            args.append(stacked)
    out = pl.pallas_call(
        functools.partial(_mix_kernel, carried_state=carried),
        grid=(vseq // sb, vlen // rb),
        in_specs=in_specs,
        out_specs=out_specs,
        out_shape=out_shape,
        input_output_aliases=aliases,
        compiler_params=pltpu.CompilerParams(
            dimension_semantics=("arbitrary", "arbitrary"), vmem_limit_bytes=VMEM_LIMIT),
        name="token_mix",
    )(*args)
    if carried:
        return out.reshape(rows, D_MODEL)
    return out[0].reshape(rows, D_MODEL), out[1]


def _sorted_rows(block):
    need = 2 * block + N_EXPERTS * ROW_ALIGN + EXP_TILE
    return -(-need // GATHER_CHUNK) * GATHER_CHUNK


def _split_bf16(w):
    hi = w.astype(BF16)
    return hi, (w - hi.astype(F32)).astype(BF16)


def _route_kernel(x_ref, sh_ref, sc_ref, g_ref, wr2_ref, br_ref, tri_ref, low_ref,
                  h_ref, route_ref, off_ref, nt_ref):
    blk = x_ref.shape[0]
    h = (_rms(x_ref[...]) * g_ref[...]) * (1.0 + sc_ref[...]) + sh_ref[...]
    h_hi, h_lo = _split_bf16(h)
    h_ref[...] = h_hi

    nt_dims = (((1,), (1,)), ((), ()))
    nr = ROUTER_ROWS
    r_hi = lax.dot_general(wr2_ref[...], h_hi, nt_dims, preferred_element_type=F32)
    r_lo = lax.dot_general(wr2_ref[0:nr, :], h_lo, nt_dims, preferred_element_type=F32)
    logits = r_hi[0:nr] + r_hi[nr:2 * nr] + r_lo + br_ref[:, :1]

    gl = logits[0:GROUP_ROWS]
    grow = lax.broadcasted_iota(jnp.int32, (GROUP_ROWS, blk), 0).astype(F32)
    gmax = jnp.max(gl, axis=0, keepdims=True)
    g_w = 1.0 / jnp.sum(jnp.exp(gl - gmax), axis=0, keepdims=True)
    g_idx = jnp.min(jnp.where(gl == gmax, grow, float(GROUP_ROWS)), axis=0, keepdims=True)

    el = logits[GROUP_ROWS:GROUP_ROWS + N_EXPERTS]
    erow_i = lax.broadcasted_iota(jnp.int32, (N_EXPERTS, blk), 0)
    erow = erow_i.astype(F32)
    el = jnp.where((erow_i >> 3).astype(F32) == g_idx, el, NEG)
    m1 = jnp.max(el, axis=0, keepdims=True)
    i1 = jnp.min(jnp.where(el == m1, erow, float(N_EXPERTS)), axis=0, keepdims=True)
    el2 = jnp.where(erow == i1, NEG, el)
    m2 = jnp.max(el2, axis=0, keepdims=True)
    i2 = jnp.min(jnp.where(el2 == m2, erow, float(N_EXPERTS)), axis=0, keepdims=True)
    t = jnp.exp(m2 - m1)
    w1 = g_w / (1.0 + t)
    w2 = g_w * t / (1.0 + t)

    s1 = erow == i1
    s2 = erow == i2
    onehot = jnp.where(s1 | s2, 1.0, 0.0)
    rank = _dot(onehot.astype(BF16), tri_ref[...])
    cnt = jnp.sum(onehot, axis=1, keepdims=True)
    units = jnp.floor((cnt + (ROW_ALIGN - 1.0)) * (1.0 / ROW_ALIGN))
    units = jnp.broadcast_to(units, (N_EXPERTS, LANES))
    off = float(ROW_ALIGN) * _dot(low_ref[...], units.astype(BF16))
    base = off[:, :1] + rank
    pos1 = jnp.sum(jnp.where(s1, base, 0.0), axis=0, keepdims=True)
    pos2 = jnp.sum(jnp.where(s2, base, 0.0), axis=0, keepdims=True)

    r8 = lax.broadcasted_iota(jnp.int32, (8, blk), 0)
    route_ref[...] = jnp.where(r8 == 0, pos1, jnp.where(r8 == 1, pos2, jnp.where(r8 == 2, w1,
                               jnp.where(r8 == 3, w2, 0.0))))
    off_ref[...] = off.astype(jnp.int32)
    tiles = jnp.floor((cnt + (EXP_TILE - 1.0)) * (1.0 / EXP_TILE))
    nt_ref[...] = jnp.broadcast_to(tiles, (N_EXPERTS, LANES)).astype(jnp.int32)


def _experts_kernel(off_sm, nt_sm, x_ref, h_ref, route_ref, gt_ref, wg_ref, wu_ref, wd_ref, fg_ref, o_ref,
                    p_scr, xs_scr, ys_scr, cw_scr, *, final_norm):
    i = pl.program_id(0)
    step = pl.program_id(1)
    nrows, blk = p_scr.shape

    @pl.when(step == 0)
    def _():
        pos1 = route_ref[0:1, :]
        pos2 = route_ref[1:2, :]
        w1 = route_ref[2:3, :]
        w2 = route_ref[3:4, :]
        row_iota = lax.broadcasted_iota(jnp.int32, (GATHER_CHUNK, blk), 0)

        def gather(c, carry):
            r0 = pl.multiple_of(c * GATHER_CHUNK, GATHER_CHUNK)
            prow = (row_iota + r0).astype(F32)
            m1 = prow == pos1
            m2 = prow == pos2
            perm = jnp.where(m1 | m2, 1.0, 0.0).astype(BF16)
            p_scr[pl.ds(r0, GATHER_CHUNK), :] = perm
            xs_scr[pl.ds(r0, GATHER_CHUNK), :] = _dot(perm, h_ref[...]).astype(BF16)
            cw = jnp.sum(jnp.where(m1, w1, 0.0) + jnp.where(m2, w2, 0.0), axis=1, keepdims=True)
            cw_scr[pl.ds(r0, GATHER_CHUNK), :] = jnp.broadcast_to(cw, (GATHER_CHUNK, LANES))
            ys_scr[pl.ds(r0, GATHER_CHUNK), :] = jnp.zeros((GATHER_CHUNK, D_MODEL), BF16)
            return carry

        lax.fori_loop(0, nrows // GATHER_CHUNK, gather, 0)

    def tile(g, s):
        s = pl.multiple_of(s, ROW_ALIGN)
        rows = xs_scr[pl.ds(s, EXP_TILE), :]
        gate = _dot(rows, wg_ref[g])
        up = _dot(rows, wu_ref[g])
        cw = cw_scr[pl.ds(s, EXP_TILE), :][:, :1]
        hid = (gate * _sigmoid(gate) * up * cw).astype(BF16)
        return s, _dot(hid, wd_ref[g]).astype(BF16)

    def put(s, y):
        ys_scr[pl.ds(s, EXP_TILE), :] = y

    starts = [off_sm[i, step * EXPERTS_PER_STEP + g] for g in range(EXPERTS_PER_STEP)]
    first = [tile(g, starts[g]) for g in range(EXPERTS_PER_STEP)]
    for g in range(EXPERTS_PER_STEP):
        put(*first[g])

        def more(t, carry, g=g):
            put(*tile(g, starts[g] + t * EXP_TILE))
            return carry

        lax.fori_loop(1, nt_sm[i, step * EXPERTS_PER_STEP + g], more, 0)

    @pl.when(step == N_EXPERTS // EXPERTS_PER_STEP - 1)
    def _():
        tn_dims = (((0,), (0,)), ((), ()))
        for r0 in range(0, blk, GATHER_CHUNK):
            rs = slice(r0, r0 + GATHER_CHUNK)
            moe = lax.dot_general(p_scr[:, rs], ys_scr[...], tn_dims, preferred_element_type=F32)
            gt = gt_ref[...] if gt_ref.shape[0] == 1 else gt_ref[rs, :]
            y = x_ref[rs, :] + gt * moe
            if final_norm:
                y = _rms(y) * fg_ref[...]
            o_ref[rs, :] = y


def _moe_call(x, mod, g2, wr2, br, w_exp, fg, *, layer, block, per_row, final_norm):
    rows = x.shape[0]
    nblk = rows // block
    nsorted = _sorted_rows(block)
    if per_row:
        mspec = lambda s: pl.BlockSpec((None, block, D_MODEL), lambda i, *_: (layer, i, s))
    else:
        blocks_per_seq = (rows // mod.shape[0]) // block
        mspec = lambda s: pl.BlockSpec((None, 1, D_MODEL), lambda i, *_: (i // blocks_per_seq, 0, s))
    const = lambda i: (0, 0)
    ridx = jnp.arange(block)
    tri = (ridx[:, None] < ridx[None, :]).astype(BF16)
    eidx = jnp.arange(N_EXPERTS)
    low = (eidx[None, :] < eidx[:, None]).astype(BF16)
    h, route, off, nt = pl.pallas_call(
        _route_kernel,
        grid=(nblk,),
        in_specs=[
            pl.BlockSpec((block, D_MODEL), lambda i: (i, 0)),
            mspec(3), mspec(4),
            pl.BlockSpec((1, D_MODEL), const),
            pl.BlockSpec((2 * ROUTER_ROWS, D_MODEL), const),
            pl.BlockSpec((ROUTER_ROWS, LANES), const),
            pl.BlockSpec((block, block), const),
            pl.BlockSpec((N_EXPERTS, N_EXPERTS), const),
        ],
        out_specs=[
            pl.BlockSpec((block, D_MODEL), lambda i: (i, 0)),
            pl.BlockSpec((None, 8, block), lambda i: (i, 0, 0)),
            pl.BlockSpec((None, N_EXPERTS, LANES), lambda i: (i, 0, 0)),
            pl.BlockSpec((None, N_EXPERTS, LANES), lambda i: (i, 0, 0)),
        ],
        out_shape=[
            jax.ShapeDtypeStruct((rows, D_MODEL), BF16),
            jax.ShapeDtypeStruct((nblk, 8, block), F32),
            jax.ShapeDtypeStruct((nblk, N_EXPERTS, LANES), jnp.int32),
            jax.ShapeDtypeStruct((nblk, N_EXPERTS, LANES), jnp.int32),
        ],
        compiler_params=pltpu.CompilerParams(vmem_limit_bytes=VMEM_LIMIT),
        name="moe_route",
    )(x, mod, mod, g2, wr2, br, tri, low)

    const2 = lambda i, e, *_: (0, 0)
    wmap = lambda i, s, *_: (layer, s, 0, 0)
    return pl.pallas_call(
        functools.partial(_experts_kernel, final_norm=final_norm),
        grid_spec=pltpu.PrefetchScalarGridSpec(
            num_scalar_prefetch=2,
            grid=(nblk, N_EXPERTS // EXPERTS_PER_STEP),
            in_specs=[
                pl.BlockSpec((block, D_MODEL), lambda i, e, *_: (i, 0)),
                pl.BlockSpec((block, D_MODEL), lambda i, e, *_: (i, 0), pipeline_mode=pl.Buffered(1)),
                pl.BlockSpec((None, 8, block), lambda i, e, *_: (i, 0, 0)),
                mspec(5),
                pl.BlockSpec((None, EXPERTS_PER_STEP, D_MODEL, EXPERT_DIM), wmap),
                pl.BlockSpec((None, EXPERTS_PER_STEP, D_MODEL, EXPERT_DIM), wmap),
                pl.BlockSpec((None, EXPERTS_PER_STEP, EXPERT_DIM, D_MODEL), wmap),
                pl.BlockSpec((1, D_MODEL), const2),
            ],
            out_specs=pl.BlockSpec((block, D_MODEL), lambda i, e, *_: (i, 0)),
            scratch_shapes=[
                pltpu.VMEM((nsorted, block), BF16),
                pltpu.VMEM((nsorted, D_MODEL), BF16),
                pltpu.VMEM((nsorted, D_MODEL), BF16),
                pltpu.VMEM((nsorted, LANES), F32),
            ],
        ),
        out_shape=jax.ShapeDtypeStruct((rows, D_MODEL), F32),
        compiler_params=pltpu.CompilerParams(
            dimension_semantics=("arbitrary", "arbitrary"), vmem_limit_bytes=VMEM_LIMIT),
        name="moe_experts",
    )(off[:, :, 0], nt[:, :, 0], x, h, route, mod, *w_exp, fg)


def _rope_tables(pos):
    half = R_QK_DIM // 2
    inv_freq = jnp.power(ROPE_BASE, -jnp.arange(half, dtype=F32) / half)
    ang = pos.astype(F32)[:, None] * inv_freq[None, :]
    return jnp.cos(ang), jnp.sin(ang)


def _decay_tables(chunk, rows):
    log_g = jnp.log1p(-jnp.power(2.0, -5.0 - jnp.arange(R_HEADS, dtype=F32)))
    t1 = (jnp.arange(rows) % chunk).astype(F32) + 1.0
    dq = jnp.exp(log_g[None, :] * t1[:, None])
    dk = jnp.exp(-log_g[None, :] * t1[:, None]) * (R_QK_DIM ** -0.5)
    cd = jnp.exp(log_g * chunk)
    rep = lambda a: jnp.repeat(a, LANES, axis=1)
    return rep(dq), rep(dk), cd


def kernel(x_prompt, x_sample, state_ret, c_prompt, c_sample, w_mod, b_mod, norm1_g, w_in, ln_v_g, ln_v_b,
           w_s, b_s, w_a_out, w_b_out, w_o, norm2_g, w_router_group, b_router_group, w_router_expert,
           b_router_expert, w_gate, w_up, w_down, final_g):
    n_p, t_p, _ = x_prompt.shape
    n_s, t_s, _ = x_sample.shape
    assert t_p % CHUNK == 0 and (n_s * t_s) % CHUNK == 0 and CHUNK % t_s == 0 and RET_ROWS % t_s == 0
    chunks_p = t_p // CHUNK
    chunks_s = (n_s * t_s) // CHUNK

    mod = _mod_call(jnp.concatenate([c_prompt, c_sample], axis=0), w_mod, b_mod)

    cos_p, sin_p = _rope_tables(jnp.arange(t_p, dtype=jnp.int32))
    cos_s, sin_s = _rope_tables(PAST_LEN + jnp.arange(IN_TILE_F32, dtype=jnp.int32) % t_s)
    dq_p, dk_p, cd_p = _decay_tables(CHUNK, IN_TILE_BF16)
    dq_s, dk_s, cd_s = _decay_tables(t_s, IN_TILE_F32)

    idx = jnp.arange(CHUNK)
    causal = idx[:, None] >= idx[None, :]
    same_seq = (idx[:, None] // t_s) == (idx[None, :] // t_s)
    mask_p = causal.astype(F32)
    mask_s = (causal & same_seq).astype(F32)

    xp = x_prompt.reshape(n_p * t_p, D_MODEL)
    xs = x_sample.reshape(n_s * t_s, D_MODEL)
    fg = final_g.reshape(1, D_MODEL)
    w_in_b = w_in.astype(BF16)
    wa, wb, wo = w_a_out.astype(BF16), w_b_out.astype(BF16), w_o.astype(BF16)
    w_exp = (w_gate.astype(BF16), w_up.astype(BF16), w_down.astype(BF16))
    mod_s = jnp.take(mod[:, n_p:], jnp.arange(n_s * t_s) // t_s, axis=1)
    r_prompt, r_sample, v_sample = None, None, []
    for l in range(DEPTH):
        mod_p = mod[l, :n_p].reshape(n_p, 1, MOD_COLS)
        g1 = norm1_g[l].reshape(1, D_MODEL)
        g2 = norm2_g[l].reshape(1, D_MODEL)
        lng = ln_v_g[l].reshape(1, D_MODEL)
        lnb = ln_v_b[l].reshape(1, D_MODEL)

        ws_p = jnp.where(causal[None], w_s[l], 0.0).astype(BF16)
        bs_p = jnp.repeat(b_s[l].T, D_MODEL // A_GROUPS, axis=1)
        blk = jnp.where(causal[:t_s, :t_s][None], w_s[l][:, :t_s, :t_s], 0.0)
        ws_s = jnp.where(same_seq[None], jnp.tile(blk, (1, CHUNK // t_s, CHUNK // t_s)), 0.0).astype(BF16)
        bs_s = jnp.tile(bs_p[:t_s], (CHUNK // t_s, 1))

        zp = _inproj_call(xp, mod_p, g1, cos_p, sin_p, dq_p, dk_p, lng, lnb, w_in_b,
                          layer=l, per_row=False, z_dtype=BF16)
        zs = _inproj_call(xs, mod_s, g1, cos_s, sin_s, dq_s, dk_s, lng, lnb, w_in_b,
                          layer=l, per_row=True, z_dtype=F32)

        xp, r_prompt = _mix_call(cd_p, zp, xp, mod_p, ws_p, bs_p, mask_p, wa, wb, wo,
                                 nseq=n_p, nchunk=chunks_p, layer=l, stacked=r_prompt)
        cross, r_sample = _ret_sample_call(cd_s, zs, state_ret, l, t_s, r_sample)
        xs = _mix_call(cd_s, zs, xs, mod_s, ws_s, bs_s, mask_s, wa, wb, wo,
                       nseq=chunks_s, nchunk=1, layer=l, cross=cross)

        tail = ROUTER_ROWS - GROUP_ROWS - N_EXPERTS
        wr = jnp.concatenate([jnp.pad(w_router_group[l].T, ((0, GROUP_ROWS - N_GROUPS), (0, 0))),
                              jnp.pad(w_router_expert[l].T, ((0, tail), (0, 0)))], axis=0)
        br = jnp.concatenate([jnp.pad(b_router_group[l], (0, GROUP_ROWS - N_GROUPS), constant_values=NEG),
                              jnp.pad(b_router_expert[l], (0, tail))])
        wr2 = jnp.concatenate(_split_bf16(wr), axis=0)
        br = jnp.broadcast_to(br[:, None], (ROUTER_ROWS, LANES))
        last = l == DEPTH - 1
        xp = _moe_call(xp, mod_p, g2, wr2, br, w_exp, fg,
                       layer=l, block=MOE_BLOCK, per_row=False, final_norm=last)
        xs = _moe_call(xs, mod_s, g2, wr2, br, w_exp, fg,
                       layer=l, block=n_s * t_s, per_row=True, final_norm=last)

        v_sample.append(zs[:, OFF_V:OFF_V + D_MODEL].reshape(n_s, t_s, D_MODEL))

    return (xp.reshape(n_p, t_p, D_MODEL), xs.reshape(n_s, t_s, D_MODEL),
            r_prompt, r_sample, jnp.stack(v_sample))
```

```python
import functools

import jax
import jax.numpy as jnp
from jax import lax
from jax.experimental import pallas as pl
from jax.experimental.pallas import tpu as pltpu

D_MODEL = 1024
DEPTH = 2
PAST_LEN = 16384
A_GROUPS = 8
CHUNK = 128
R_HEADS = 4
R_QK_DIM = 256
R_V_DIM = 512
ROPE_BASE = 10000.0
N_GROUPS = 4
EXPERTS_PER_GROUP = 8
N_EXPERTS = 32
EXPERT_DIM = 256
EPS = 1e-6
IN_COLS = 10 * D_MODEL
MOD_COLS = 6 * D_MODEL

OFF_U, OFF_V, OFF_Q, OFF_K, OFF_VR, OFF_GR, OFF_GA, OFF_GB = 0, 1024, 2048, 3072, 4096, 6144, 8192, 9216

LANES = 128
IN_TILE_BF16 = 256
IN_TILE_F32 = 256
MIX_SEQS = 2
RET_ROWS = 16
MOE_BLOCK = 1024
EXP_TILE = 128
ROW_ALIGN = 16
GATHER_CHUNK = 256
GROUP_ROWS = 8
ROUTER_ROWS = 48
EXPERTS_PER_STEP = 4
MOD_TILE = 1536
VMEM_LIMIT = 56 * 1024 * 1024
NEG = -1e30

BF16 = jnp.bfloat16
F32 = jnp.float32


def _dot(a, b):
    return jnp.dot(a, b, preferred_element_type=F32)


def _sigmoid(x):
    return 1.0 / (1.0 + jnp.exp(-x))


def _gelu_tanh(x):
    return x * (0.5 * (1.0 + jnp.tanh(0.7978845608028654 * (x + 0.044715 * (x * x * x)))))


def _rms(x):
    return x * lax.rsqrt(jnp.mean(x * x, axis=-1, keepdims=True) + EPS)


def _split_bf16(w):
    hi = w.astype(BF16)
    return hi, (w - hi.astype(F32)).astype(BF16)


def _dot_split(a, b):
    a_hi, a_lo = _split_bf16(a)
    b_hi, b_lo = _split_bf16(b)
    return _dot(a_hi, b_hi) + (_dot(a_lo, b_hi) + _dot(a_hi, b_lo))


def _mod_kernel(c_ref, w_ref, b_ref, o_ref):
    c = c_ref[...]
    o_ref[...] = _dot_split(c * _sigmoid(c), w_ref[...]) + b_ref[...]


def _mod_call(c_all, w_mod, b_mod):
    n = c_all.shape[0]
    return pl.pallas_call(
        _mod_kernel,
        grid=(DEPTH, MOD_COLS // MOD_TILE),
        in_specs=[
            pl.BlockSpec((n, D_MODEL), lambda l, j: (0, 0)),
            pl.BlockSpec((None, D_MODEL, MOD_TILE), lambda l, j: (l, 0, j)),
            pl.BlockSpec((None, 1, MOD_TILE), lambda l, j: (l, 0, j)),
        ],
        out_specs=pl.BlockSpec((None, n, MOD_TILE), lambda l, j: (l, 0, j)),
        out_shape=jax.ShapeDtypeStruct((DEPTH, n, MOD_COLS), F32),
        compiler_params=pltpu.CompilerParams(vmem_limit_bytes=VMEM_LIMIT),
        name="adaln_mod",
    )(c_all, w_mod, b_mod.reshape(DEPTH, 1, MOD_COLS))


def _inproj_kernel(x_ref, sh_ref, sc_ref, g_ref, cos_ref, sin_ref, dq_ref, dk_ref, lng_ref, lnb_ref,
                   w_ref, z_ref):
    h = (_rms(x_ref[...]) * g_ref[...]) * (1.0 + sc_ref[...]) + sh_ref[...]
    h = h.astype(BF16)
    cos = cos_ref[...]
    sin = sin_ref[...]

    def proj(off, width=D_MODEL):
        return _dot(h, w_ref[:, off:off + width])

    def put(off, val):
        z_ref[:, off:off + val.shape[1]] = val.astype(z_ref.dtype)

    put(OFF_U, _gelu_tanh(proj(OFF_U)))

    v = _gelu_tanh(proj(OFF_V))
    mu = jnp.mean(v, axis=-1, keepdims=True)
    vc = v - mu
    var = jnp.mean(vc * vc, axis=-1, keepdims=True)
    put(OFF_V, vc * lax.rsqrt(var + EPS) * lng_ref[...] + lnb_ref[...])

    half = R_QK_DIM // 2
    for off, d_ref in ((OFF_Q, dq_ref), (OFF_K, dk_ref)):
        acc = proj(off)
        for hd in range(R_HEADS):
            x1 = acc[:, hd * R_QK_DIM:hd * R_QK_DIM + half]
            x2 = acc[:, hd * R_QK_DIM + half:(hd + 1) * R_QK_DIM]
            dec = d_ref[:, hd * LANES:(hd + 1) * LANES]
            put(off + hd * R_QK_DIM, (x1 * cos - x2 * sin) * dec)
            put(off + hd * R_QK_DIM + half, (x1 * sin + x2 * cos) * dec)

    for j in range(2):
        put(OFF_VR + j * D_MODEL, proj(OFF_VR + j * D_MODEL))
    for j in range(2):
        g = proj(OFF_GR + j * D_MODEL)
        put(OFF_GR + j * D_MODEL, g * _sigmoid(g))
    put(OFF_GA, _sigmoid(proj(OFF_GA)))
    put(OFF_GB, _sigmoid(proj(OFF_GB)))


def _inproj_call(x, mod, g1, cos, sin, dq, dk, lng, lnb, w_in, *, layer, per_row, z_dtype):
    rows = x.shape[0]
    tile = dq.shape[0]
    tiles_per_seq = None if per_row else (rows // mod.shape[0]) // tile
    if per_row:
        sh_spec = pl.BlockSpec((None, tile, D_MODEL), lambda i: (layer, i, 0))
        sc_spec = pl.BlockSpec((None, tile, D_MODEL), lambda i: (layer, i, 1))
        rope_spec = pl.BlockSpec((tile, LANES), lambda i: (0, 0))
    else:
        sh_spec = pl.BlockSpec((None, 1, D_MODEL), lambda i: (i // tiles_per_seq, 0, 0))
        sc_spec = pl.BlockSpec((None, 1, D_MODEL), lambda i: (i // tiles_per_seq, 0, 1))
        rope_spec = pl.BlockSpec((tile, LANES), lambda i: (i % tiles_per_seq, 0))
    const = lambda i: (0, 0)
    return pl.pallas_call(
        _inproj_kernel,
        grid=(rows // tile,),
        in_specs=[
            pl.BlockSpec((tile, D_MODEL), lambda i: (i, 0)),
            sh_spec, sc_spec,
            pl.BlockSpec((1, D_MODEL), const),
            rope_spec, rope_spec,
            pl.BlockSpec((tile, R_HEADS * LANES), const),
            pl.BlockSpec((tile, R_HEADS * LANES), const),
            pl.BlockSpec((1, D_MODEL), const),
            pl.BlockSpec((1, D_MODEL), const),
            pl.BlockSpec((None, D_MODEL, IN_COLS), lambda i: (layer, 0, 0), pipeline_mode=pl.Buffered(1)),
        ],
        out_specs=pl.BlockSpec((tile, IN_COLS), lambda i: (i, 0)),
        out_shape=jax.ShapeDtypeStruct((rows, IN_COLS), z_dtype),
        compiler_params=pltpu.CompilerParams(vmem_limit_bytes=VMEM_LIMIT),
        name="in_proj",
    )(x, mod, mod, g1, cos, sin, dq, dk, lng, lnb, w_in)


def _ret_sample_kernel(cd_ref, z_q_ref, z_k_ref, z_v_ref, r0_ref, *rest, t_s):
    cross_ref, r_ref = rest[-2:]
    q = z_q_ref[...].astype(BF16)
    k = z_k_ref[...]
    v = z_v_ref[...].astype(BF16)
    row_k = lax.broadcasted_iota(jnp.int32, (RET_ROWS, R_QK_DIM), 0)
    row_v = lax.broadcasted_iota(jnp.int32, (RET_ROWS, R_V_DIM), 0)
    for hd in range(R_HEADS):
        qh = q[:, hd * R_QK_DIM:(hd + 1) * R_QK_DIM]
        kh = k[:, hd * R_QK_DIM:(hd + 1) * R_QK_DIM]
        vh = v[:, hd * R_V_DIM:(hd + 1) * R_V_DIM]
        cross = jnp.zeros((RET_ROWS, R_V_DIM), F32)
        for s in range(RET_ROWS // t_s):
            r = r0_ref[s, hd]
            lo, hi = s * t_s, (s + 1) * t_s
            cross = jnp.where((row_v >= lo) & (row_v < hi), _dot(qh, r.astype(BF16)), cross)
            k_seq = jnp.where((row_k >= lo) & (row_k < hi), kh, 0.0).astype(BF16)
            upd = lax.dot_general(k_seq, vh, (((0,), (0,)), ((), ())), preferred_element_type=F32)
            r_ref[s, hd] = cd_ref[hd] * (r + upd)
        cross_ref[:, hd * R_V_DIM:(hd + 1) * R_V_DIM] = cross


def _ret_sample_call(cd, z, state, layer, t_s, stacked=None):
    rows = z.shape[0]
    seqs = RET_ROWS // t_s
    qk_w = R_HEADS * R_QK_DIM
    v_w = R_HEADS * R_V_DIM
    state_spec = pl.BlockSpec((None, seqs, R_HEADS, R_QK_DIM, R_V_DIM), lambda b: (layer, b, 0, 0, 0))
    in_specs = [
        pl.BlockSpec(memory_space=pltpu.SMEM),
        pl.BlockSpec((RET_ROWS, qk_w), lambda b: (b, OFF_Q // qk_w)),
        pl.BlockSpec((RET_ROWS, qk_w), lambda b: (b, OFF_K // qk_w)),
        pl.BlockSpec((RET_ROWS, v_w), lambda b: (b, OFF_VR // v_w)),
        state_spec,
    ]
    args = [cd, z, z, z, state]
    aliases = {}
    if stacked is not None:
        aliases = {len(args): 1}
        in_specs.append(pl.BlockSpec(memory_space=pl.ANY))
        args.append(stacked)
    return pl.pallas_call(
        functools.partial(_ret_sample_kernel, t_s=t_s),
        grid=(rows // RET_ROWS,),
        in_specs=in_specs,
        out_specs=[
            pl.BlockSpec((RET_ROWS, v_w), lambda b: (b, 0)),
            state_spec,
        ],
        out_shape=[
            jax.ShapeDtypeStruct((rows, v_w), F32),
            jax.ShapeDtypeStruct(state.shape, F32),
        ],
        input_output_aliases=aliases,
        compiler_params=pltpu.CompilerParams(vmem_limit_bytes=VMEM_LIMIT),
        name="ret_sample",
    )(*args)


def _mix_kernel(cd_ref, z_ref, x_ref, gt_ref, ws_ref, bs_ref, mask_ref, wa_ref, wb_ref, wo_ref, *rest,
                carried_state):
    nsb, nrows = z_ref.shape[:2]
    per_sb = nrows // CHUNK
    nseq = nsb * per_sb
    if carried_state:
        cross_ref, xo_ref = rest
    else:
        xo_ref, r_ref = rest[-2:]

        @pl.when(pl.program_id(1) == 0)
        def _():
            r_ref[...] = jnp.zeros_like(r_ref)

    mask = mask_ref[...]
    gdim = D_MODEL // A_GROUPS
    a_rows, b_rows = [], []
    for sq in range(nseq):
        sb, chunk = divmod(sq, per_sb)
        rs = slice(chunk * CHUNK, (chunk + 1) * CHUNK)

        def sec(off, width, sb=sb, rs=rs):
            return z_ref[sb, rs, off:off + width]

        v = sec(OFF_V, D_MODEL).astype(BF16)
        mixed = jnp.concatenate(
            [_dot(ws_ref[g], v[:, g * gdim:(g + 1) * gdim]) for g in range(A_GROUPS)], axis=1)
        a_rows.append((sec(OFF_U, D_MODEL).astype(F32) * (mixed + bs_ref[...])).astype(BF16))

        b_parts = []
        for hd in range(R_HEADS):
            qh = sec(OFF_Q + hd * R_QK_DIM, R_QK_DIM).astype(BF16)
            kh = sec(OFF_K + hd * R_QK_DIM, R_QK_DIM).astype(BF16)
            vh = sec(OFF_VR + hd * R_V_DIM, R_V_DIM).astype(BF16)
            s = lax.dot_general(qh, kh, (((1,), (1,)), ((), ())), preferred_element_type=F32) * mask
            y = _dot(s.astype(BF16), vh)
            if carried_state:
                y = y + cross_ref[sb, rs, hd * R_V_DIM:(hd + 1) * R_V_DIM]
            else:
                r = r_ref[sq, hd]
                y = y + _dot(qh, r.astype(BF16))
                upd = lax.dot_general(kh, vh, (((0,), (0,)), ((), ())), preferred_element_type=F32)
                r_ref[sq, hd] = cd_ref[hd] * (r + upd)
            mu = jnp.mean(y, axis=-1, keepdims=True)
            yc = y - mu
            var = jnp.mean(yc * yc, axis=-1, keepdims=True)
            yn = yc * lax.rsqrt(var + EPS)
            b_parts.append((sec(OFF_GR + hd * R_V_DIM, R_V_DIM).astype(F32) * yn).astype(BF16))
        b_rows.append(jnp.concatenate(b_parts, axis=1))

    pa = _dot(jnp.concatenate(a_rows, axis=0), wa_ref[...])
    pb = _dot(jnp.concatenate(b_rows, axis=0), wb_ref[...])
    ga = z_ref[:, :, OFF_GA:OFF_GA + D_MODEL].reshape(nseq * CHUNK, D_MODEL).astype(F32)
    gb = z_ref[:, :, OFF_GB:OFF_GB + D_MODEL].reshape(nseq * CHUNK, D_MODEL).astype(F32)
    mix = _dot((ga * pa + gb * pb).astype(BF16), wo_ref[...])
    xo_ref[...] = x_ref[...] + gt_ref[...] * mix.reshape(nsb, nrows, D_MODEL)


def _mix_call(cd, z, x, mod, ws, bs, mask, wa, wb, wo, *, nseq, nchunk, cross=None, layer=0, stacked=None):
    rows = x.shape[0]
    carried = cross is not None
    aliases = {}
    if carried:
        vseq, vlen, sb, rb = 1, rows, 1, MIX_SEQS * CHUNK
    else:
        vseq, vlen, sb, rb = nseq, nchunk * CHUNK, MIX_SEQS, CHUNK
    assert vseq % sb == 0 and vlen % rb == 0
    view = lambda a: a.reshape(vseq, vlen, a.shape[-1])
    row_map = lambda b, c: (b, c, 0)
    const2 = lambda b, c: (0, 0)
    wmap = lambda b, c: (layer, 0, 0)
    if carried:
        mod = mod.reshape(mod.shape[0], vseq, vlen, MOD_COLS)
        gt_spec = pl.BlockSpec((None, sb, rb, D_MODEL), lambda b, c: (layer, b, c, 2))
    else:
        gt_spec = pl.BlockSpec((sb, 1, D_MODEL), lambda b, c: (b, 0, 2))
    in_specs = [
        pl.BlockSpec(memory_space=pltpu.SMEM),
        pl.BlockSpec((sb, rb, IN_COLS), row_map),
        pl.BlockSpec((sb, rb, D_MODEL), row_map),
        gt_spec,
        pl.BlockSpec((A_GROUPS, CHUNK, CHUNK), lambda b, c: (0, 0, 0)),
        pl.BlockSpec((CHUNK, D_MODEL), const2),
        pl.BlockSpec((CHUNK, CHUNK), const2),
        pl.BlockSpec((None, D_MODEL, D_MODEL), wmap),
        pl.BlockSpec((None, R_HEADS * R_V_DIM, D_MODEL), wmap),
        pl.BlockSpec((None, D_MODEL, D_MODEL), wmap),
    ]
    args = [cd, view(z), view(x), mod, ws, bs, mask, wa, wb, wo]
    x_out = jax.ShapeDtypeStruct((vseq, vlen, D_MODEL), F32)
    x_spec = pl.BlockSpec((sb, rb, D_MODEL), row_map)
    if carried:
        in_specs.append(pl.BlockSpec((sb, rb, R_HEADS * R_V_DIM), row_map))
        args.append(view(cross))
        out_specs, out_shape = x_spec, x_out
    else:
        out_specs = [x_spec, pl.BlockSpec((None, sb, R_HEADS, R_QK_DIM, R_V_DIM),
                                          lambda b, c: (layer, b, 0, 0, 0))]
        out_shape = [x_out, jax.ShapeDtypeStruct((DEPTH, nseq, R_HEADS, R_QK_DIM, R_V_DIM), F32)]
        if stacked is not None:
            aliases = {len(args): 1}
            in_specs.append(pl.BlockSpec(memory_space=pl.ANY))
            args.append(stacked)
    out = pl.pallas_call(
        functools.partial(_mix_kernel, carried_state=carried),
        grid=(vseq // sb, vlen // rb),
        in_specs=in_specs,
        out_specs=out_specs,
        out_shape=out_shape,
        input_output_aliases=aliases,
        compiler_params=pltpu.CompilerParams(
            dimension_semantics=("arbitrary", "arbitrary"), vmem_limit_bytes=VMEM_LIMIT),
        name="token_mix",
    )(*args)
    if carried:
        return out.reshape(rows, D_MODEL)
    return out[0].reshape(rows, D_MODEL), out[1]


def _sorted_rows(block):
    need = 2 * block + N_EXPERTS * ROW_ALIGN + EXP_TILE
    return -(-need // GATHER_CHUNK) * GATHER_CHUNK


def _route_kernel(x_ref, sh_ref, sc_ref, g_ref, wr2_ref, br_ref, tri_ref, low_ref,
                  h_ref, route_ref, off_ref, nt_ref):
    blk = x_ref.shape[0]
    h = (_rms(x_ref[...]) * g_ref[...]) * (1.0 + sc_ref[...]) + sh_ref[...]
    h_hi, h_lo = _split_bf16(h)
    h_ref[...] = h_hi

    nt_dims = (((1,), (1,)), ((), ()))
    nr = ROUTER_ROWS
    r_hi = lax.dot_general(wr2_ref[...], h_hi, nt_dims, preferred_element_type=F32)
    r_lo = lax.dot_general(wr2_ref[0:nr, :], h_lo, nt_dims, preferred_element_type=F32)
    logits = r_hi[0:nr] + r_hi[nr:2 * nr] + r_lo + br_ref[:, :1]

    gl = logits[0:GROUP_ROWS]
    grow = lax.broadcasted_iota(jnp.int32, (GROUP_ROWS, blk), 0).astype(F32)
    gmax = jnp.max(gl, axis=0, keepdims=True)
    g_w = 1.0 / jnp.sum(jnp.exp(gl - gmax), axis=0, keepdims=True)
    g_idx = jnp.min(jnp.where(gl == gmax, grow, float(GROUP_ROWS)), axis=0, keepdims=True)

    el = logits[GROUP_ROWS:GROUP_ROWS + N_EXPERTS]
    erow_i = lax.broadcasted_iota(jnp.int32, (N_EXPERTS, blk), 0)
    erow = erow_i.astype(F32)
    el = jnp.where((erow_i >> 3).astype(F32) == g_idx, el, NEG)
    m1 = jnp.max(el, axis=0, keepdims=True)
    i1 = jnp.min(jnp.where(el == m1, erow, float(N_EXPERTS)), axis=0, keepdims=True)
    el2 = jnp.where(erow == i1, NEG, el)
    m2 = jnp.max(el2, axis=0, keepdims=True)
    i2 = jnp.min(jnp.where(el2 == m2, erow, float(N_EXPERTS)), axis=0, keepdims=True)
    t = jnp.exp(m2 - m1)
    w1 = g_w / (1.0 + t)
    w2 = g_w * t / (1.0 + t)

    s1 = erow == i1
    s2 = erow == i2
    onehot = jnp.where(s1 | s2, 1.0, 0.0)
    rank = _dot(onehot.astype(BF16), tri_ref[...])
    cnt = jnp.sum(onehot, axis=1, keepdims=True)
    units = jnp.floor((cnt + (ROW_ALIGN - 1.0)) * (1.0 / ROW_ALIGN))
    units = jnp.broadcast_to(units, (N_EXPERTS, LANES))
    off = float(ROW_ALIGN) * _dot(low_ref[...], units.astype(BF16))
    base = off[:, :1] + rank
    pos1 = jnp.sum(jnp.where(s1, base, 0.0), axis=0, keepdims=True)
    pos2 = jnp.sum(jnp.where(s2, base, 0.0), axis=0, keepdims=True)

    r8 = lax.broadcasted_iota(jnp.int32, (8, blk), 0)
    route_ref[...] = jnp.where(r8 == 0, pos1, jnp.where(r8 == 1, pos2, jnp.where(r8 == 2, w1,
                               jnp.where(r8 == 3, w2, 0.0))))
    off_ref[...] = off.astype(jnp.int32)
    tiles = jnp.floor((cnt + (EXP_TILE - 1.0)) * (1.0 / EXP_TILE))
    nt_ref[...] = jnp.broadcast_to(tiles, (N_EXPERTS, LANES)).astype(jnp.int32)


def _experts_kernel(off_sm, nt_sm, x_ref, h_ref, route_ref, gt_ref, wg_ref, wu_ref, wd_ref, fg_ref, o_ref,
                    p_scr, xs_scr, ys_scr, cw_scr, *, final_norm):
    i = pl.program_id(0)
    step = pl.program_id(1)
    nrows, blk = p_scr.shape

    @pl.when(step == 0)
    def _():
        pos1 = route_ref[0:1, :]
        pos2 = route_ref[1:2, :]
        w1 = route_ref[2:3, :]
        w2 = route_ref[3:4, :]
        row_iota = lax.broadcasted_iota(jnp.int32, (GATHER_CHUNK, blk), 0)

        def gather(c, carry):
            r0 = pl.multiple_of(c * GATHER_CHUNK, GATHER_CHUNK)
            prow = (row_iota + r0).astype(F32)
            m1 = prow == pos1
            m2 = prow == pos2
            perm = jnp.where(m1 | m2, 1.0, 0.0).astype(BF16)
            p_scr[pl.ds(r0, GATHER_CHUNK), :] = perm
            xs_scr[pl.ds(r0, GATHER_CHUNK), :] = _dot(perm, h_ref[...]).astype(BF16)
            cw = jnp.sum(jnp.where(m1, w1, 0.0) + jnp.where(m2, w2, 0.0), axis=1, keepdims=True)
            cw_scr[pl.ds(r0, GATHER_CHUNK), :] = jnp.broadcast_to(cw, (GATHER_CHUNK, LANES))
            ys_scr[pl.ds(r0, GATHER_CHUNK), :] = jnp.zeros((GATHER_CHUNK, D_MODEL), BF16)
            return carry

        lax.fori_loop(0, nrows // GATHER_CHUNK, gather, 0)

    def tile(g, s):
        s = pl.multiple_of(s, ROW_ALIGN)
        rows = xs_scr[pl.ds(s, EXP_TILE), :]
        gate = _dot(rows, wg_ref[g])
        up = _dot(rows, wu_ref[g])
        cw = cw_scr[pl.ds(s, EXP_TILE), :][:, :1]
        hid = (gate * _sigmoid(gate) * up * cw).astype(BF16)
        return s, _dot(hid, wd_ref[g]).astype(BF16)

    def put(s, y):
        ys_scr[pl.ds(s, EXP_TILE), :] = y

    starts = [off_sm[i, step * EXPERTS_PER_STEP + g] for g in range(EXPERTS_PER_STEP)]
    first = [tile(g, starts[g]) for g in range(EXPERTS_PER_STEP)]
    for g in range(EXPERTS_PER_STEP):
        put(*first[g])

        def more(t, carry, g=g):
            put(*tile(g, starts[g] + t * EXP_TILE))
            return carry

        lax.fori_loop(1, nt_sm[i, step * EXPERTS_PER_STEP + g], more, 0)

    @pl.when(step == N_EXPERTS // EXPERTS_PER_STEP - 1)
    def _():
        tn_dims = (((0,), (0,)), ((), ()))
        for r0 in range(0, blk, GATHER_CHUNK):
            rs = slice(r0, r0 + GATHER_CHUNK)
            moe = lax.dot_general(p_scr[:, rs], ys_scr[...], tn_dims, preferred_element_type=F32)
            gt = gt_ref[...] if gt_ref.shape[0] == 1 else gt_ref[rs, :]
            y = x_ref[rs, :] + gt * moe
            if final_norm:
                y = _rms(y) * fg_ref[...]
            o_ref[rs, :] = y


def _moe_call(x, mod, g2, wr2, br, w_exp, fg, *, layer, block, per_row, final_norm):
    rows = x.shape[0]
    nblk = rows // block
    nsorted = _sorted_rows(block)
    if per_row:
        mspec = lambda s: pl.BlockSpec((None, block, D_MODEL), lambda i, *_: (layer, i, s))
    else:
        blocks_per_seq = (rows // mod.shape[0]) // block
        mspec = lambda s: pl.BlockSpec((None, 1, D_MODEL), lambda i, *_: (i // blocks_per_seq, 0, s))
    const = lambda i: (0, 0)
    ridx = jnp.arange(block)
    tri = (ridx[:, None] < ridx[None, :]).astype(BF16)
    eidx = jnp.arange(N_EXPERTS)
    low = (eidx[None, :] < eidx[:, None]).astype(BF16)
    h, route, off, nt = pl.pallas_call(
        _route_kernel,
        grid=(nblk,),
        in_specs=[
            pl.BlockSpec((block, D_MODEL), lambda i: (i, 0)),
            mspec(3), mspec(4),
            pl.BlockSpec((1, D_MODEL), const),
            pl.BlockSpec((2 * ROUTER_ROWS, D_MODEL), const),
            pl.BlockSpec((ROUTER_ROWS, LANES), const),
            pl.BlockSpec((block, block), const),
            pl.BlockSpec((N_EXPERTS, N_EXPERTS), const),
        ],
        out_specs=[
            pl.BlockSpec((block, D_MODEL), lambda i: (i, 0)),
            pl.BlockSpec((None, 8, block), lambda i: (i, 0, 0)),
            pl.BlockSpec((None, N_EXPERTS, LANES), lambda i: (i, 0, 0)),
            pl.BlockSpec((None, N_EXPERTS, LANES), lambda i: (i, 0, 0)),
        ],
        out_shape=[
            jax.ShapeDtypeStruct((rows, D_MODEL), BF16),
            jax.ShapeDtypeStruct((nblk, 8, block), F32),
            jax.ShapeDtypeStruct((nblk, N_EXPERTS, LANES), jnp.int32),
            jax.ShapeDtypeStruct((nblk, N_EXPERTS, LANES), jnp.int32),
        ],
        compiler_params=pltpu.CompilerParams(vmem_limit_bytes=VMEM_LIMIT),
        name="moe_route",
    )(x, mod, mod, g2, wr2, br, tri, low)

    const2 = lambda i, e, *_: (0, 0)
    wmap = lambda i, s, *_: (layer, s, 0, 0)
    return pl.pallas_call(
        functools.partial(_experts_kernel, final_norm=final_norm),
        grid_spec=pltpu.PrefetchScalarGridSpec(
            num_scalar_prefetch=2,
            grid=(nblk, N_EXPERTS // EXPERTS_PER_STEP),
            in_specs=[
                pl.BlockSpec((block, D_MODEL), lambda i, e, *_: (i, 0)),
                pl.BlockSpec((block, D_MODEL), lambda i, e, *_: (i, 0), pipeline_mode=pl.Buffered(1)),
                pl.BlockSpec((None, 8, block), lambda i, e, *_: (i, 0, 0)),
                mspec(5),
                pl.BlockSpec((None, EXPERTS_PER_STEP, D_MODEL, EXPERT_DIM), wmap),
                pl.BlockSpec((None, EXPERTS_PER_STEP, D_MODEL, EXPERT_DIM), wmap),
                pl.BlockSpec((None, EXPERTS_PER_STEP, EXPERT_DIM, D_MODEL), wmap),
                pl.BlockSpec((1, D_MODEL), const2),
            ],
            out_specs=pl.BlockSpec((block, D_MODEL), lambda i, e, *_: (i, 0)),
            scratch_shapes=[
                pltpu.VMEM((nsorted, block), BF16),
                pltpu.VMEM((nsorted, D_MODEL), BF16),
                pltpu.VMEM((nsorted, D_MODEL), BF16),
                pltpu.VMEM((nsorted, LANES), F32),
            ],
        ),
        out_shape=jax.ShapeDtypeStruct((rows, D_MODEL), F32),
        compiler_params=pltpu.CompilerParams(
            dimension_semantics=("arbitrary", "arbitrary"), vmem_limit_bytes=VMEM_LIMIT),
        name="moe_experts",
    )(off[:, :, 0], nt[:, :, 0], x, h, route, mod, *w_exp, fg)


def _rope_tables(pos):
    half = R_QK_DIM // 2
    inv_freq = jnp.power(ROPE_BASE, -jnp.arange(half, dtype=F32) / half)
    ang = pos.astype(F32)[:, None] * inv_freq[None, :]
    return jnp.cos(ang), jnp.sin(ang)


def _decay_tables(chunk, rows):
    log_g = jnp.log1p(-jnp.power(2.0, -5.0 - jnp.arange(R_HEADS, dtype=F32)))
    t1 = (jnp.arange(rows) % chunk).astype(F32) + 1.0
    dq = jnp.exp(log_g[None, :] * t1[:, None])
    dk = jnp.exp(-log_g[None, :] * t1[:, None]) * (R_QK_DIM ** -0.5)
    cd = jnp.exp(log_g * chunk)
    rep = lambda a: jnp.repeat(a, LANES, axis=1)
    return rep(dq), rep(dk), cd


def kernel(x_prompt, x_sample, state_ret, c_prompt, c_sample, w_mod, b_mod, norm1_g, w_in, ln_v_g, ln_v_b,
           w_s, b_s, w_a_out, w_b_out, w_o, norm2_g, w_router_group, b_router_group, w_router_expert,
           b_router_expert, w_gate, w_up, w_down, final_g):
    n_p, t_p, _ = x_prompt.shape
    n_s, t_s, _ = x_sample.shape
    assert t_p % CHUNK == 0 and (n_s * t_s) % CHUNK == 0 and CHUNK % t_s == 0 and RET_ROWS % t_s == 0
    chunks_p = t_p // CHUNK
    chunks_s = (n_s * t_s) // CHUNK

    mod = _mod_call(jnp.concatenate([c_prompt, c_sample], axis=0), w_mod, b_mod)

    cos_p, sin_p = _rope_tables(jnp.arange(t_p, dtype=jnp.int32))
    cos_s, sin_s = _rope_tables(PAST_LEN + jnp.arange(IN_TILE_F32, dtype=jnp.int32) % t_s)
    dq_p, dk_p, cd_p = _decay_tables(CHUNK, IN_TILE_BF16)
    dq_s, dk_s, cd_s = _decay_tables(t_s, IN_TILE_F32)

    idx = jnp.arange(CHUNK)
    causal = idx[:, None] >= idx[None, :]
    same_seq = (idx[:, None] // t_s) == (idx[None, :] // t_s)
    mask_p = causal.astype(F32)
    mask_s = (causal & same_seq).astype(F32)

    xp = x_prompt.reshape(n_p * t_p, D_MODEL)
    xs = x_sample.reshape(n_s * t_s, D_MODEL)
    fg = final_g.reshape(1, D_MODEL)
    w_in_b = w_in.astype(BF16)
    wa, wb, wo = w_a_out.astype(BF16), w_b_out.astype(BF16), w_o.astype(BF16)
    w_exp = (w_gate.astype(BF16), w_up.astype(BF16), w_down.astype(BF16))
    mod_s = jnp.broadcast_to(mod[:, n_p:, None, :], (DEPTH, n_s, t_s, MOD_COLS))
    mod_s = mod_s.reshape(DEPTH, n_s * t_s, MOD_COLS)
    r_prompt, r_sample, v_sample = None, None, []
    for l in range(DEPTH):
        mod_p = mod[l, :n_p].reshape(n_p, 1, MOD_COLS)
        g1 = norm1_g[l].reshape(1, D_MODEL)
        g2 = norm2_g[l].reshape(1, D_MODEL)
        lng = ln_v_g[l].reshape(1, D_MODEL)
        lnb = ln_v_b[l].reshape(1, D_MODEL)

        ws_p = jnp.where(causal[None], w_s[l], 0.0).astype(BF16)
        bs_p = jnp.repeat(b_s[l].T, D_MODEL // A_GROUPS, axis=1)
        blk = jnp.where(causal[:t_s, :t_s][None], w_s[l][:, :t_s, :t_s], 0.0)
        ws_s = jnp.where(same_seq[None], jnp.tile(blk, (1, CHUNK // t_s, CHUNK // t_s)), 0.0).astype(BF16)
        bs_s = jnp.tile(bs_p[:t_s], (CHUNK // t_s, 1))

        zp = _inproj_call(xp, mod_p, g1, cos_p, sin_p, dq_p, dk_p, lng, lnb, w_in_b,
                          layer=l, per_row=False, z_dtype=BF16)
        zs = _inproj_call(xs, mod_s, g1, cos_s, sin_s, dq_s, dk_s, lng, lnb, w_in_b,
                          layer=l, per_row=True, z_dtype=F32)

        xp, r_prompt = _mix_call(cd_p, zp, xp, mod_p, ws_p, bs_p, mask_p, wa, wb, wo,
                                 nseq=n_p, nchunk=chunks_p, layer=l, stacked=r_prompt)
        cross, r_sample = _ret_sample_call(cd_s, zs, state_ret, l, t_s, r_sample)
        xs = _mix_call(cd_s, zs, xs, mod_s, ws_s, bs_s, mask_s, wa, wb, wo,
                       nseq=chunks_s, nchunk=1, layer=l, cross=cross)

        tail = ROUTER_ROWS - GROUP_ROWS - N_EXPERTS
        wr = jnp.concatenate([jnp.pad(w_router_group[l].T, ((0, GROUP_ROWS - N_GROUPS), (0, 0))),
                              jnp.pad(w_router_expert[l].T, ((0, tail), (0, 0)))], axis=0)
        br = jnp.concatenate([jnp.pad(b_router_group[l], (0, GROUP_ROWS - N_GROUPS), constant_values=NEG),
                              jnp.pad(b_router_expert[l], (0, tail))])
        wr2 = jnp.concatenate(_split_bf16(wr), axis=0)
        br = jnp.broadcast_to(br[:, None], (ROUTER_ROWS, LANES))
        last = l == DEPTH - 1
        xp = _moe_call(xp, mod_p, g2, wr2, br, w_exp, fg,
                       layer=l, block=MOE_BLOCK, per_row=False, final_norm=last)
        xs = _moe_call(xs, mod_s, g2, wr2, br, w_exp, fg,
                       layer=l, block=n_s * t_s, per_row=True, final_norm=last)

        v_sample.append(zs[:, OFF_V:OFF_V + D_MODEL].reshape(n_s, t_s, D_MODEL))

    return (xp.reshape(n_p, t_p, D_MODEL), xs.reshape(n_s, t_s, D_MODEL),
            r_prompt, r_sample, jnp.stack(v_sample))
```

```python
import functools

import jax
import jax.numpy as jnp
from jax import lax
from jax.experimental import pallas as pl
from jax.experimental.pallas import tpu as pltpu

D_MODEL = 1024
DEPTH = 2
PAST_LEN = 16384
A_GROUPS = 8
CHUNK = 128
R_HEADS = 4
R_QK_DIM = 256
R_V_DIM = 512
ROPE_BASE = 10000.0
N_GROUPS = 4
EXPERTS_PER_GROUP = 8
N_EXPERTS = 32
EXPERT_DIM = 256
EPS = 1e-6
IN_COLS = 10 * D_MODEL
MOD_COLS = 6 * D_MODEL

OFF_U, OFF_V, OFF_Q, OFF_K, OFF_VR, OFF_GR, OFF_GA, OFF_GB = 0, 1024, 2048, 3072, 4096, 6144, 8192, 9216

LANES = 128
IN_TILE_BF16 = 256
IN_TILE_F32 = 256
MIX_SEQS = 2
MIX_CHUNKS_F32 = 2
MOE_BLOCK = 1024
EXP_TILE = 128
ROW_ALIGN = 16
GATHER_CHUNK = 256
GROUP_ROWS = 8
ROUTER_ROWS = 48
EXPERTS_PER_STEP = 4
MOD_TILE = 1536
VMEM_LIMIT = 56 * 1024 * 1024
NEG = float("-inf")

BF16 = jnp.bfloat16
F32 = jnp.float32


def _dot(a, b):
    return jnp.dot(a, b, preferred_element_type=F32)


def _sigmoid(x):
    return 1.0 / (1.0 + jnp.exp(-x))


def _gelu_tanh(x):
    return x * (0.5 * (1.0 + jnp.tanh(0.7978845608028654 * (x + 0.044715 * (x * x * x)))))


def _rms(x):
    return x * lax.rsqrt(jnp.mean(x * x, axis=-1, keepdims=True) + EPS)


def _split_bf16(w):
    hi = w.astype(BF16)
    return hi, (w - hi.astype(F32)).astype(BF16)


def _dot_split(a, b):
    a_hi, a_lo = _split_bf16(a)
    b_hi, b_lo = _split_bf16(b)
    return _dot(a_hi, b_hi) + (_dot(a_lo, b_hi) + _dot(a_hi, b_lo))


def _mod_kernel(c_ref, w_ref, b_ref, o_ref):
    c = c_ref[...]
    o_ref[...] = _dot_split(c * _sigmoid(c), w_ref[...]) + b_ref[...]


def _mod_call(c_all, w_mod, b_mod):
    n = c_all.shape[0]
    return pl.pallas_call(
        _mod_kernel,
        grid=(DEPTH, MOD_COLS // MOD_TILE),
        in_specs=[
            pl.BlockSpec((n, D_MODEL), lambda l, j: (0, 0)),
            pl.BlockSpec((None, D_MODEL, MOD_TILE), lambda l, j: (l, 0, j)),
            pl.BlockSpec((None, 1, MOD_TILE), lambda l, j: (l, 0, j)),
        ],
        out_specs=pl.BlockSpec((None, n, MOD_TILE), lambda l, j: (l, 0, j)),
        out_shape=jax.ShapeDtypeStruct((DEPTH, n, MOD_COLS), F32),
        compiler_params=pltpu.CompilerParams(vmem_limit_bytes=VMEM_LIMIT),
        name="adaln_mod",
    )(c_all, w_mod, b_mod.reshape(DEPTH, 1, MOD_COLS))


def _inproj_kernel(x_ref, sh_ref, sc_ref, g_ref, cos_ref, sin_ref, dq_ref, dk_ref, lng_ref, lnb_ref,
                   w_ref, z_ref):
    h = (_rms(x_ref[...]) * g_ref[...]) * (1.0 + sc_ref[...]) + sh_ref[...]
    h = h.astype(BF16)
    cos = cos_ref[...]
    sin = sin_ref[...]

    def proj(off, width=D_MODEL):
        return _dot(h, w_ref[:, off:off + width])

    def put(off, val):
        z_ref[:, off:off + val.shape[1]] = val.astype(z_ref.dtype)

    def plain(off):
        put(off, proj(off))

    def swish(off):
        g = proj(off)
        put(off, g * _sigmoid(g))

    def rotary(off, d_ref):
        half = R_QK_DIM // 2
        acc = proj(off)
        for hd in range(R_HEADS):
            x1 = acc[:, hd * R_QK_DIM:hd * R_QK_DIM + half]
            x2 = acc[:, hd * R_QK_DIM + half:(hd + 1) * R_QK_DIM]
            dec = d_ref[:, hd * LANES:(hd + 1) * LANES]
            put(off + hd * R_QK_DIM, (x1 * cos - x2 * sin) * dec)
            put(off + hd * R_QK_DIM + half, (x1 * sin + x2 * cos) * dec)

    put(OFF_U, _gelu_tanh(proj(OFF_U)))
    plain(OFF_VR)
    v = _gelu_tanh(proj(OFF_V))
    mu = jnp.mean(v, axis=-1, keepdims=True)
    vc = v - mu
    var = jnp.mean(vc * vc, axis=-1, keepdims=True)
    put(OFF_V, vc * lax.rsqrt(var + EPS) * lng_ref[...] + lnb_ref[...])
    plain(OFF_VR + D_MODEL)
    rotary(OFF_Q, dq_ref)
    put(OFF_GA, _sigmoid(proj(OFF_GA)))
    rotary(OFF_K, dk_ref)
    put(OFF_GB, _sigmoid(proj(OFF_GB)))
    swish(OFF_GR)
    swish(OFF_GR + D_MODEL)


def _inproj_call(x, mod, g1, cos, sin, dq, dk, lng, lnb, w_in, *, layer, per_row, z_dtype):
    rows = x.shape[0]
    tile = dq.shape[0]
    tiles_per_seq = None if per_row else (rows // mod.shape[0]) // tile
    if per_row:
        sh_spec = pl.BlockSpec((None, tile, D_MODEL), lambda i: (layer, i, 0))
        sc_spec = pl.BlockSpec((None, tile, D_MODEL), lambda i: (layer, i, 1))
        rope_spec = pl.BlockSpec((tile, LANES), lambda i: (0, 0))
    else:
        sh_spec = pl.BlockSpec((None, 1, D_MODEL), lambda i: (i // tiles_per_seq, 0, 0))
        sc_spec = pl.BlockSpec((None, 1, D_MODEL), lambda i: (i // tiles_per_seq, 0, 1))
        rope_spec = pl.BlockSpec((tile, LANES), lambda i: (i % tiles_per_seq, 0))
    const = lambda i: (0, 0)
    return pl.pallas_call(
        _inproj_kernel,
        grid=(rows // tile,),
        in_specs=[
            pl.BlockSpec((tile, D_MODEL), lambda i: (i, 0)),
            sh_spec, sc_spec,
            pl.BlockSpec((1, D_MODEL), const),
            rope_spec, rope_spec,
            pl.BlockSpec((tile, R_HEADS * LANES), const),
            pl.BlockSpec((tile, R_HEADS * LANES), const),
            pl.BlockSpec((1, D_MODEL), const),
            pl.BlockSpec((1, D_MODEL), const),
            pl.BlockSpec((None, D_MODEL, IN_COLS), lambda i: (layer, 0, 0), pipeline_mode=pl.Buffered(1)),
        ],
        out_specs=pl.BlockSpec((tile, IN_COLS), lambda i: (i, 0)),
        out_shape=jax.ShapeDtypeStruct((rows, IN_COLS), z_dtype),
        compiler_params=pltpu.CompilerParams(vmem_limit_bytes=VMEM_LIMIT),
        name="in_proj",
    )(x, mod, mod, g1, cos, sin, dq, dk, lng, lnb, w_in)


def _carried_state_step(cd_ref, z_q_ref, z_k_ref, z_v_ref, r0_ref, cross_ref, r_ref, t_s):
    nrows = z_q_ref.shape[0]
    q = z_q_ref[...].astype(BF16)
    k = z_k_ref[...]
    v = z_v_ref[...].astype(BF16)
    row_k = lax.broadcasted_iota(jnp.int32, (nrows, R_QK_DIM), 0)
    row_v = lax.broadcasted_iota(jnp.int32, (nrows, R_V_DIM), 0)
    for hd in range(R_HEADS):
        qh = q[:, hd * R_QK_DIM:(hd + 1) * R_QK_DIM]
        kh = k[:, hd * R_QK_DIM:(hd + 1) * R_QK_DIM]
        vh = v[:, hd * R_V_DIM:(hd + 1) * R_V_DIM]
        cross = jnp.zeros((nrows, R_V_DIM), F32)
        for s in range(nrows // t_s):
            r = r0_ref[s, hd]
            lo, hi = s * t_s, (s + 1) * t_s
            cross = jnp.where((row_v >= lo) & (row_v < hi), _dot(qh, r.astype(BF16)), cross)
            k_seq = jnp.where((row_k >= lo) & (row_k < hi), kh, 0.0).astype(BF16)
            upd = lax.dot_general(k_seq, vh, (((0,), (0,)), ((), ())), preferred_element_type=F32)
            r_ref[s, hd] = cd_ref[hd] * (r + upd)
        cross_ref[:, hd * R_V_DIM:(hd + 1) * R_V_DIM] = cross


def _mix_sample_kernel(cd_ref, z_ref, x_ref, gt_ref, ws_ref, bs_ref, mask_ref, wa_ref, wb_ref, wo_ref,
                       cross_ref, xo_ref):
    _mix_step(cd_ref, z_ref, x_ref, gt_ref, ws_ref, bs_ref, mask_ref, wa_ref, wb_ref, wo_ref, xo_ref,
              cross_ref=cross_ref)


def _mix_prompt_kernel(cd_ref, z_ref, x_ref, gt_ref, ws_ref, bs_ref, mask_ref, wa_ref, wb_ref, wo_ref,
                       cds_ref, zq_ref, zk_ref, zv_ref, r0_ref, *rest, t_s):
    xo_ref, r_ref, cross_ref, rs_ref = rest[-4:]
    _mix_step(cd_ref, z_ref, x_ref, gt_ref, ws_ref, bs_ref, mask_ref, wa_ref, wb_ref, wo_ref, xo_ref,
              r_ref=r_ref)
    _carried_state_step(cds_ref, zq_ref, zk_ref, zv_ref, r0_ref, cross_ref, rs_ref, t_s)


def _mix_step(cd_ref, z_ref, x_ref, gt_ref, ws_ref, bs_ref, mask_ref, wa_ref, wb_ref, wo_ref, xo_ref, *,
              cross_ref=None, r_ref=None):
    nsb, nrows = z_ref.shape[:2]
    per_sb = nrows // CHUNK
    nseq = nsb * per_sb
    carried_state = cross_ref is not None
    if not carried_state:
        @pl.when(pl.program_id(1) == 0)
        def _():
            r_ref[...] = jnp.zeros_like(r_ref)

    mask = mask_ref[...]
    gdim = D_MODEL // A_GROUPS
    a_rows, b_rows = [], []
    for sq in range(nseq):
        sb, chunk = divmod(sq, per_sb)
        rs = slice(chunk * CHUNK, (chunk + 1) * CHUNK)

        def sec(off, width, sb=sb, rs=rs):
            return z_ref[sb, rs, off:off + width]

        v = sec(OFF_V, D_MODEL).astype(BF16)
        mixed = jnp.concatenate(
            [_dot(ws_ref[g], v[:, g * gdim:(g + 1) * gdim]) for g in range(A_GROUPS)], axis=1)
        a_rows.append((sec(OFF_U, D_MODEL).astype(F32) * (mixed + bs_ref[...])).astype(BF16))

        b_parts = []
        for hd in range(R_HEADS):
            qh = sec(OFF_Q + hd * R_QK_DIM, R_QK_DIM).astype(BF16)
            kh = sec(OFF_K + hd * R_QK_DIM, R_QK_DIM).astype(BF16)
            vh = sec(OFF_VR + hd * R_V_DIM, R_V_DIM).astype(BF16)
            s = lax.dot_general(qh, kh, (((1,), (1,)), ((), ())), preferred_element_type=F32) * mask
            y = _dot(s.astype(BF16), vh)
            if carried_state:
                y = y + cross_ref[sb, rs, hd * R_V_DIM:(hd + 1) * R_V_DIM]
            else:
                r = r_ref[sq, hd]
                y = y + _dot(qh, r.astype(BF16))
                upd = lax.dot_general(kh, vh, (((0,), (0,)), ((), ())), preferred_element_type=F32)
                r_ref[sq, hd] = cd_ref[hd] * (r + upd)
            mu = jnp.mean(y, axis=-1, keepdims=True)
            yc = y - mu
            var = jnp.mean(yc * yc, axis=-1, keepdims=True)
            yn = yc * lax.rsqrt(var + EPS)
            b_parts.append((sec(OFF_GR + hd * R_V_DIM, R_V_DIM).astype(F32) * yn).astype(BF16))
        b_rows.append(jnp.concatenate(b_parts, axis=1))

    pa = _dot(jnp.concatenate(a_rows, axis=0), wa_ref[...])
    pb = _dot(jnp.concatenate(b_rows, axis=0), wb_ref[...])
    ga = z_ref[:, :, OFF_GA:OFF_GA + D_MODEL].reshape(nseq * CHUNK, D_MODEL).astype(F32)
    gb = z_ref[:, :, OFF_GB:OFF_GB + D_MODEL].reshape(nseq * CHUNK, D_MODEL).astype(F32)
    mix = _dot((ga * pa + gb * pb).astype(BF16), wo_ref[...])
    xo_ref[...] = x_ref[...] + gt_ref[...] * mix.reshape(nsb, nrows, D_MODEL)


def _mix_specs(layer, sb, rb, gt_spec):
    row_map = lambda b, c: (b, c, 0)
    const2 = lambda b, c: (0, 0)
    wmap = lambda b, c: (layer, 0, 0)
    return [
        pl.BlockSpec(memory_space=pltpu.SMEM),
        pl.BlockSpec((sb, rb, IN_COLS), row_map),
        pl.BlockSpec((sb, rb, D_MODEL), row_map),
        gt_spec,
        pl.BlockSpec((A_GROUPS, CHUNK, CHUNK), lambda b, c: (0, 0, 0)),
        pl.BlockSpec((CHUNK, D_MODEL), const2),
        pl.BlockSpec((CHUNK, CHUNK), const2),
        pl.BlockSpec((None, D_MODEL, D_MODEL), wmap, pipeline_mode=pl.Buffered(1)),
        pl.BlockSpec((None, R_HEADS * R_V_DIM, D_MODEL), wmap, pipeline_mode=pl.Buffered(1)),
        pl.BlockSpec((None, D_MODEL, D_MODEL), wmap, pipeline_mode=pl.Buffered(1)),
    ]


def _mix_sample_call(cd, z, x, mod, ws, bs, mask, wa, wb, wo, cross, *, layer):
    rows = x.shape[0]
    rb = MIX_CHUNKS_F32 * CHUNK
    assert rows % rb == 0
    view = lambda a: a.reshape(1, rows, a.shape[-1])
    row_map = lambda b, c: (b, c, 0)
    gt_spec = pl.BlockSpec((None, 1, rb, D_MODEL), lambda b, c: (layer, b, c, 2))
    in_specs = _mix_specs(layer, 1, rb, gt_spec)
    in_specs.append(pl.BlockSpec((1, rb, R_HEADS * R_V_DIM), row_map))
    out = pl.pallas_call(
        _mix_sample_kernel,
        grid=(1, rows // rb),
        in_specs=in_specs,
        out_specs=pl.BlockSpec((1, rb, D_MODEL), row_map),
        out_shape=jax.ShapeDtypeStruct((1, rows, D_MODEL), F32),
        compiler_params=pltpu.CompilerParams(
            dimension_semantics=("arbitrary", "arbitrary"), vmem_limit_bytes=VMEM_LIMIT),
        name="token_mix_sample",
    )(cd, view(z), view(x), mod.reshape(mod.shape[0], 1, rows, MOD_COLS), ws, bs, mask, wa, wb, wo,
      view(cross))
    return out.reshape(rows, D_MODEL)


def _mix_prompt_call(cd, z, x, mod, ws, bs, mask, wa, wb, wo, cd_s, z_s, state_s, *, nseq, nchunk, t_s,
                     layer, stacked_p=None, stacked_s=None):
    rows = x.shape[0]
    seq_len = nchunk * CHUNK
    steps = (nseq // MIX_SEQS) * nchunk
    rows_s = z_s.shape[0]
    ret_rows = rows_s // steps
    assert nseq % MIX_SEQS == 0 and rows_s % steps == 0 and ret_rows % t_s == 0 and ret_rows % 8 == 0
    seqs_s = ret_rows // t_s
    qk_w = R_HEADS * R_QK_DIM
    v_w = R_HEADS * R_V_DIM
    view = lambda a: a.reshape(nseq, seq_len, a.shape[-1])
    row_map = lambda b, c: (b, c, 0)
    step_of = lambda b, c: b * nchunk + c
    gt_spec = pl.BlockSpec((MIX_SEQS, 1, D_MODEL), lambda b, c: (b, 0, 2))
    state_p_spec = pl.BlockSpec((None, MIX_SEQS, R_HEADS, R_QK_DIM, R_V_DIM), lambda b, c: (layer, b, 0, 0, 0))
    state_s_spec = pl.BlockSpec((None, seqs_s, R_HEADS, R_QK_DIM, R_V_DIM),
                                lambda b, c: (layer, step_of(b, c), 0, 0, 0))
    in_specs = _mix_specs(layer, MIX_SEQS, CHUNK, gt_spec) + [
        pl.BlockSpec(memory_space=pltpu.SMEM),
        pl.BlockSpec((ret_rows, qk_w), lambda b, c: (step_of(b, c), OFF_Q // qk_w)),
        pl.BlockSpec((ret_rows, qk_w), lambda b, c: (step_of(b, c), OFF_K // qk_w)),
        pl.BlockSpec((ret_rows, v_w), lambda b, c: (step_of(b, c), OFF_VR // v_w)),
        state_s_spec,
    ]
    args = [cd, view(z), view(x), mod, ws, bs, mask, wa, wb, wo, cd_s, z_s, z_s, z_s, state_s]
    aliases = {}
    for buf, out_idx in ((stacked_p, 1), (stacked_s, 3)):
        if buf is not None:
            aliases[len(args)] = out_idx
            in_specs.append(pl.BlockSpec(memory_space=pl.ANY))
            args.append(buf)
    x_out, r_p, cross, r_s = pl.pallas_call(
        functools.partial(_mix_prompt_kernel, t_s=t_s),
        grid=(nseq // MIX_SEQS, nchunk),
        in_specs=in_specs,
        out_specs=[
            pl.BlockSpec((MIX_SEQS, CHUNK, D_MODEL), row_map),
            state_p_spec,
            pl.BlockSpec((ret_rows, v_w), lambda b, c: (step_of(b, c), 0)),
            state_s_spec,
        ],
        out_shape=[
            jax.ShapeDtypeStruct((nseq, seq_len, D_MODEL), F32),
            jax.ShapeDtypeStruct((DEPTH, nseq, R_HEADS, R_QK_DIM, R_V_DIM), F32),
            jax.ShapeDtypeStruct((rows_s, v_w), F32),
            jax.ShapeDtypeStruct(state_s.shape, F32),
        ],
        input_output_aliases=aliases,
        compiler_params=pltpu.CompilerParams(
            dimension_semantics=("arbitrary", "arbitrary"), vmem_limit_bytes=VMEM_LIMIT),
        name="token_mix_prompt",
    )(*args)
    return x_out.reshape(rows, D_MODEL), r_p, cross, r_s


def _sorted_rows(block):
    need = 2 * block + N_EXPERTS * ROW_ALIGN + EXP_TILE
    return -(-need // GATHER_CHUNK) * GATHER_CHUNK


def _route_kernel(x_ref, sh_ref, sc_ref, g_ref, wr2_ref, br_ref, tri_ref, low_ref,
                  h_ref, route_ref, off_ref, nt_ref):
    blk = x_ref.shape[0]
    h = (_rms(x_ref[...]) * g_ref[...]) * (1.0 + sc_ref[...]) + sh_ref[...]
    h_hi, h_lo = _split_bf16(h)
    h_ref[...] = h_hi

    nt_dims = (((1,), (1,)), ((), ()))
    nr = ROUTER_ROWS
    r_hi = lax.dot_general(wr2_ref[...], h_hi, nt_dims, preferred_element_type=F32)
    r_lo = lax.dot_general(wr2_ref[0:nr, :], h_lo, nt_dims, preferred_element_type=F32)
    logits = r_hi[0:nr] + r_hi[nr:2 * nr] + r_lo + br_ref[:, :1]

    gl = logits[0:GROUP_ROWS]
    grow = lax.broadcasted_iota(jnp.int32, (GROUP_ROWS, blk), 0).astype(F32)
    gmax = jnp.max(gl, axis=0, keepdims=True)
    g_w = 1.0 / jnp.sum(jnp.exp(gl - gmax), axis=0, keepdims=True)
    g_idx = jnp.min(jnp.where(gl == gmax, grow, float(GROUP_ROWS)), axis=0, keepdims=True)

    el = logits[GROUP_ROWS:GROUP_ROWS + N_EXPERTS]
    erow_i = lax.broadcasted_iota(jnp.int32, (N_EXPERTS, blk), 0)
    erow = erow_i.astype(F32)
    el = jnp.where((erow_i >> 3).astype(F32) == g_idx, el, NEG)
    m1 = jnp.max(el, axis=0, keepdims=True)
    i1 = jnp.min(jnp.where(el == m1, erow, float(N_EXPERTS)), axis=0, keepdims=True)
    el2 = jnp.where(erow == i1, NEG, el)
    m2 = jnp.max(el2, axis=0, keepdims=True)
    i2 = jnp.min(jnp.where(el2 == m2, erow, float(N_EXPERTS)), axis=0, keepdims=True)
    t = jnp.exp(m2 - m1)
    w1 = g_w / (1.0 + t)
    w2 = g_w * t / (1.0 + t)

    s1 = erow == i1
    s2 = erow == i2
    onehot = jnp.where(s1 | s2, 1.0, 0.0)
    rank = _dot(onehot.astype(BF16), tri_ref[...])
    cnt = jnp.sum(onehot, axis=1, keepdims=True)
    units = jnp.floor((cnt + (ROW_ALIGN - 1.0)) * (1.0 / ROW_ALIGN))
    units = jnp.broadcast_to(units, (N_EXPERTS, LANES))
    off = float(ROW_ALIGN) * _dot(low_ref[...], units.astype(BF16))
    base = off[:, :1] + rank
    pos1 = jnp.sum(jnp.where(s1, base, 0.0), axis=0, keepdims=True)
    pos2 = jnp.sum(jnp.where(s2, base, 0.0), axis=0, keepdims=True)

    r8 = lax.broadcasted_iota(jnp.int32, (8, blk), 0)
    route_ref[...] = jnp.where(r8 == 0, pos1, jnp.where(r8 == 1, pos2, jnp.where(r8 == 2, w1,
                               jnp.where(r8 == 3, w2, 0.0))))
    off_ref[...] = off.astype(jnp.int32)
    tiles = jnp.floor((cnt + (EXP_TILE - 1.0)) * (1.0 / EXP_TILE))
    nt_ref[...] = jnp.broadcast_to(tiles, (N_EXPERTS, LANES)).astype(jnp.int32)


def _experts_kernel(off_sm, nt_sm, x_ref, h_ref, route_ref, gt_ref, wg_ref, wu_ref, wd_ref, fg_ref, o_ref,
                    p_scr, xs_scr, ys_scr, cw_scr, *, final_norm):
    i = pl.program_id(0)
    step = pl.program_id(1)
    nrows, blk = p_scr.shape

    @pl.when(step == 0)
    def _():
        pos1 = route_ref[0:1, :]
        pos2 = route_ref[1:2, :]
        w1 = route_ref[2:3, :]
        w2 = route_ref[3:4, :]
        row_iota = lax.broadcasted_iota(jnp.int32, (GATHER_CHUNK, blk), 0)

        def gather(c, carry):
            r0 = pl.multiple_of(c * GATHER_CHUNK, GATHER_CHUNK)
            prow = (row_iota + r0).astype(F32)
            m1 = prow == pos1
            m2 = prow == pos2
            perm = jnp.where(m1 | m2, 1.0, 0.0).astype(BF16)
            p_scr[pl.ds(r0, GATHER_CHUNK), :] = perm
            xs_scr[pl.ds(r0, GATHER_CHUNK), :] = _dot(perm, h_ref[...]).astype(BF16)
            cw = jnp.sum(jnp.where(m1, w1, 0.0) + jnp.where(m2, w2, 0.0), axis=1, keepdims=True)
            cw_scr[pl.ds(r0, GATHER_CHUNK), :] = jnp.broadcast_to(cw, (GATHER_CHUNK, LANES))
            ys_scr[pl.ds(r0, GATHER_CHUNK), :] = jnp.zeros((GATHER_CHUNK, D_MODEL), BF16)
            return carry

        lax.fori_loop(0, nrows // GATHER_CHUNK, gather, 0)

    def tile(g, s):
        s = pl.multiple_of(s, ROW_ALIGN)
        rows = xs_scr[pl.ds(s, EXP_TILE), :]
        gate = _dot(rows, wg_ref[g])
        up = _dot(rows, wu_ref[g])
        cw = cw_scr[pl.ds(s, EXP_TILE), :][:, :1]
        hid = (gate * _sigmoid(gate) * up * cw).astype(BF16)
        return s, _dot(hid, wd_ref[g]).astype(BF16)

    def put(s, y):
        ys_scr[pl.ds(s, EXP_TILE), :] = y

    starts = [off_sm[i, step * EXPERTS_PER_STEP + g] for g in range(EXPERTS_PER_STEP)]
    first = [tile(g, starts[g]) for g in range(EXPERTS_PER_STEP)]
    for g in range(EXPERTS_PER_STEP):
        put(*first[g])

        def more(t, carry, g=g):
            put(*tile(g, starts[g] + t * EXP_TILE))
            return carry

        lax.fori_loop(1, nt_sm[i, step * EXPERTS_PER_STEP + g], more, 0)

    @pl.when(step == N_EXPERTS // EXPERTS_PER_STEP - 1)
    def _():
        tn_dims = (((0,), (0,)), ((), ()))
        for r0 in range(0, blk, GATHER_CHUNK):
            rs = slice(r0, r0 + GATHER_CHUNK)
            moe = lax.dot_general(p_scr[:, rs], ys_scr[...], tn_dims, preferred_element_type=F32)
            gt = gt_ref[...] if gt_ref.shape[0] == 1 else gt_ref[rs, :]
            y = x_ref[rs, :] + gt * moe
            if final_norm:
                y = _rms(y) * fg_ref[...]
            o_ref[rs, :] = y


def _moe_call(x, mod, g2, wr2, br, w_exp, fg, *, layer, block, per_row, final_norm):
    rows = x.shape[0]
    nblk = rows // block
    nsorted = _sorted_rows(block)
    if per_row:
        mspec = lambda s: pl.BlockSpec((None, block, D_MODEL), lambda i, *_: (layer, i, s))
    else:
        blocks_per_seq = (rows // mod.shape[0]) // block
        mspec = lambda s: pl.BlockSpec((None, 1, D_MODEL), lambda i, *_: (i // blocks_per_seq, 0, s))
    const = lambda i: (0, 0)
    ridx = jnp.arange(block)
    tri = (ridx[:, None] < ridx[None, :]).astype(BF16)
    eidx = jnp.arange(N_EXPERTS)
    low = (eidx[None, :] < eidx[:, None]).astype(BF16)
    h, route, off, nt = pl.pallas_call(
        _route_kernel,
        grid=(nblk,),
        in_specs=[
            pl.BlockSpec((block, D_MODEL), lambda i: (i, 0)),
            mspec(3), mspec(4),
            pl.BlockSpec((1, D_MODEL), const),
            pl.BlockSpec((2 * ROUTER_ROWS, D_MODEL), const),
            pl.BlockSpec((ROUTER_ROWS, LANES), const),
            pl.BlockSpec((block, block), const),
            pl.BlockSpec((N_EXPERTS, N_EXPERTS), const),
        ],
        out_specs=[
            pl.BlockSpec((block, D_MODEL), lambda i: (i, 0)),
            pl.BlockSpec((None, 8, block), lambda i: (i, 0, 0)),
            pl.BlockSpec((None, N_EXPERTS, LANES), lambda i: (i, 0, 0)),
            pl.BlockSpec((None, N_EXPERTS, LANES), lambda i: (i, 0, 0)),
        ],
        out_shape=[
            jax.ShapeDtypeStruct((rows, D_MODEL), BF16),
            jax.ShapeDtypeStruct((nblk, 8, block), F32),
            jax.ShapeDtypeStruct((nblk, N_EXPERTS, LANES), jnp.int32),
            jax.ShapeDtypeStruct((nblk, N_EXPERTS, LANES), jnp.int32),
        ],
        compiler_params=pltpu.CompilerParams(vmem_limit_bytes=VMEM_LIMIT),
        name="moe_route",
    )(x, mod, mod, g2, wr2, br, tri, low)

    const2 = lambda i, e, *_: (0, 0)
    wmap = lambda i, s, *_: (layer, s, 0, 0)
    return pl.pallas_call(
        functools.partial(_experts_kernel, final_norm=final_norm),
        grid_spec=pltpu.PrefetchScalarGridSpec(
            num_scalar_prefetch=2,
            grid=(nblk, N_EXPERTS // EXPERTS_PER_STEP),
            in_specs=[
                pl.BlockSpec((block, D_MODEL), lambda i, e, *_: (i, 0)),
                pl.BlockSpec((block, D_MODEL), lambda i, e, *_: (i, 0), pipeline_mode=pl.Buffered(1)),
                pl.BlockSpec((None, 8, block), lambda i, e, *_: (i, 0, 0)),
                mspec(5),
                pl.BlockSpec((None, EXPERTS_PER_STEP, D_MODEL, EXPERT_DIM), wmap),
                pl.BlockSpec((None, EXPERTS_PER_STEP, D_MODEL, EXPERT_DIM), wmap),
                pl.BlockSpec((None, EXPERTS_PER_STEP, EXPERT_DIM, D_MODEL), wmap),
                pl.BlockSpec((1, D_MODEL), const2),
            ],
            out_specs=pl.BlockSpec((block, D_MODEL), lambda i, e, *_: (i, 0)),
            scratch_shapes=[
                pltpu.VMEM((nsorted, block), BF16),
                pltpu.VMEM((nsorted, D_MODEL), BF16),
                pltpu.VMEM((nsorted, D_MODEL), BF16),
                pltpu.VMEM((nsorted, LANES), F32),
            ],
        ),
        out_shape=jax.ShapeDtypeStruct((rows, D_MODEL), F32),
        compiler_params=pltpu.CompilerParams(
            dimension_semantics=("arbitrary", "arbitrary"), vmem_limit_bytes=VMEM_LIMIT),
        name="moe_experts",
    )(off[:, :, 0], nt[:, :, 0], x, h, route, mod, *w_exp, fg)


def _rope_tables(pos):
    half = R_QK_DIM // 2
    inv_freq = jnp.power(ROPE_BASE, -jnp.arange(half, dtype=F32) / half)
    ang = pos.astype(F32)[:, None] * inv_freq[None, :]
    return jnp.cos(ang), jnp.sin(ang)


def _decay_tables(chunk, rows):
    log_g = jnp.log1p(-jnp.power(2.0, -5.0 - jnp.arange(R_HEADS, dtype=F32)))
    t1 = (jnp.arange(rows) % chunk).astype(F32) + 1.0
    dq = jnp.exp(log_g[None, :] * t1[:, None])
    dk = jnp.exp(-log_g[None, :] * t1[:, None]) * (R_QK_DIM ** -0.5)
    cd = jnp.exp(log_g * chunk)
    rep = lambda a: jnp.repeat(a, LANES, axis=1)
    return rep(dq), rep(dk), cd


def kernel(x_prompt, x_sample, state_ret, c_prompt, c_sample, w_mod, b_mod, norm1_g, w_in, ln_v_g, ln_v_b,
           w_s, b_s, w_a_out, w_b_out, w_o, norm2_g, w_router_group, b_router_group, w_router_expert,
           b_router_expert, w_gate, w_up, w_down, final_g):
    n_p, t_p, _ = x_prompt.shape
    n_s, t_s, _ = x_sample.shape
    assert t_p % CHUNK == 0 and (n_s * t_s) % CHUNK == 0 and CHUNK % t_s == 0
    chunks_p = t_p // CHUNK

    mod = _mod_call(jnp.concatenate([c_prompt, c_sample], axis=0), w_mod, b_mod)

    cos_p, sin_p = _rope_tables(jnp.arange(t_p, dtype=jnp.int32))
    cos_s, sin_s = _rope_tables(PAST_LEN + jnp.arange(IN_TILE_F32, dtype=jnp.int32) % t_s)
    dq_p, dk_p, cd_p = _decay_tables(CHUNK, IN_TILE_BF16)
    dq_s, dk_s, cd_s = _decay_tables(t_s, IN_TILE_F32)

    idx = jnp.arange(CHUNK)
    causal = idx[:, None] >= idx[None, :]
    same_seq = (idx[:, None] // t_s) == (idx[None, :] // t_s)
    mask_p = causal.astype(F32)
    mask_s = (causal & same_seq).astype(F32)

    xp = x_prompt.reshape(n_p * t_p, D_MODEL)
    xs = x_sample.reshape(n_s * t_s, D_MODEL)
    fg = final_g.reshape(1, D_MODEL)
    w_in_b = w_in.astype(BF16)
    wa, wb, wo = w_a_out.astype(BF16), w_b_out.astype(BF16), w_o.astype(BF16)
    w_exp = (w_gate.astype(BF16), w_up.astype(BF16), w_down.astype(BF16))
    mod_s = jnp.broadcast_to(mod[:, n_p:, None, :], (DEPTH, n_s, t_s, MOD_COLS))
    mod_s = mod_s.reshape(DEPTH, n_s * t_s, MOD_COLS)
    r_prompt, r_sample, v_sample = None, None, []
    for l in range(DEPTH):
        mod_p = mod[l, :n_p].reshape(n_p, 1, MOD_COLS)
        g1 = norm1_g[l].reshape(1, D_MODEL)
        g2 = norm2_g[l].reshape(1, D_MODEL)
        lng = ln_v_g[l].reshape(1, D_MODEL)
        lnb = ln_v_b[l].reshape(1, D_MODEL)

        ws_p = jnp.where(causal[None], w_s[l], 0.0).astype(BF16)
        bs_p = jnp.repeat(b_s[l].T, D_MODEL // A_GROUPS, axis=1)
        blk = jnp.where(causal[:t_s, :t_s][None], w_s[l][:, :t_s, :t_s], 0.0)
        ws_s = jnp.where(same_seq[None], jnp.tile(blk, (1, CHUNK // t_s, CHUNK // t_s)), 0.0).astype(BF16)
        bs_s = jnp.tile(bs_p[:t_s], (CHUNK // t_s, 1))

        zp = _inproj_call(xp, mod_p, g1, cos_p, sin_p, dq_p, dk_p, lng, lnb, w_in_b,
                          layer=l, per_row=False, z_dtype=BF16)
        zs = _inproj_call(xs, mod_s, g1, cos_s, sin_s, dq_s, dk_s, lng, lnb, w_in_b,
                          layer=l, per_row=True, z_dtype=F32)

        xp, r_prompt, cross, r_sample = _mix_prompt_call(
            cd_p, zp, xp, mod_p, ws_p, bs_p, mask_p, wa, wb, wo, cd_s, zs, state_ret,
            nseq=n_p, nchunk=chunks_p, t_s=t_s, layer=l, stacked_p=r_prompt, stacked_s=r_sample)
        xs = _mix_sample_call(cd_s, zs, xs, mod_s, ws_s, bs_s, mask_s, wa, wb, wo, cross, layer=l)

        tail = ROUTER_ROWS - GROUP_ROWS - N_EXPERTS
        wr = jnp.concatenate([jnp.pad(w_router_group[l].T, ((0, GROUP_ROWS - N_GROUPS), (0, 0))),
                              jnp.pad(w_router_expert[l].T, ((0, tail), (0, 0)))], axis=0)
        br = jnp.concatenate([jnp.pad(b_router_group[l], (0, GROUP_ROWS - N_GROUPS), constant_values=NEG),
                              jnp.pad(b_router_expert[l], (0, tail))])
        wr2 = jnp.concatenate(_split_bf16(wr), axis=0)
        br = jnp.broadcast_to(br[:, None], (ROUTER_ROWS, LANES))
        last = l == DEPTH - 1
        xp = _moe_call(xp, mod_p, g2, wr2, br, w_exp, fg,
                       layer=l, block=MOE_BLOCK, per_row=False, final_norm=last)
        xs = _moe_call(xs, mod_s, g2, wr2, br, w_exp, fg,
                       layer=l, block=n_s * t_s, per_row=True, final_norm=last)

        v_sample.append(zs[:, OFF_V:OFF_V + D_MODEL].reshape(n_s, t_s, D_MODEL))

    return (xp.reshape(n_p, t_p, D_MODEL), xs.reshape(n_s, t_s, D_MODEL),
            r_prompt, r_sample, jnp.stack(v_sample))
```

```python
import functools

import jax
import jax.numpy as jnp
from jax import lax
from jax.experimental import pallas as pl
from jax.experimental.pallas import tpu as pltpu

D_MODEL = 1024
DEPTH = 2
PAST_LEN = 16384
A_GROUPS = 8
CHUNK = 128
R_HEADS = 4
R_QK_DIM = 256
R_V_DIM = 512
ROPE_BASE = 10000.0
N_GROUPS = 4
EXPERTS_PER_GROUP = 8
N_EXPERTS = 32
EXPERT_DIM = 256
EPS = 1e-6
IN_COLS = 10 * D_MODEL
MOD_COLS = 6 * D_MODEL

OFF_U, OFF_V, OFF_Q, OFF_K, OFF_VR, OFF_GR, OFF_GA, OFF_GB = 0, 1024, 2048, 3072, 4096, 6144, 8192, 9216

LANES = 128
IN_TILE_BF16 = 256
IN_TILE_F32 = 256
MIX_SEQS = 2
MIX_CHUNKS_F32 = 2
MOE_BLOCK = 1024
EXP_TILE = 128
ROW_ALIGN = 16
GATHER_CHUNK = 256
GROUP_ROWS = 8
ROUTER_ROWS = 48
EXPERTS_PER_STEP = 4
MOD_TILE = 1536
VMEM_LIMIT = 56 * 1024 * 1024
NEG = float("-inf")

BF16 = jnp.bfloat16
F32 = jnp.float32


def _dot(a, b):
    return jnp.dot(a, b, preferred_element_type=F32)


def _sigmoid(x):
    return 1.0 / (1.0 + jnp.exp(-x))


def _gelu_tanh(x):
    return x * (0.5 * (1.0 + jnp.tanh(0.7978845608028654 * (x + 0.044715 * (x * x * x)))))


def _rms(x):
    return x * lax.rsqrt(jnp.mean(x * x, axis=-1, keepdims=True) + EPS)


def _split_bf16(w):
    hi = w.astype(BF16)
    return hi, (w - hi.astype(F32)).astype(BF16)


def _dot_split(a, b):
    a_hi, a_lo = _split_bf16(a)
    b_hi, b_lo = _split_bf16(b)
    return _dot(a_hi, b_hi) + (_dot(a_lo, b_hi) + _dot(a_hi, b_lo))


def _mod_kernel(c_ref, w_ref, b_ref, o_ref):
    c = c_ref[...]
    o_ref[...] = _dot_split(c * _sigmoid(c), w_ref[...]) + b_ref[...]


def _mod_call(c_all, w_mod, b_mod):
    n = c_all.shape[0]
    return pl.pallas_call(
        _mod_kernel,
        grid=(DEPTH, MOD_COLS // MOD_TILE),
        in_specs=[
            pl.BlockSpec((n, D_MODEL), lambda l, j: (0, 0)),
            pl.BlockSpec((None, D_MODEL, MOD_TILE), lambda l, j: (l, 0, j)),
            pl.BlockSpec((None, 1, MOD_TILE), lambda l, j: (l, 0, j)),
        ],
        out_specs=pl.BlockSpec((None, n, MOD_TILE), lambda l, j: (l, 0, j)),
        out_shape=jax.ShapeDtypeStruct((DEPTH, n, MOD_COLS), F32),
        compiler_params=pltpu.CompilerParams(vmem_limit_bytes=VMEM_LIMIT),
        name="adaln_mod",
    )(c_all, w_mod, b_mod.reshape(DEPTH, 1, MOD_COLS))


def _inproj_kernel(x_ref, sh_ref, sc_ref, g_ref, cos_ref, sin_ref, dq_ref, dk_ref, lng_ref, lnb_ref,
                   w_ref, *rest):
    n_cast = (len(rest) - 1) // 2
    z_ref = rest[n_cast]
    for src_ref, dst_ref in zip(rest[:n_cast], rest[n_cast + 1:]):
        dst_ref[...] = src_ref[...].astype(dst_ref.dtype)

    h = (_rms(x_ref[...]) * g_ref[...]) * (1.0 + sc_ref[...]) + sh_ref[...]
    h = h.astype(BF16)
    cos = cos_ref[...]
    sin = sin_ref[...]

    def proj(off, width=D_MODEL):
        return _dot(h, w_ref[:, off:off + width])

    def put(off, val):
        z_ref[:, off:off + val.shape[1]] = val.astype(z_ref.dtype)

    def plain(off):
        put(off, proj(off))

    def swish(off):
        g = proj(off)
        put(off, g * _sigmoid(g))

    def rotary(off, d_ref):
        half = R_QK_DIM // 2
        acc = proj(off)
        for hd in range(R_HEADS):
            x1 = acc[:, hd * R_QK_DIM:hd * R_QK_DIM + half]
            x2 = acc[:, hd * R_QK_DIM + half:(hd + 1) * R_QK_DIM]
            dec = d_ref[:, hd * LANES:(hd + 1) * LANES]
            put(off + hd * R_QK_DIM, (x1 * cos - x2 * sin) * dec)
            put(off + hd * R_QK_DIM + half, (x1 * sin + x2 * cos) * dec)

    put(OFF_U, _gelu_tanh(proj(OFF_U)))
    plain(OFF_VR)
    v = _gelu_tanh(proj(OFF_V))
    mu = jnp.mean(v, axis=-1, keepdims=True)
    vc = v - mu
    var = jnp.mean(vc * vc, axis=-1, keepdims=True)
    put(OFF_V, vc * lax.rsqrt(var + EPS) * lng_ref[...] + lnb_ref[...])
    plain(OFF_VR + D_MODEL)
    rotary(OFF_Q, dq_ref)
    put(OFF_GA, _sigmoid(proj(OFF_GA)))
    rotary(OFF_K, dk_ref)
    put(OFF_GB, _sigmoid(proj(OFF_GB)))
    swish(OFF_GR)
    swish(OFF_GR + D_MODEL)


def _inproj_call(x, mod, g1, cos, sin, dq, dk, lng, lnb, w_in, *, layer, per_row, z_dtype, cast=()):
    rows = x.shape[0]
    tile = dq.shape[0]
    steps = rows // tile
    cast_in, cast_specs_in, cast_specs_out, cast_shapes = [], [], [], []
    for w in cast:
        cols = w.shape[-1]
        flat = w.reshape(w.shape[0], -1, cols)
        per = flat.shape[1] // steps
        assert flat.shape[1] % steps == 0 and per % ROW_ALIGN == 0
        cast_in.append(flat)
        cast_specs_in.append(pl.BlockSpec((None, per, cols), lambda i: (layer, i, 0)))
        cast_specs_out.append(pl.BlockSpec((per, cols), lambda i: (i, 0)))
        cast_shapes.append(jax.ShapeDtypeStruct(flat.shape[1:], BF16))
    tiles_per_seq = None if per_row else (rows // mod.shape[0]) // tile
    if per_row:
        sh_spec = pl.BlockSpec((None, tile, D_MODEL), lambda i: (layer, i, 0))
        sc_spec = pl.BlockSpec((None, tile, D_MODEL), lambda i: (layer, i, 1))
        rope_spec = pl.BlockSpec((tile, LANES), lambda i: (0, 0))
    else:
        sh_spec = pl.BlockSpec((None, 1, D_MODEL), lambda i: (i // tiles_per_seq, 0, 0))
        sc_spec = pl.BlockSpec((None, 1, D_MODEL), lambda i: (i // tiles_per_seq, 0, 1))
        rope_spec = pl.BlockSpec((tile, LANES), lambda i: (i % tiles_per_seq, 0))
    const = lambda i: (0, 0)
    z, *casted = pl.pallas_call(
        _inproj_kernel,
        grid=(steps,),
        in_specs=[
            pl.BlockSpec((tile, D_MODEL), lambda i: (i, 0)),
            sh_spec, sc_spec,
            pl.BlockSpec((1, D_MODEL), const),
            rope_spec, rope_spec,
            pl.BlockSpec((tile, R_HEADS * LANES), const),
            pl.BlockSpec((tile, R_HEADS * LANES), const),
            pl.BlockSpec((1, D_MODEL), const),
            pl.BlockSpec((1, D_MODEL), const),
            pl.BlockSpec((None, D_MODEL, IN_COLS), lambda i: (layer, 0, 0), pipeline_mode=pl.Buffered(1)),
            *cast_specs_in,
        ],
        out_specs=[pl.BlockSpec((tile, IN_COLS), lambda i: (i, 0)), *cast_specs_out],
        out_shape=[jax.ShapeDtypeStruct((rows, IN_COLS), z_dtype), *cast_shapes],
        compiler_params=pltpu.CompilerParams(vmem_limit_bytes=VMEM_LIMIT),
        name="in_proj",
    )(x, mod, mod, g1, cos, sin, dq, dk, lng, lnb, w_in, *cast_in)
    return (z, *[c.reshape(w.shape[1:]) for c, w in zip(casted, cast)])


def _carried_state_step(cd_ref, z_q_ref, z_k_ref, z_v_ref, r0_ref, cross_ref, r_ref, t_s):
    nrows = z_q_ref.shape[0]
    q = z_q_ref[...].astype(BF16)
    k = z_k_ref[...]
    v = z_v_ref[...].astype(BF16)
    row_k = lax.broadcasted_iota(jnp.int32, (nrows, R_QK_DIM), 0)
    row_v = lax.broadcasted_iota(jnp.int32, (nrows, R_V_DIM), 0)
    for hd in range(R_HEADS):
        qh = q[:, hd * R_QK_DIM:(hd + 1) * R_QK_DIM]
        kh = k[:, hd * R_QK_DIM:(hd + 1) * R_QK_DIM]
        vh = v[:, hd * R_V_DIM:(hd + 1) * R_V_DIM]
        cross = jnp.zeros((nrows, R_V_DIM), F32)
        for s in range(nrows // t_s):
            r = r0_ref[s, hd]
            lo, hi = s * t_s, (s + 1) * t_s
            cross = jnp.where((row_v >= lo) & (row_v < hi), _dot(qh, r.astype(BF16)), cross)
            k_seq = jnp.where((row_k >= lo) & (row_k < hi), kh, 0.0).astype(BF16)
            upd = lax.dot_general(k_seq, vh, (((0,), (0,)), ((), ())), preferred_element_type=F32)
            r_ref[s, hd] = cd_ref[hd] * (r + upd)
        cross_ref[:, hd * R_V_DIM:(hd + 1) * R_V_DIM] = cross


def _mix_sample_kernel(cd_ref, z_ref, x_ref, gt_ref, ws_ref, bs_ref, mask_ref, wa_ref, wb_ref, wo_ref,
                       cross_ref, xo_ref):
    _mix_step(cd_ref, z_ref, x_ref, gt_ref, ws_ref, bs_ref, mask_ref, wa_ref, wb_ref, wo_ref, xo_ref,
              cross_ref=cross_ref)


def _mix_prompt_kernel(cd_ref, z_ref, x_ref, gt_ref, ws_ref, bs_ref, mask_ref, wa_ref, wb_ref, wo_ref,
                       cds_ref, zq_ref, zk_ref, zv_ref, r0_ref, *rest, t_s):
    xo_ref, r_ref, cross_ref, rs_ref = rest[-4:]
    _mix_step(cd_ref, z_ref, x_ref, gt_ref, ws_ref, bs_ref, mask_ref, wa_ref, wb_ref, wo_ref, xo_ref,
              r_ref=r_ref)
    _carried_state_step(cds_ref, zq_ref, zk_ref, zv_ref, r0_ref, cross_ref, rs_ref, t_s)


def _mix_step(cd_ref, z_ref, x_ref, gt_ref, ws_ref, bs_ref, mask_ref, wa_ref, wb_ref, wo_ref, xo_ref, *,
              cross_ref=None, r_ref=None):
    nsb, nrows = z_ref.shape[:2]
    per_sb = nrows // CHUNK
    nseq = nsb * per_sb
    carried_state = cross_ref is not None
    if not carried_state:
        @pl.when(pl.program_id(1) == 0)
        def _():
            r_ref[...] = jnp.zeros_like(r_ref)

    mask = mask_ref[...]
    gdim = D_MODEL // A_GROUPS
    a_rows, b_rows = [], []
    for sq in range(nseq):
        sb, chunk = divmod(sq, per_sb)
        rs = slice(chunk * CHUNK, (chunk + 1) * CHUNK)

        def sec(off, width, sb=sb, rs=rs):
            return z_ref[sb, rs, off:off + width]

        v = sec(OFF_V, D_MODEL).astype(BF16)
        mixed = jnp.concatenate(
            [_dot(ws_ref[g], v[:, g * gdim:(g + 1) * gdim]) for g in range(A_GROUPS)], axis=1)
        a_rows.append((sec(OFF_U, D_MODEL).astype(F32) * (mixed + bs_ref[...])).astype(BF16))

        b_parts = []
        for hd in range(R_HEADS):
            qh = sec(OFF_Q + hd * R_QK_DIM, R_QK_DIM).astype(BF16)
            kh = sec(OFF_K + hd * R_QK_DIM, R_QK_DIM).astype(BF16)
            vh = sec(OFF_VR + hd * R_V_DIM, R_V_DIM).astype(BF16)
            s = lax.dot_general(qh, kh, (((1,), (1,)), ((), ())), preferred_element_type=F32) * mask
            y = _dot(s.astype(BF16), vh)
            if carried_state:
                y = y + cross_ref[sb, rs, hd * R_V_DIM:(hd + 1) * R_V_DIM]
            else:
                r = r_ref[sq, hd]
                y = y + _dot(qh, r.astype(BF16))
                upd = lax.dot_general(kh, vh, (((0,), (0,)), ((), ())), preferred_element_type=F32)
                r_ref[sq, hd] = cd_ref[hd] * (r + upd)
            mu = jnp.mean(y, axis=-1, keepdims=True)
            yc = y - mu
            var = jnp.mean(yc * yc, axis=-1, keepdims=True)
            yn = yc * lax.rsqrt(var + EPS)
            b_parts.append((sec(OFF_GR + hd * R_V_DIM, R_V_DIM).astype(F32) * yn).astype(BF16))
        b_rows.append(jnp.concatenate(b_parts, axis=1))

    pa = _dot(jnp.concatenate(a_rows, axis=0), wa_ref[...])
    pb = _dot(jnp.concatenate(b_rows, axis=0), wb_ref[...])
    ga = z_ref[:, :, OFF_GA:OFF_GA + D_MODEL].reshape(nseq * CHUNK, D_MODEL).astype(F32)
    gb = z_ref[:, :, OFF_GB:OFF_GB + D_MODEL].reshape(nseq * CHUNK, D_MODEL).astype(F32)
    mix = _dot((ga * pa + gb * pb).astype(BF16), wo_ref[...])
    xo_ref[...] = x_ref[...] + gt_ref[...] * mix.reshape(nsb, nrows, D_MODEL)


def _mix_specs(sb, rb, gt_spec):
    row_map = lambda b, c: (b, c, 0)
    const2 = lambda b, c: (0, 0)
    return [
        pl.BlockSpec(memory_space=pltpu.SMEM),
        pl.BlockSpec((sb, rb, IN_COLS), row_map),
        pl.BlockSpec((sb, rb, D_MODEL), row_map),
        gt_spec,
        pl.BlockSpec((A_GROUPS, CHUNK, CHUNK), lambda b, c: (0, 0, 0)),
        pl.BlockSpec((CHUNK, D_MODEL), const2),
        pl.BlockSpec((CHUNK, CHUNK), const2),
        pl.BlockSpec((D_MODEL, D_MODEL), const2, pipeline_mode=pl.Buffered(1)),
        pl.BlockSpec((R_HEADS * R_V_DIM, D_MODEL), const2, pipeline_mode=pl.Buffered(1)),
        pl.BlockSpec((D_MODEL, D_MODEL), const2, pipeline_mode=pl.Buffered(1)),
    ]


def _mix_sample_call(cd, z, x, mod, ws, bs, mask, wa, wb, wo, cross, *, layer):
    rows = x.shape[0]
    rb = MIX_CHUNKS_F32 * CHUNK
    assert rows % rb == 0
    view = lambda a: a.reshape(1, rows, a.shape[-1])
    row_map = lambda b, c: (b, c, 0)
    gt_spec = pl.BlockSpec((None, 1, rb, D_MODEL), lambda b, c: (layer, b, c, 2))
    in_specs = _mix_specs(1, rb, gt_spec)
    in_specs.append(pl.BlockSpec((1, rb, R_HEADS * R_V_DIM), row_map))
    out = pl.pallas_call(
        _mix_sample_kernel,
        grid=(1, rows // rb),
        in_specs=in_specs,
        out_specs=pl.BlockSpec((1, rb, D_MODEL), row_map),
        out_shape=jax.ShapeDtypeStruct((1, rows, D_MODEL), F32),
        compiler_params=pltpu.CompilerParams(
            dimension_semantics=("arbitrary", "arbitrary"), vmem_limit_bytes=VMEM_LIMIT),
        name="token_mix_sample",
    )(cd, view(z), view(x), mod.reshape(mod.shape[0], 1, rows, MOD_COLS), ws, bs, mask, wa, wb, wo,
      view(cross))
    return out.reshape(rows, D_MODEL)


def _mix_prompt_call(cd, z, x, mod, ws, bs, mask, wa, wb, wo, cd_s, z_s, state_s, *, nseq, nchunk, t_s,
                     layer, stacked_p=None, stacked_s=None):
    rows = x.shape[0]
    seq_len = nchunk * CHUNK
    steps = (nseq // MIX_SEQS) * nchunk
    rows_s = z_s.shape[0]
    ret_rows = rows_s // steps
    assert nseq % MIX_SEQS == 0 and rows_s % steps == 0 and ret_rows % t_s == 0 and ret_rows % 8 == 0
    seqs_s = ret_rows // t_s
    qk_w = R_HEADS * R_QK_DIM
    v_w = R_HEADS * R_V_DIM
    view = lambda a: a.reshape(nseq, seq_len, a.shape[-1])
    row_map = lambda b, c: (b, c, 0)
    step_of = lambda b, c: b * nchunk + c
    gt_spec = pl.BlockSpec((MIX_SEQS, 1, D_MODEL), lambda b, c: (b, 0, 2))
    state_p_spec = pl.BlockSpec((None, MIX_SEQS, R_HEADS, R_QK_DIM, R_V_DIM), lambda b, c: (layer, b, 0, 0, 0))
    state_s_spec = pl.BlockSpec((None, seqs_s, R_HEADS, R_QK_DIM, R_V_DIM),
                                lambda b, c: (layer, step_of(b, c), 0, 0, 0))
    in_specs = _mix_specs(MIX_SEQS, CHUNK, gt_spec) + [
        pl.BlockSpec(memory_space=pltpu.SMEM),
        pl.BlockSpec((ret_rows, qk_w), lambda b, c: (step_of(b, c), OFF_Q // qk_w)),
        pl.BlockSpec((ret_rows, qk_w), lambda b, c: (step_of(b, c), OFF_K // qk_w)),
        pl.BlockSpec((ret_rows, v_w), lambda b, c: (step_of(b, c), OFF_VR // v_w)),
        state_s_spec,
    ]
    args = [cd, view(z), view(x), mod, ws, bs, mask, wa, wb, wo, cd_s, z_s, z_s, z_s, state_s]
    aliases = {}
    for buf, out_idx in ((stacked_p, 1), (stacked_s, 3)):
        if buf is not None:
            aliases[len(args)] = out_idx
            in_specs.append(pl.BlockSpec(memory_space=pl.ANY))
            args.append(buf)
    x_out, r_p, cross, r_s = pl.pallas_call(
        functools.partial(_mix_prompt_kernel, t_s=t_s),
        grid=(nseq // MIX_SEQS, nchunk),
        in_specs=in_specs,
        out_specs=[
            pl.BlockSpec((MIX_SEQS, CHUNK, D_MODEL), row_map),
            state_p_spec,
            pl.BlockSpec((ret_rows, v_w), lambda b, c: (step_of(b, c), 0)),
            state_s_spec,
        ],
        out_shape=[
            jax.ShapeDtypeStruct((nseq, seq_len, D_MODEL), F32),
            jax.ShapeDtypeStruct((DEPTH, nseq, R_HEADS, R_QK_DIM, R_V_DIM), F32),
            jax.ShapeDtypeStruct((rows_s, v_w), F32),
            jax.ShapeDtypeStruct(state_s.shape, F32),
        ],
        input_output_aliases=aliases,
        compiler_params=pltpu.CompilerParams(
            dimension_semantics=("arbitrary", "arbitrary"), vmem_limit_bytes=VMEM_LIMIT),
        name="token_mix_prompt",
    )(*args)
    return x_out.reshape(rows, D_MODEL), r_p, cross, r_s


def _sorted_rows(block):
    need = 2 * block + N_EXPERTS * ROW_ALIGN + EXP_TILE
    return -(-need // GATHER_CHUNK) * GATHER_CHUNK


def _route_kernel(x_ref, sh_ref, sc_ref, g_ref, wr2_ref, br_ref, tri_ref, low_ref,
                  h_ref, route_ref, off_ref, nt_ref):
    blk = x_ref.shape[0]
    h = (_rms(x_ref[...]) * g_ref[...]) * (1.0 + sc_ref[...]) + sh_ref[...]
    h_hi, h_lo = _split_bf16(h)
    h_ref[...] = h_hi

    nt_dims = (((1,), (1,)), ((), ()))
    nr = ROUTER_ROWS
    r_hi = lax.dot_general(wr2_ref[...], h_hi, nt_dims, preferred_element_type=F32)
    r_lo = lax.dot_general(wr2_ref[0:nr, :], h_lo, nt_dims, preferred_element_type=F32)
    logits = r_hi[0:nr] + r_hi[nr:2 * nr] + r_lo + br_ref[:, :1]

    gl = logits[0:GROUP_ROWS]
    grow = lax.broadcasted_iota(jnp.int32, (GROUP_ROWS, blk), 0).astype(F32)
    gmax = jnp.max(gl, axis=0, keepdims=True)
    g_w = 1.0 / jnp.sum(jnp.exp(gl - gmax), axis=0, keepdims=True)
    g_idx = jnp.min(jnp.where(gl == gmax, grow, float(GROUP_ROWS)), axis=0, keepdims=True)

    el = logits[GROUP_ROWS:GROUP_ROWS + N_EXPERTS]
    erow_i = lax.broadcasted_iota(jnp.int32, (N_EXPERTS, blk), 0)
    erow = erow_i.astype(F32)
    el = jnp.where((erow_i >> 3).astype(F32) == g_idx, el, NEG)
    m1 = jnp.max(el, axis=0, keepdims=True)
    i1 = jnp.min(jnp.where(el == m1, erow, float(N_EXPERTS)), axis=0, keepdims=True)
    el2 = jnp.where(erow == i1, NEG, el)
    m2 = jnp.max(el2, axis=0, keepdims=True)
    i2 = jnp.min(jnp.where(el2 == m2, erow, float(N_EXPERTS)), axis=0, keepdims=True)
    t = jnp.exp(m2 - m1)
    w1 = g_w / (1.0 + t)
    w2 = g_w * t / (1.0 + t)

    s1 = erow == i1
    s2 = erow == i2
    onehot = jnp.where(s1 | s2, 1.0, 0.0)
    rank = _dot(onehot.astype(BF16), tri_ref[...])
    cnt = jnp.sum(onehot, axis=1, keepdims=True)
    units = jnp.floor((cnt + (ROW_ALIGN - 1.0)) * (1.0 / ROW_ALIGN))
    units = jnp.broadcast_to(units, (N_EXPERTS, LANES))
    off = float(ROW_ALIGN) * _dot(low_ref[...], units.astype(BF16))
    base = off[:, :1] + rank
    pos1 = jnp.sum(jnp.where(s1, base, 0.0), axis=0, keepdims=True)
    pos2 = jnp.sum(jnp.where(s2, base, 0.0), axis=0, keepdims=True)

    r8 = lax.broadcasted_iota(jnp.int32, (8, blk), 0)
    route_ref[...] = jnp.where(r8 == 0, pos1, jnp.where(r8 == 1, pos2, jnp.where(r8 == 2, w1,
                               jnp.where(r8 == 3, w2, 0.0))))
    off_ref[...] = off.astype(jnp.int32)
    tiles = jnp.floor((cnt + (EXP_TILE - 1.0)) * (1.0 / EXP_TILE))
    nt_ref[...] = jnp.broadcast_to(tiles, (N_EXPERTS, LANES)).astype(jnp.int32)


def _experts_kernel(off_sm, nt_sm, x_ref, h_ref, route_ref, gt_ref, wg_ref, wu_ref, wd_ref, fg_ref, o_ref,
                    p_scr, xs_scr, ys_scr, cw_scr, *, final_norm):
    i = pl.program_id(0)
    step = pl.program_id(1)
    nrows, blk = p_scr.shape

    @pl.when(step == 0)
    def _():
        pos1 = route_ref[0:1, :]
        pos2 = route_ref[1:2, :]
        w1 = route_ref[2:3, :]
        w2 = route_ref[3:4, :]
        row_iota = lax.broadcasted_iota(jnp.int32, (GATHER_CHUNK, blk), 0)

        def gather(c, carry):
            r0 = pl.multiple_of(c * GATHER_CHUNK, GATHER_CHUNK)
            prow = (row_iota + r0).astype(F32)
            m1 = prow == pos1
            m2 = prow == pos2
            perm = jnp.where(m1 | m2, 1.0, 0.0).astype(BF16)
            p_scr[pl.ds(r0, GATHER_CHUNK), :] = perm
            xs_scr[pl.ds(r0, GATHER_CHUNK), :] = _dot(perm, h_ref[...]).astype(BF16)
            cw = jnp.sum(jnp.where(m1, w1, 0.0) + jnp.where(m2, w2, 0.0), axis=1, keepdims=True)
            cw_scr[pl.ds(r0, GATHER_CHUNK), :] = jnp.broadcast_to(cw, (GATHER_CHUNK, LANES))
            ys_scr[pl.ds(r0, GATHER_CHUNK), :] = jnp.zeros((GATHER_CHUNK, D_MODEL), BF16)
            return carry

        lax.fori_loop(0, nrows // GATHER_CHUNK, gather, 0)

    def tile(g, s):
        s = pl.multiple_of(s, ROW_ALIGN)
        rows = xs_scr[pl.ds(s, EXP_TILE), :]
        gate = _dot(rows, wg_ref[g])
        up = _dot(rows, wu_ref[g])
        cw = cw_scr[pl.ds(s, EXP_TILE), :][:, :1]
        hid = (gate * _sigmoid(gate) * up * cw).astype(BF16)
        return s, _dot(hid, wd_ref[g]).astype(BF16)

    def put(s, y):
        ys_scr[pl.ds(s, EXP_TILE), :] = y

    starts = [off_sm[i, step * EXPERTS_PER_STEP + g] for g in range(EXPERTS_PER_STEP)]
    first = [tile(g, starts[g]) for g in range(EXPERTS_PER_STEP)]
    for g in range(EXPERTS_PER_STEP):
        put(*first[g])

        def more(t, carry, g=g):
            put(*tile(g, starts[g] + t * EXP_TILE))
            return carry

        lax.fori_loop(1, nt_sm[i, step * EXPERTS_PER_STEP + g], more, 0)

    @pl.when(step == N_EXPERTS // EXPERTS_PER_STEP - 1)
    def _():
        tn_dims = (((0,), (0,)), ((), ()))
        for r0 in range(0, blk, GATHER_CHUNK):
            rs = slice(r0, r0 + GATHER_CHUNK)
            moe = lax.dot_general(p_scr[:, rs], ys_scr[...], tn_dims, preferred_element_type=F32)
            gt = gt_ref[...] if gt_ref.shape[0] == 1 else gt_ref[rs, :]
            y = x_ref[rs, :] + gt * moe
            if final_norm:
                y = _rms(y) * fg_ref[...]
            o_ref[rs, :] = y


def _moe_call(x, mod, g2, wr2, br, w_exp, fg, *, layer, block, per_row, final_norm):
    rows = x.shape[0]
    nblk = rows // block
    nsorted = _sorted_rows(block)
    if per_row:
        mspec = lambda s: pl.BlockSpec((None, block, D_MODEL), lambda i, *_: (layer, i, s))
    else:
        blocks_per_seq = (rows // mod.shape[0]) // block
        mspec = lambda s: pl.BlockSpec((None, 1, D_MODEL), lambda i, *_: (i // blocks_per_seq, 0, s))
    const = lambda i: (0, 0)
    ridx = jnp.arange(block)
    tri = (ridx[:, None] < ridx[None, :]).astype(BF16)
    eidx = jnp.arange(N_EXPERTS)
    low = (eidx[None, :] < eidx[:, None]).astype(BF16)
    h, route, off, nt = pl.pallas_call(
        _route_kernel,
        grid=(nblk,),
        in_specs=[
            pl.BlockSpec((block, D_MODEL), lambda i: (i, 0)),
            mspec(3), mspec(4),
            pl.BlockSpec((1, D_MODEL), const),
            pl.BlockSpec((2 * ROUTER_ROWS, D_MODEL), const),
            pl.BlockSpec((ROUTER_ROWS, LANES), const),
            pl.BlockSpec((block, block), const),
            pl.BlockSpec((N_EXPERTS, N_EXPERTS), const),
        ],
        out_specs=[
            pl.BlockSpec((block, D_MODEL), lambda i: (i, 0)),
            pl.BlockSpec((None, 8, block), lambda i: (i, 0, 0)),
            pl.BlockSpec((None, N_EXPERTS, LANES), lambda i: (i, 0, 0)),
            pl.BlockSpec((None, N_EXPERTS, LANES), lambda i: (i, 0, 0)),
        ],
        out_shape=[
            jax.ShapeDtypeStruct((rows, D_MODEL), BF16),
            jax.ShapeDtypeStruct((nblk, 8, block), F32),
            jax.ShapeDtypeStruct((nblk, N_EXPERTS, LANES), jnp.int32),
            jax.ShapeDtypeStruct((nblk, N_EXPERTS, LANES), jnp.int32),
        ],
        compiler_params=pltpu.CompilerParams(vmem_limit_bytes=VMEM_LIMIT),
        name="moe_route",
    )(x, mod, mod, g2, wr2, br, tri, low)

    const2 = lambda i, e, *_: (0, 0)
    wmap = lambda i, s, *_: (s, 0, 0)
    return pl.pallas_call(
        functools.partial(_experts_kernel, final_norm=final_norm),
        grid_spec=pltpu.PrefetchScalarGridSpec(
            num_scalar_prefetch=2,
            grid=(nblk, N_EXPERTS // EXPERTS_PER_STEP),
            in_specs=[
                pl.BlockSpec((block, D_MODEL), lambda i, e, *_: (i, 0)),
                pl.BlockSpec((block, D_MODEL), lambda i, e, *_: (i, 0), pipeline_mode=pl.Buffered(1)),
                pl.BlockSpec((None, 8, block), lambda i, e, *_: (i, 0, 0)),
                mspec(5),
                pl.BlockSpec((EXPERTS_PER_STEP, D_MODEL, EXPERT_DIM), wmap),
                pl.BlockSpec((EXPERTS_PER_STEP, D_MODEL, EXPERT_DIM), wmap),
                pl.BlockSpec((EXPERTS_PER_STEP, EXPERT_DIM, D_MODEL), wmap),
                pl.BlockSpec((1, D_MODEL), const2),
            ],
            out_specs=pl.BlockSpec((block, D_MODEL), lambda i, e, *_: (i, 0)),
            scratch_shapes=[
                pltpu.VMEM((nsorted, block), BF16),
                pltpu.VMEM((nsorted, D_MODEL), BF16),
                pltpu.VMEM((nsorted, D_MODEL), BF16),
                pltpu.VMEM((nsorted, LANES), F32),
            ],
        ),
        out_shape=jax.ShapeDtypeStruct((rows, D_MODEL), F32),
        compiler_params=pltpu.CompilerParams(
            dimension_semantics=("arbitrary", "arbitrary"), vmem_limit_bytes=VMEM_LIMIT),
        name="moe_experts",
    )(off[:, :, 0], nt[:, :, 0], x, h, route, mod, *w_exp, fg)


def _rope_tables(pos):
    half = R_QK_DIM // 2
    inv_freq = jnp.power(ROPE_BASE, -jnp.arange(half, dtype=F32) / half)
    ang = pos.astype(F32)[:, None] * inv_freq[None, :]
    return jnp.cos(ang), jnp.sin(ang)


def _decay_tables(chunk, rows):
    log_g = jnp.log1p(-jnp.power(2.0, -5.0 - jnp.arange(R_HEADS, dtype=F32)))
    t1 = (jnp.arange(rows) % chunk).astype(F32) + 1.0
    dq = jnp.exp(log_g[None, :] * t1[:, None])
    dk = jnp.exp(-log_g[None, :] * t1[:, None]) * (R_QK_DIM ** -0.5)
    cd = jnp.exp(log_g * chunk)
    rep = lambda a: jnp.repeat(a, LANES, axis=1)
    return rep(dq), rep(dk), cd


def kernel(x_prompt, x_sample, state_ret, c_prompt, c_sample, w_mod, b_mod, norm1_g, w_in, ln_v_g, ln_v_b,
           w_s, b_s, w_a_out, w_b_out, w_o, norm2_g, w_router_group, b_router_group, w_router_expert,
           b_router_expert, w_gate, w_up, w_down, final_g):
    n_p, t_p, _ = x_prompt.shape
    n_s, t_s, _ = x_sample.shape
    assert t_p % CHUNK == 0 and (n_s * t_s) % CHUNK == 0 and CHUNK % t_s == 0
    chunks_p = t_p // CHUNK

    mod = _mod_call(jnp.concatenate([c_prompt, c_sample], axis=0), w_mod, b_mod)

    cos_p, sin_p = _rope_tables(jnp.arange(t_p, dtype=jnp.int32))
    cos_s, sin_s = _rope_tables(PAST_LEN + jnp.arange(IN_TILE_F32, dtype=jnp.int32) % t_s)
    dq_p, dk_p, cd_p = _decay_tables(CHUNK, IN_TILE_BF16)
    dq_s, dk_s, cd_s = _decay_tables(t_s, IN_TILE_F32)

    idx = jnp.arange(CHUNK)
    causal = idx[:, None] >= idx[None, :]
    same_seq = (idx[:, None] // t_s) == (idx[None, :] // t_s)
    mask_p = causal.astype(F32)
    mask_s = (causal & same_seq).astype(F32)

    xp = x_prompt.reshape(n_p * t_p, D_MODEL)
    xs = x_sample.reshape(n_s * t_s, D_MODEL)
    fg = final_g.reshape(1, D_MODEL)
    w_in_b = w_in.astype(BF16)
    mod_s = jnp.broadcast_to(mod[:, n_p:, None, :], (DEPTH, n_s, t_s, MOD_COLS))
    mod_s = mod_s.reshape(DEPTH, n_s * t_s, MOD_COLS)
    r_prompt, r_sample, v_sample = None, None, []
    for l in range(DEPTH):
        mod_p = mod[l, :n_p].reshape(n_p, 1, MOD_COLS)
        g1 = norm1_g[l].reshape(1, D_MODEL)
        g2 = norm2_g[l].reshape(1, D_MODEL)
        lng = ln_v_g[l].reshape(1, D_MODEL)
        lnb = ln_v_b[l].reshape(1, D_MODEL)

        ws_p = jnp.where(causal[None], w_s[l], 0.0).astype(BF16)
        bs_p = jnp.repeat(b_s[l].T, D_MODEL // A_GROUPS, axis=1)
        blk = jnp.where(causal[:t_s, :t_s][None], w_s[l][:, :t_s, :t_s], 0.0)
        ws_s = jnp.where(same_seq[None], jnp.tile(blk, (1, CHUNK // t_s, CHUNK // t_s)), 0.0).astype(BF16)
        bs_s = jnp.tile(bs_p[:t_s], (CHUNK // t_s, 1))

        zp, wa, wb, wo, *w_exp = _inproj_call(
            xp, mod_p, g1, cos_p, sin_p, dq_p, dk_p, lng, lnb, w_in_b, layer=l, per_row=False, z_dtype=BF16,
            cast=(w_a_out, w_b_out, w_o, w_gate, w_up, w_down))
        (zs,) = _inproj_call(xs, mod_s, g1, cos_s, sin_s, dq_s, dk_s, lng, lnb, w_in_b,
                             layer=l, per_row=True, z_dtype=F32)

        xp, r_prompt, cross, r_sample = _mix_prompt_call(
            cd_p, zp, xp, mod_p, ws_p, bs_p, mask_p, wa, wb, wo, cd_s, zs, state_ret,
            nseq=n_p, nchunk=chunks_p, t_s=t_s, layer=l, stacked_p=r_prompt, stacked_s=r_sample)
        xs = _mix_sample_call(cd_s, zs, xs, mod_s, ws_s, bs_s, mask_s, wa, wb, wo, cross, layer=l)

        tail = ROUTER_ROWS - GROUP_ROWS - N_EXPERTS
        wr = jnp.concatenate([jnp.pad(w_router_group[l].T, ((0, GROUP_ROWS - N_GROUPS), (0, 0))),
                              jnp.pad(w_router_expert[l].T, ((0, tail), (0, 0)))], axis=0)
        br = jnp.concatenate([jnp.pad(b_router_group[l], (0, GROUP_ROWS - N_GROUPS), constant_values=NEG),
                              jnp.pad(b_router_expert[l], (0, tail))])
        wr2 = jnp.concatenate(_split_bf16(wr), axis=0)
        br = jnp.broadcast_to(br[:, None], (ROUTER_ROWS, LANES))
        last = l == DEPTH - 1
        xp = _moe_call(xp, mod_p, g2, wr2, br, w_exp, fg,
                       layer=l, block=MOE_BLOCK, per_row=False, final_norm=last)
        xs = _moe_call(xs, mod_s, g2, wr2, br, w_exp, fg,
                       layer=l, block=n_s * t_s, per_row=True, final_norm=last)

        v_sample.append(zs[:, OFF_V:OFF_V + D_MODEL].reshape(n_s, t_s, D_MODEL))

    return (xp.reshape(n_p, t_p, D_MODEL), xs.reshape(n_s, t_s, D_MODEL),
            r_prompt, r_sample, jnp.stack(v_sample))
```

```python
import functools

import jax
import jax.numpy as jnp
from jax import lax
from jax.experimental import pallas as pl
from jax.experimental.pallas import tpu as pltpu

D_MODEL = 1024
DEPTH = 2
PAST_LEN = 16384
A_GROUPS = 8
CHUNK = 128
R_HEADS = 4
R_QK_DIM = 256
R_V_DIM = 512
ROPE_BASE = 10000.0
N_GROUPS = 4
EXPERTS_PER_GROUP = 8
N_EXPERTS = 32
EXPERT_DIM = 256
EPS = 1e-6
IN_COLS = 10 * D_MODEL
MOD_COLS = 6 * D_MODEL

OFF_U, OFF_V, OFF_Q, OFF_K, OFF_VR, OFF_GR, OFF_GA, OFF_GB = 0, 1024, 2048, 3072, 4096, 6144, 8192, 9216

LANES = 128
IN_TILE_BF16 = 256
IN_TILE_F32 = 256
MIX_SEQS = 2
MIX_CHUNKS_F32 = 2
MOE_BLOCK = 1024
EXP_TILE = 128
ROW_ALIGN = 16
GATHER_CHUNK = 256
GROUP_ROWS = 8
ROUTER_ROWS = 48
EXPERTS_PER_STEP = 4
MOD_TILE = 1536
VMEM_LIMIT = 56 * 1024 * 1024
NEG = float("-inf")

BF16 = jnp.bfloat16
F32 = jnp.float32


def _dot(a, b):
    return jnp.dot(a, b, preferred_element_type=F32)


def _sigmoid(x):
    return 1.0 / (1.0 + jnp.exp(-x))


def _gelu_tanh(x):
    return x * (0.5 * (1.0 + jnp.tanh(0.7978845608028654 * (x + 0.044715 * (x * x * x)))))


def _rms(x):
    return x * lax.rsqrt(jnp.mean(x * x, axis=-1, keepdims=True) + EPS)


def _split_bf16(w):
    hi = w.astype(BF16)
    return hi, (w - hi.astype(F32)).astype(BF16)


def _dot_split(a, b):
    a_hi, a_lo = _split_bf16(a)
    b_hi, b_lo = _split_bf16(b)
    return _dot(a_hi, b_hi) + (_dot(a_lo, b_hi) + _dot(a_hi, b_lo))


def _mod_kernel(c_ref, w_ref, b_ref, o_ref):
    c = c_ref[...]
    o_ref[...] = _dot_split(c * _sigmoid(c), w_ref[...]) + b_ref[...]


def _mod_call(c_all, w_mod, b_mod):
    n = c_all.shape[0]
    return pl.pallas_call(
        _mod_kernel,
        grid=(DEPTH, MOD_COLS // MOD_TILE),
        in_specs=[
            pl.BlockSpec((n, D_MODEL), lambda l, j: (0, 0)),
            pl.BlockSpec((None, D_MODEL, MOD_TILE), lambda l, j: (l, 0, j)),
            pl.BlockSpec((None, 1, MOD_TILE), lambda l, j: (l, 0, j)),
        ],
        out_specs=pl.BlockSpec((None, n, MOD_TILE), lambda l, j: (l, 0, j)),
        out_shape=jax.ShapeDtypeStruct((DEPTH, n, MOD_COLS), F32),
        compiler_params=pltpu.CompilerParams(vmem_limit_bytes=VMEM_LIMIT),
        name="adaln_mod",
    )(c_all, w_mod, b_mod.reshape(DEPTH, 1, MOD_COLS))


def _inproj_kernel(x_ref, sh_ref, sc_ref, g_ref, cos_ref, sin_ref, dq_ref, dk_ref, lng_ref, lnb_ref,
                   w_ref, *rest, n_cast, pack_parts):
    n_in = n_cast + (3 if pack_parts else 0)
    z_ref = rest[n_in]
    for src_ref, dst_ref in zip(rest[:n_cast], rest[n_in + 1:]):
        dst_ref[...] = src_ref[...].astype(dst_ref.dtype)
    if pack_parts:
        wg_ref, wu_ref, wd_ref = rest[n_cast:n_in]
        pack_ref = rest[-1]
        gr, dr, half = wg_ref.shape[0], wd_ref.shape[0], D_MODEL // 2
        part = pl.program_id(0) % pack_parts
        for q in range(pack_parts):
            @pl.when(part == q)
            def _(q=q):
                pack_ref[q * gr:(q + 1) * gr, 0:EXPERT_DIM] = wg_ref[...].astype(BF16)
                pack_ref[q * gr:(q + 1) * gr, EXPERT_DIM:] = wu_ref[...].astype(BF16)
                lo = D_MODEL + q * dr
                pack_ref[lo:lo + dr, :] = wd_ref[:, :half].astype(BF16)
                pack_ref[lo + EXPERT_DIM:lo + EXPERT_DIM + dr, :] = wd_ref[:, half:].astype(BF16)

    h = (_rms(x_ref[...]) * g_ref[...]) * (1.0 + sc_ref[...]) + sh_ref[...]
    h = h.astype(BF16)
    cos = cos_ref[...]
    sin = sin_ref[...]

    def proj(off, width=D_MODEL):
        return _dot(h, w_ref[:, off:off + width])

    def put(off, val):
        z_ref[:, off:off + val.shape[1]] = val.astype(z_ref.dtype)

    def plain(off):
        put(off, proj(off))

    def swish(off):
        g = proj(off)
        put(off, g * _sigmoid(g))

    def rotary(off, d_ref):
        half = R_QK_DIM // 2
        acc = proj(off)
        for hd in range(R_HEADS):
            x1 = acc[:, hd * R_QK_DIM:hd * R_QK_DIM + half]
            x2 = acc[:, hd * R_QK_DIM + half:(hd + 1) * R_QK_DIM]
            dec = d_ref[:, hd * LANES:(hd + 1) * LANES]
            put(off + hd * R_QK_DIM, (x1 * cos - x2 * sin) * dec)
            put(off + hd * R_QK_DIM + half, (x1 * sin + x2 * cos) * dec)

    put(OFF_U, _gelu_tanh(proj(OFF_U)))
    plain(OFF_VR)
    v = _gelu_tanh(proj(OFF_V))
    mu = jnp.mean(v, axis=-1, keepdims=True)
    vc = v - mu
    var = jnp.mean(vc * vc, axis=-1, keepdims=True)
    put(OFF_V, vc * lax.rsqrt(var + EPS) * lng_ref[...] + lnb_ref[...])
    plain(OFF_VR + D_MODEL)
    rotary(OFF_Q, dq_ref)
    put(OFF_GA, _sigmoid(proj(OFF_GA)))
    rotary(OFF_K, dk_ref)
    put(OFF_GB, _sigmoid(proj(OFF_GB)))
    swish(OFF_GR)
    swish(OFF_GR + D_MODEL)


def _inproj_call(x, mod, g1, cos, sin, dq, dk, lng, lnb, w_in, *, layer, per_row, z_dtype, cast=(),
                 experts=None):
    rows = x.shape[0]
    tile = dq.shape[0]
    steps = rows // tile
    cast_in, cast_specs_in, cast_specs_out, cast_shapes = [], [], [], []
    pack_parts = 0
    pack_in, pack_specs_in, pack_specs_out, pack_shapes = [], [], [], []
    if experts is not None:
        assert steps % N_EXPERTS == 0 and D_MODEL // 2 == 2 * EXPERT_DIM
        pack_parts = steps // N_EXPERTS
        gr, dr = D_MODEL // pack_parts, EXPERT_DIM // pack_parts
        assert gr % ROW_ALIGN == 0 and dr % ROW_ALIGN == 0
        part_map = lambda i: (layer, i // pack_parts, i % pack_parts, 0)
        pack_in = list(experts)
        pack_specs_in = [pl.BlockSpec((None, None, gr, EXPERT_DIM), part_map),
                         pl.BlockSpec((None, None, gr, EXPERT_DIM), part_map),
                         pl.BlockSpec((None, None, dr, D_MODEL), part_map)]
        packed = (N_EXPERTS, D_MODEL + 2 * EXPERT_DIM, 2 * EXPERT_DIM)
        pack_specs_out = [pl.BlockSpec((None,) + packed[1:], lambda i: (i // pack_parts, 0, 0))]
        pack_shapes = [jax.ShapeDtypeStruct(packed, BF16)]
    for w in cast:
        cols = w.shape[-1]
        flat = w.reshape(w.shape[0], -1, cols)
        per = flat.shape[1] // steps
        assert flat.shape[1] % steps == 0 and per % ROW_ALIGN == 0
        cast_in.append(flat)
        cast_specs_in.append(pl.BlockSpec((None, per, cols), lambda i: (layer, i, 0)))
        cast_specs_out.append(pl.BlockSpec((per, cols), lambda i: (i, 0)))
        cast_shapes.append(jax.ShapeDtypeStruct(flat.shape[1:], BF16))
    tiles_per_seq = None if per_row else (rows // mod.shape[0]) // tile
    if per_row:
        sh_spec = pl.BlockSpec((None, tile, D_MODEL), lambda i: (layer, i, 0))
        sc_spec = pl.BlockSpec((None, tile, D_MODEL), lambda i: (layer, i, 1))
        rope_spec = pl.BlockSpec((tile, LANES), lambda i: (0, 0))
    else:
        sh_spec = pl.BlockSpec((None, 1, D_MODEL), lambda i: (i // tiles_per_seq, 0, 0))
        sc_spec = pl.BlockSpec((None, 1, D_MODEL), lambda i: (i // tiles_per_seq, 0, 1))
        rope_spec = pl.BlockSpec((tile, LANES), lambda i: (i % tiles_per_seq, 0))
    const = lambda i: (0, 0)
    z, *casted = pl.pallas_call(
        functools.partial(_inproj_kernel, n_cast=len(cast), pack_parts=pack_parts),
        grid=(steps,),
        in_specs=[
            pl.BlockSpec((tile, D_MODEL), lambda i: (i, 0)),
            sh_spec, sc_spec,
            pl.BlockSpec((1, D_MODEL), const),
            rope_spec, rope_spec,
            pl.BlockSpec((tile, R_HEADS * LANES), const),
            pl.BlockSpec((tile, R_HEADS * LANES), const),
            pl.BlockSpec((1, D_MODEL), const),
            pl.BlockSpec((1, D_MODEL), const),
            pl.BlockSpec((None, D_MODEL, IN_COLS), lambda i: (layer, 0, 0), pipeline_mode=pl.Buffered(1)),
            *cast_specs_in, *pack_specs_in,
        ],
        out_specs=[pl.BlockSpec((tile, IN_COLS), lambda i: (i, 0)), *cast_specs_out, *pack_specs_out],
        out_shape=[jax.ShapeDtypeStruct((rows, IN_COLS), z_dtype), *cast_shapes, *pack_shapes],
        compiler_params=pltpu.CompilerParams(vmem_limit_bytes=VMEM_LIMIT),
        name="in_proj",
    )(x, mod, mod, g1, cos, sin, dq, dk, lng, lnb, w_in, *cast_in, *pack_in)
    return (z, *[c.reshape(w.shape[1:]) for c, w in zip(casted, cast)], *casted[len(cast):])


def _carried_state_step(cd_ref, z_q_ref, z_k_ref, z_v_ref, r0_ref, cross_ref, r_ref, t_s):
    nrows = z_q_ref.shape[0]
    q = z_q_ref[...].astype(BF16)
    k = z_k_ref[...]
    v = z_v_ref[...].astype(BF16)
    row_k = lax.broadcasted_iota(jnp.int32, (nrows, R_QK_DIM), 0)
    row_v = lax.broadcasted_iota(jnp.int32, (nrows, R_V_DIM), 0)
    for hd in range(R_HEADS):
        qh = q[:, hd * R_QK_DIM:(hd + 1) * R_QK_DIM]
        kh = k[:, hd * R_QK_DIM:(hd + 1) * R_QK_DIM]
        vh = v[:, hd * R_V_DIM:(hd + 1) * R_V_DIM]
        cross = jnp.zeros((nrows, R_V_DIM), F32)
        for s in range(nrows // t_s):
            r = r0_ref[s, hd]
            lo, hi = s * t_s, (s + 1) * t_s
            cross = jnp.where((row_v >= lo) & (row_v < hi), _dot(qh, r.astype(BF16)), cross)
            k_seq = jnp.where((row_k >= lo) & (row_k < hi), kh, 0.0).astype(BF16)
            upd = lax.dot_general(k_seq, vh, (((0,), (0,)), ((), ())), preferred_element_type=F32)
            r_ref[s, hd] = cd_ref[hd] * (r + upd)
        cross_ref[:, hd * R_V_DIM:(hd + 1) * R_V_DIM] = cross


def _mix_sample_kernel(cd_ref, z_ref, x_ref, gt_ref, ws_ref, bs_ref, mask_ref, wa_ref, wb_ref, wo_ref,
                       cross_ref, xo_ref):
    _mix_step(cd_ref, z_ref, x_ref, gt_ref, ws_ref, bs_ref, mask_ref, wa_ref, wb_ref, wo_ref, xo_ref,
              cross_ref=cross_ref)


def _mix_prompt_kernel(cd_ref, z_ref, x_ref, gt_ref, ws_ref, bs_ref, mask_ref, wa_ref, wb_ref, wo_ref,
                       cds_ref, zq_ref, zk_ref, zv_ref, r0_ref, *rest, t_s):
    xo_ref, r_ref, cross_ref, rs_ref = rest[-4:]
    _mix_step(cd_ref, z_ref, x_ref, gt_ref, ws_ref, bs_ref, mask_ref, wa_ref, wb_ref, wo_ref, xo_ref,
              r_ref=r_ref)
    _carried_state_step(cds_ref, zq_ref, zk_ref, zv_ref, r0_ref, cross_ref, rs_ref, t_s)


def _mix_step(cd_ref, z_ref, x_ref, gt_ref, ws_ref, bs_ref, mask_ref, wa_ref, wb_ref, wo_ref, xo_ref, *,
              cross_ref=None, r_ref=None):
    nsb, nrows = z_ref.shape[:2]
    per_sb = nrows // CHUNK
    nseq = nsb * per_sb
    carried_state = cross_ref is not None
    if not carried_state:
        @pl.when(pl.program_id(1) == 0)
        def _():
            r_ref[...] = jnp.zeros_like(r_ref)

    mask = mask_ref[...]
    gdim = D_MODEL // A_GROUPS
    a_rows, b_rows = [], []
    for sq in range(nseq):
        sb, chunk = divmod(sq, per_sb)
        rs = slice(chunk * CHUNK, (chunk + 1) * CHUNK)

        def sec(off, width, sb=sb, rs=rs):
            return z_ref[sb, rs, off:off + width]

        v = sec(OFF_V, D_MODEL).astype(BF16)
        mixed = jnp.concatenate(
            [_dot(ws_ref[g], v[:, g * gdim:(g + 1) * gdim]) for g in range(A_GROUPS)], axis=1)
        a_rows.append((sec(OFF_U, D_MODEL).astype(F32) * (mixed + bs_ref[...])).astype(BF16))

        b_parts = []
        for hd in range(R_HEADS):
            qh = sec(OFF_Q + hd * R_QK_DIM, R_QK_DIM).astype(BF16)
            kh = sec(OFF_K + hd * R_QK_DIM, R_QK_DIM).astype(BF16)
            vh = sec(OFF_VR + hd * R_V_DIM, R_V_DIM).astype(BF16)
            s = lax.dot_general(qh, kh, (((1,), (1,)), ((), ())), preferred_element_type=F32) * mask
            y = _dot(s.astype(BF16), vh)
            if carried_state:
                y = y + cross_ref[sb, rs, hd * R_V_DIM:(hd + 1) * R_V_DIM]
            else:
                r = r_ref[sq, hd]
                y = y + _dot(qh, r.astype(BF16))
                upd = lax.dot_general(kh, vh, (((0,), (0,)), ((), ())), preferred_element_type=F32)
                r_ref[sq, hd] = cd_ref[hd] * (r + upd)
            mu = jnp.mean(y, axis=-1, keepdims=True)
            yc = y - mu
            var = jnp.mean(yc * yc, axis=-1, keepdims=True)
            yn = yc * lax.rsqrt(var + EPS)
            b_parts.append((sec(OFF_GR + hd * R_V_DIM, R_V_DIM).astype(F32) * yn).astype(BF16))
        b_rows.append(jnp.concatenate(b_parts, axis=1))

    pa = _dot(jnp.concatenate(a_rows, axis=0), wa_ref[...])
    pb = _dot(jnp.concatenate(b_rows, axis=0), wb_ref[...])
    ga = z_ref[:, :, OFF_GA:OFF_GA + D_MODEL].reshape(nseq * CHUNK, D_MODEL).astype(F32)
    gb = z_ref[:, :, OFF_GB:OFF_GB + D_MODEL].reshape(nseq * CHUNK, D_MODEL).astype(F32)
    mix = _dot((ga * pa + gb * pb).astype(BF16), wo_ref[...])
    xo_ref[...] = x_ref[...] + gt_ref[...] * mix.reshape(nsb, nrows, D_MODEL)


def _mix_specs(sb, rb, gt_spec):
    row_map = lambda b, c: (b, c, 0)
    const2 = lambda b, c: (0, 0)
    return [
        pl.BlockSpec(memory_space=pltpu.SMEM),
        pl.BlockSpec((sb, rb, IN_COLS), row_map),
        pl.BlockSpec((sb, rb, D_MODEL), row_map),
        gt_spec,
        pl.BlockSpec((A_GROUPS, CHUNK, CHUNK), lambda b, c: (0, 0, 0)),
        pl.BlockSpec((CHUNK, D_MODEL), const2),
        pl.BlockSpec((CHUNK, CHUNK), const2),
        pl.BlockSpec((D_MODEL, D_MODEL), const2, pipeline_mode=pl.Buffered(1)),
        pl.BlockSpec((R_HEADS * R_V_DIM, D_MODEL), const2, pipeline_mode=pl.Buffered(1)),
        pl.BlockSpec((D_MODEL, D_MODEL), const2, pipeline_mode=pl.Buffered(1)),
    ]


def _mix_sample_call(cd, z, x, mod, ws, bs, mask, wa, wb, wo, cross, *, layer):
    rows = x.shape[0]
    rb = MIX_CHUNKS_F32 * CHUNK
    assert rows % rb == 0
    view = lambda a: a.reshape(1, rows, a.shape[-1])
    row_map = lambda b, c: (b, c, 0)
    gt_spec = pl.BlockSpec((None, rb, D_MODEL), lambda b, c: (layer, c, 2))
    in_specs = _mix_specs(1, rb, gt_spec)
    in_specs.append(pl.BlockSpec((1, rb, R_HEADS * R_V_DIM), row_map))
    out = pl.pallas_call(
        _mix_sample_kernel,
        grid=(1, rows // rb),
        in_specs=in_specs,
        out_specs=pl.BlockSpec((1, rb, D_MODEL), row_map),
        out_shape=jax.ShapeDtypeStruct((1, rows, D_MODEL), F32),
        compiler_params=pltpu.CompilerParams(
            dimension_semantics=("arbitrary", "arbitrary"), vmem_limit_bytes=VMEM_LIMIT),
        name="token_mix_sample",
    )(cd, view(z), view(x), mod, ws, bs, mask, wa, wb, wo,
      view(cross))
    return out.reshape(rows, D_MODEL)


def _mix_prompt_call(cd, z, x, mod, ws, bs, mask, wa, wb, wo, cd_s, z_s, state_s, *, nseq, nchunk, t_s,
                     layer, stacked_p=None, stacked_s=None):
    rows = x.shape[0]
    seq_len = nchunk * CHUNK
    steps = (nseq // MIX_SEQS) * nchunk
    rows_s = z_s.shape[0]
    ret_rows = rows_s // steps
    assert nseq % MIX_SEQS == 0 and rows_s % steps == 0 and ret_rows % t_s == 0 and ret_rows % 8 == 0
    seqs_s = ret_rows // t_s
    qk_w = R_HEADS * R_QK_DIM
    v_w = R_HEADS * R_V_DIM
    view = lambda a: a.reshape(nseq, seq_len, a.shape[-1])
    row_map = lambda b, c: (b, c, 0)
    step_of = lambda b, c: b * nchunk + c
    gt_spec = pl.BlockSpec((MIX_SEQS, 1, D_MODEL), lambda b, c: (b, 0, 2))
    state_p_spec = pl.BlockSpec((None, MIX_SEQS, R_HEADS, R_QK_DIM, R_V_DIM), lambda b, c: (layer, b, 0, 0, 0))
    state_s_spec = pl.BlockSpec((None, seqs_s, R_HEADS, R_QK_DIM, R_V_DIM),
                                lambda b, c: (layer, step_of(b, c), 0, 0, 0))
    in_specs = _mix_specs(MIX_SEQS, CHUNK, gt_spec) + [
        pl.BlockSpec(memory_space=pltpu.SMEM),
        pl.BlockSpec((ret_rows, qk_w), lambda b, c: (step_of(b, c), OFF_Q // qk_w)),
        pl.BlockSpec((ret_rows, qk_w), lambda b, c: (step_of(b, c), OFF_K // qk_w)),
        pl.BlockSpec((ret_rows, v_w), lambda b, c: (step_of(b, c), OFF_VR // v_w)),
        state_s_spec,
    ]
    args = [cd, view(z), view(x), mod, ws, bs, mask, wa, wb, wo, cd_s, z_s, z_s, z_s, state_s]
    aliases = {}
    for buf, out_idx in ((stacked_p, 1), (stacked_s, 3)):
        if buf is not None:
            aliases[len(args)] = out_idx
            in_specs.append(pl.BlockSpec(memory_space=pl.ANY))
            args.append(buf)
    x_out, r_p, cross, r_s = pl.pallas_call(
        functools.partial(_mix_prompt_kernel, t_s=t_s),
        grid=(nseq // MIX_SEQS, nchunk),
        in_specs=in_specs,
        out_specs=[
            pl.BlockSpec((MIX_SEQS, CHUNK, D_MODEL), row_map),
            state_p_spec,
            pl.BlockSpec((ret_rows, v_w), lambda b, c: (step_of(b, c), 0)),
            state_s_spec,
        ],
        out_shape=[
            jax.ShapeDtypeStruct((nseq, seq_len, D_MODEL), F32),
            jax.ShapeDtypeStruct((DEPTH, nseq, R_HEADS, R_QK_DIM, R_V_DIM), F32),
            jax.ShapeDtypeStruct((rows_s, v_w), F32),
            jax.ShapeDtypeStruct(state_s.shape, F32),
        ],
        input_output_aliases=aliases,
        compiler_params=pltpu.CompilerParams(
            dimension_semantics=("arbitrary", "arbitrary"), vmem_limit_bytes=VMEM_LIMIT),
        name="token_mix_prompt",
    )(*args)
    return x_out.reshape(rows, D_MODEL), r_p, cross, r_s


def _sorted_rows(block):
    need = 2 * block + N_EXPERTS * ROW_ALIGN + EXP_TILE
    return -(-need // GATHER_CHUNK) * GATHER_CHUNK


def _route_kernel(x_ref, sh_ref, sc_ref, g_ref, wr2_ref, br_ref, tri_ref, low_ref,
                  h_ref, route_ref, off_ref, nt_ref):
    blk = x_ref.shape[0]
    h = (_rms(x_ref[...]) * g_ref[...]) * (1.0 + sc_ref[...]) + sh_ref[...]
    h_hi, h_lo = _split_bf16(h)
    h_ref[...] = h_hi

    nt_dims = (((1,), (1,)), ((), ()))
    nr = ROUTER_ROWS
    r_hi = lax.dot_general(wr2_ref[...], h_hi, nt_dims, preferred_element_type=F32)
    r_lo = lax.dot_general(wr2_ref[0:nr, :], h_lo, nt_dims, preferred_element_type=F32)
    logits = r_hi[0:nr] + r_hi[nr:2 * nr] + r_lo + br_ref[:, :1]

    gl = logits[0:GROUP_ROWS]
    grow = lax.broadcasted_iota(jnp.int32, (GROUP_ROWS, blk), 0).astype(F32)
    gmax = jnp.max(gl, axis=0, keepdims=True)
    g_w = 1.0 / jnp.sum(jnp.exp(gl - gmax), axis=0, keepdims=True)
    g_idx = jnp.min(jnp.where(gl == gmax, grow, float(GROUP_ROWS)), axis=0, keepdims=True)

    el = logits[GROUP_ROWS:GROUP_ROWS + N_EXPERTS]
    erow_i = lax.broadcasted_iota(jnp.int32, (N_EXPERTS, blk), 0)
    erow = erow_i.astype(F32)
    el = jnp.where((erow_i >> 3).astype(F32) == g_idx, el, NEG)
    m1 = jnp.max(el, axis=0, keepdims=True)
    i1 = jnp.min(jnp.where(el == m1, erow, float(N_EXPERTS)), axis=0, keepdims=True)
    el2 = jnp.where(erow == i1, NEG, el)
    m2 = jnp.max(el2, axis=0, keepdims=True)
    i2 = jnp.min(jnp.where(el2 == m2, erow, float(N_EXPERTS)), axis=0, keepdims=True)
    t = jnp.exp(m2 - m1)
    w1 = g_w / (1.0 + t)
    w2 = g_w * t / (1.0 + t)

    s1 = erow == i1
    s2 = erow == i2
    onehot = jnp.where(s1 | s2, 1.0, 0.0)
    rank = _dot(onehot.astype(BF16), tri_ref[...])
    cnt = jnp.sum(onehot, axis=1, keepdims=True)
    units = jnp.floor((cnt + (ROW_ALIGN - 1.0)) * (1.0 / ROW_ALIGN))
    units = jnp.broadcast_to(units, (N_EXPERTS, LANES))
    off = float(ROW_ALIGN) * _dot(low_ref[...], units.astype(BF16))
    base = off[:, :1] + rank
    pos1 = jnp.sum(jnp.where(s1, base, 0.0), axis=0, keepdims=True)
    pos2 = jnp.sum(jnp.where(s2, base, 0.0), axis=0, keepdims=True)

    r8 = lax.broadcasted_iota(jnp.int32, (8, blk), 0)
    route_ref[...] = jnp.where(r8 == 0, pos1, jnp.where(r8 == 1, pos2, jnp.where(r8 == 2, w1,
                               jnp.where(r8 == 3, w2, 0.0))))
    off_ref[...] = off.astype(jnp.int32)
    tiles = jnp.floor((cnt + (EXP_TILE - 1.0)) * (1.0 / EXP_TILE))
    nt_ref[...] = jnp.broadcast_to(tiles, (N_EXPERTS, LANES)).astype(jnp.int32)


def _experts_kernel(off_sm, nt_sm, x_ref, h_ref, route_ref, gt_ref, w_ref, fg_ref, o_ref,
                    p_scr, xs_scr, ys_scr, cw_scr, *, final_norm):
    i = pl.program_id(0)
    step = pl.program_id(1)
    nrows, blk = p_scr.shape

    @pl.when(step == 0)
    def _():
        pos1 = route_ref[0:1, :]
        pos2 = route_ref[1:2, :]
        w1 = route_ref[2:3, :]
        w2 = route_ref[3:4, :]
        row_iota = lax.broadcasted_iota(jnp.int32, (GATHER_CHUNK, blk), 0)

        def gather(c, carry):
            r0 = pl.multiple_of(c * GATHER_CHUNK, GATHER_CHUNK)
            prow = (row_iota + r0).astype(F32)
            m1 = prow == pos1
            m2 = prow == pos2
            perm = jnp.where(m1 | m2, 1.0, 0.0).astype(BF16)
            p_scr[pl.ds(r0, GATHER_CHUNK), :] = perm
            xs_scr[pl.ds(r0, GATHER_CHUNK), :] = _dot(perm, h_ref[...]).astype(BF16)
            cw = jnp.sum(jnp.where(m1, w1, 0.0) + jnp.where(m2, w2, 0.0), axis=1, keepdims=True)
            cw_scr[pl.ds(r0, GATHER_CHUNK), :] = jnp.broadcast_to(cw, (GATHER_CHUNK, LANES))
            ys_scr[pl.ds(r0, GATHER_CHUNK), :] = jnp.zeros((GATHER_CHUNK, D_MODEL), BF16)
            return carry

        lax.fori_loop(0, nrows // GATHER_CHUNK, gather, 0)

    half = D_MODEL // 2

    def tile(g, s):
        s = pl.multiple_of(s, ROW_ALIGN)
        rows = xs_scr[pl.ds(s, EXP_TILE), :]
        gu = _dot(rows, w_ref[g, 0:D_MODEL, :])
        gate = gu[:, :EXPERT_DIM]
        up = gu[:, EXPERT_DIM:]
        cw = cw_scr[pl.ds(s, EXP_TILE), :][:, :1]
        hid = (gate * _sigmoid(gate) * up * cw).astype(BF16)
        lo = _dot(hid, w_ref[g, D_MODEL:D_MODEL + EXPERT_DIM, :]).astype(BF16)
        hi = _dot(hid, w_ref[g, D_MODEL + EXPERT_DIM:, :]).astype(BF16)
        return s, lo, hi

    def put(s, lo, hi):
        ys_scr[pl.ds(s, EXP_TILE), 0:half] = lo
        ys_scr[pl.ds(s, EXP_TILE), half:D_MODEL] = hi

    starts = [off_sm[i, step * EXPERTS_PER_STEP + g] for g in range(EXPERTS_PER_STEP)]
    first = [tile(g, starts[g]) for g in range(EXPERTS_PER_STEP)]
    for g in range(EXPERTS_PER_STEP):
        put(*first[g])

        def more(t, carry, g=g):
            put(*tile(g, starts[g] + t * EXP_TILE))
            return carry

        lax.fori_loop(1, nt_sm[i, step * EXPERTS_PER_STEP + g], more, 0)

    @pl.when(step == N_EXPERTS // EXPERTS_PER_STEP - 1)
    def _():
        tn_dims = (((0,), (0,)), ((), ()))
        for r0 in range(0, blk, GATHER_CHUNK):
            rs = slice(r0, r0 + GATHER_CHUNK)
            moe = lax.dot_general(p_scr[:, rs], ys_scr[...], tn_dims, preferred_element_type=F32)
            gt = gt_ref[...] if gt_ref.shape[0] == 1 else gt_ref[rs, :]
            y = x_ref[rs, :] + gt * moe
            if final_norm:
                y = _rms(y) * fg_ref[...]
            o_ref[rs, :] = y


def _moe_call(x, mod, g2, wr2, br, w_exp, fg, *, layer, block, per_row, final_norm):
    rows = x.shape[0]
    nblk = rows // block
    nsorted = _sorted_rows(block)
    if per_row:
        mspec = lambda s: pl.BlockSpec((None, block, D_MODEL), lambda i, *_: (layer, i, s))
    else:
        blocks_per_seq = (rows // mod.shape[0]) // block
        mspec = lambda s: pl.BlockSpec((None, 1, D_MODEL), lambda i, *_: (i // blocks_per_seq, 0, s))
    const = lambda i: (0, 0)
    ridx = jnp.arange(block)
    tri = (ridx[:, None] < ridx[None, :]).astype(BF16)
    eidx = jnp.arange(N_EXPERTS)
    low = (eidx[None, :] < eidx[:, None]).astype(BF16)
    h, route, off, nt = pl.pallas_call(
        _route_kernel,
        grid=(nblk,),
        in_specs=[
            pl.BlockSpec((block, D_MODEL), lambda i: (i, 0)),
            mspec(3), mspec(4),
            pl.BlockSpec((1, D_MODEL), const),
            pl.BlockSpec((2 * ROUTER_ROWS, D_MODEL), const),
            pl.BlockSpec((ROUTER_ROWS, LANES), const),
            pl.BlockSpec((block, block), const),
            pl.BlockSpec((N_EXPERTS, N_EXPERTS), const),
        ],
        out_specs=[
            pl.BlockSpec((block, D_MODEL), lambda i: (i, 0)),
            pl.BlockSpec((None, 8, block), lambda i: (i, 0, 0)),
            pl.BlockSpec((None, N_EXPERTS, LANES), lambda i: (i, 0, 0)),
            pl.BlockSpec((None, N_EXPERTS, LANES), lambda i: (i, 0, 0)),
        ],
        out_shape=[
            jax.ShapeDtypeStruct((rows, D_MODEL), BF16),
            jax.ShapeDtypeStruct((nblk, 8, block), F32),
            jax.ShapeDtypeStruct((nblk, N_EXPERTS, LANES), jnp.int32),
            jax.ShapeDtypeStruct((nblk, N_EXPERTS, LANES), jnp.int32),
        ],
        compiler_params=pltpu.CompilerParams(vmem_limit_bytes=VMEM_LIMIT),
        name="moe_route",
    )(x, mod, mod, g2, wr2, br, tri, low)

    const2 = lambda i, e, *_: (0, 0)
    wmap = lambda i, s, *_: (s, 0, 0)
    return pl.pallas_call(
        functools.partial(_experts_kernel, final_norm=final_norm),
        grid_spec=pltpu.PrefetchScalarGridSpec(
            num_scalar_prefetch=2,
            grid=(nblk, N_EXPERTS // EXPERTS_PER_STEP),
            in_specs=[
                pl.BlockSpec((block, D_MODEL), lambda i, e, *_: (i, 0)),
                pl.BlockSpec((block, D_MODEL), lambda i, e, *_: (i, 0), pipeline_mode=pl.Buffered(1)),
                pl.BlockSpec((None, 8, block), lambda i, e, *_: (i, 0, 0)),
                mspec(5),
                pl.BlockSpec((EXPERTS_PER_STEP, D_MODEL + 2 * EXPERT_DIM, 2 * EXPERT_DIM), wmap),
                pl.BlockSpec((1, D_MODEL), const2),
            ],
            out_specs=pl.BlockSpec((block, D_MODEL), lambda i, e, *_: (i, 0)),
            scratch_shapes=[
                pltpu.VMEM((nsorted, block), BF16),
                pltpu.VMEM((nsorted, D_MODEL), BF16),
                pltpu.VMEM((nsorted, D_MODEL), BF16),
                pltpu.VMEM((nsorted, LANES), F32),
            ],
        ),
        out_shape=jax.ShapeDtypeStruct((rows, D_MODEL), F32),
        compiler_params=pltpu.CompilerParams(
            dimension_semantics=("arbitrary", "arbitrary"), vmem_limit_bytes=VMEM_LIMIT),
        name="moe_experts",
    )(off[:, :, 0], nt[:, :, 0], x, h, route, mod, w_exp, fg)


def _rope_tables(pos):
    half = R_QK_DIM // 2
    inv_freq = jnp.power(ROPE_BASE, -jnp.arange(half, dtype=F32) / half)
    ang = pos.astype(F32)[:, None] * inv_freq[None, :]
    return jnp.cos(ang), jnp.sin(ang)


def _decay_tables(chunk, rows):
    log_g = jnp.log1p(-jnp.power(2.0, -5.0 - jnp.arange(R_HEADS, dtype=F32)))
    t1 = (jnp.arange(rows) % chunk).astype(F32) + 1.0
    dq = jnp.exp(log_g[None, :] * t1[:, None])
    dk = jnp.exp(-log_g[None, :] * t1[:, None]) * (R_QK_DIM ** -0.5)
    cd = jnp.exp(log_g * chunk)
    rep = lambda a: jnp.repeat(a, LANES, axis=1)
    return rep(dq), rep(dk), cd


def kernel(x_prompt, x_sample, state_ret, c_prompt, c_sample, w_mod, b_mod, norm1_g, w_in, ln_v_g, ln_v_b,
           w_s, b_s, w_a_out, w_b_out, w_o, norm2_g, w_router_group, b_router_group, w_router_expert,
           b_router_expert, w_gate, w_up, w_down, final_g):
    n_p, t_p, _ = x_prompt.shape
    n_s, t_s, _ = x_sample.shape
    assert t_p % CHUNK == 0 and (n_s * t_s) % CHUNK == 0 and CHUNK % t_s == 0
    chunks_p = t_p // CHUNK

    mod = _mod_call(jnp.concatenate([c_prompt, c_sample], axis=0), w_mod, b_mod)

    cos_p, sin_p = _rope_tables(jnp.arange(t_p, dtype=jnp.int32))
    cos_s, sin_s = _rope_tables(PAST_LEN + jnp.arange(IN_TILE_F32, dtype=jnp.int32) % t_s)
    dq_p, dk_p, cd_p = _decay_tables(CHUNK, IN_TILE_BF16)
    dq_s, dk_s, cd_s = _decay_tables(t_s, IN_TILE_F32)

    idx = jnp.arange(CHUNK)
    causal = idx[:, None] >= idx[None, :]
    same_seq = (idx[:, None] // t_s) == (idx[None, :] // t_s)
    mask_p = causal.astype(F32)
    mask_s = (causal & same_seq).astype(F32)

    xp = x_prompt.reshape(n_p * t_p, D_MODEL)
    xs = x_sample.reshape(n_s * t_s, D_MODEL)
    fg = final_g.reshape(1, D_MODEL)
    w_in_b = w_in.astype(BF16)
    mod_s = jnp.broadcast_to(mod[:, n_p:, None, :], (DEPTH, n_s, t_s, MOD_COLS))
    mod_s = mod_s.reshape(DEPTH, n_s * t_s, MOD_COLS)
    r_prompt, r_sample, v_sample = None, None, []
    for l in range(DEPTH):
        mod_p = mod[l, :n_p].reshape(n_p, 1, MOD_COLS)
        g1 = norm1_g[l].reshape(1, D_MODEL)
        g2 = norm2_g[l].reshape(1, D_MODEL)
        lng = ln_v_g[l].reshape(1, D_MODEL)
        lnb = ln_v_b[l].reshape(1, D_MODEL)

        ws_p = jnp.where(causal[None], w_s[l], 0.0).astype(BF16)
        bs_p = jnp.repeat(b_s[l].T, D_MODEL // A_GROUPS, axis=1)
        blk = jnp.where(causal[:t_s, :t_s][None], w_s[l][:, :t_s, :t_s], 0.0)
        ws_s = jnp.where(same_seq[None], jnp.tile(blk, (1, CHUNK // t_s, CHUNK // t_s)), 0.0).astype(BF16)
        bs_s = jnp.tile(bs_p[:t_s], (CHUNK // t_s, 1))

        zp, wa, wb, wo, w_exp = _inproj_call(
            xp, mod_p, g1, cos_p, sin_p, dq_p, dk_p, lng, lnb, w_in_b, layer=l, per_row=False, z_dtype=BF16,
            cast=(w_a_out, w_b_out, w_o), experts=(w_gate, w_up, w_down))
        (zs,) = _inproj_call(xs, mod_s, g1, cos_s, sin_s, dq_s, dk_s, lng, lnb, w_in_b,
                             layer=l, per_row=True, z_dtype=F32)

        xp, r_prompt, cross, r_sample = _mix_prompt_call(
            cd_p, zp, xp, mod_p, ws_p, bs_p, mask_p, wa, wb, wo, cd_s, zs, state_ret,
            nseq=n_p, nchunk=chunks_p, t_s=t_s, layer=l, stacked_p=r_prompt, stacked_s=r_sample)
        xs = _mix_sample_call(cd_s, zs, xs, mod_s, ws_s, bs_s, mask_s, wa, wb, wo, cross, layer=l)

        tail = ROUTER_ROWS - GROUP_ROWS - N_EXPERTS
        wr = jnp.concatenate([jnp.pad(w_router_group[l].T, ((0, GROUP_ROWS - N_GROUPS), (0, 0))),
                              jnp.pad(w_router_expert[l].T, ((0, tail), (0, 0)))], axis=0)
        br = jnp.concatenate([jnp.pad(b_router_group[l], (0, GROUP_ROWS - N_GROUPS), constant_values=NEG),
                              jnp.pad(b_router_expert[l], (0, tail))])
        wr2 = jnp.concatenate(_split_bf16(wr), axis=0)
        br = jnp.broadcast_to(br[:, None], (ROUTER_ROWS, LANES))
        last = l == DEPTH - 1
        xp = _moe_call(xp, mod_p, g2, wr2, br, w_exp, fg,
                       layer=l, block=MOE_BLOCK, per_row=False, final_norm=last)
        xs = _moe_call(xs, mod_s, g2, wr2, br, w_exp, fg,
                       layer=l, block=n_s * t_s, per_row=True, final_norm=last)

        v_sample.append(zs[:, OFF_V:OFF_V + D_MODEL].reshape(n_s, t_s, D_MODEL))

    return (xp.reshape(n_p, t_p, D_MODEL), xs.reshape(n_s, t_s, D_MODEL),
            r_prompt, r_sample, jnp.stack(v_sample))
```

```python
import functools

import jax
import jax.numpy as jnp
from jax import lax
from jax.experimental import pallas as pl
from jax.experimental.pallas import tpu as pltpu

D_MODEL = 1024
DEPTH = 2
PAST_LEN = 16384
A_GROUPS = 8
CHUNK = 128
R_HEADS = 4
R_QK_DIM = 256
R_V_DIM = 512
ROPE_BASE = 10000.0
N_GROUPS = 4
EXPERTS_PER_GROUP = 8
N_EXPERTS = 32
EXPERT_DIM = 256
EPS = 1e-6
IN_COLS = 10 * D_MODEL
MOD_COLS = 6 * D_MODEL

OFF_U, OFF_V, OFF_Q, OFF_K, OFF_VR, OFF_GR, OFF_GA, OFF_GB = 0, 1024, 2048, 3072, 4096, 6144, 8192, 9216

LANES = 128
IN_TILE = 256
MIX_SEQS = 2
MIX_CHUNKS_F32 = 2
MOE_BLOCK = 1024
EXP_TILE = 128
ROW_ALIGN = 16
GATHER_CHUNK = 256
GROUP_ROWS = 8
ROUTE_ROWS = 8
ROUTER_ROWS = 48
EXPERTS_PER_STEP = 4
MOD_TILE = 1536
VMEM_LIMIT = 56 * 1024 * 1024
NEG = float("-inf")

BF16 = jnp.bfloat16
F32 = jnp.float32


def _dot(a, b):
    return jnp.dot(a, b, preferred_element_type=F32)


def _sigmoid(x):
    return 1.0 / (1.0 + jnp.exp(-x))


def _gelu_tanh(x):
    return x * (0.5 * (1.0 + jnp.tanh(0.7978845608028654 * (x + 0.044715 * (x * x * x)))))


def _rms(x):
    return x * lax.rsqrt(jnp.mean(x * x, axis=-1, keepdims=True) + EPS)


def _split_bf16(w):
    hi = w.astype(BF16)
    return hi, (w - hi.astype(F32)).astype(BF16)


def _dot_split(a, b):
    a_hi, a_lo = _split_bf16(a)
    b_hi, b_lo = _split_bf16(b)
    return _dot(a_hi, b_hi) + (_dot(a_lo, b_hi) + _dot(a_hi, b_lo))


def _mod_kernel(c_ref, w_ref, b_ref, o_ref):
    c = c_ref[...]
    o_ref[...] = _dot_split(c * _sigmoid(c), w_ref[...]) + b_ref[...]


def _mod_call(c_all, w_mod, b_mod):
    n = c_all.shape[0]
    return pl.pallas_call(
        _mod_kernel,
        grid=(DEPTH, MOD_COLS // MOD_TILE),
        in_specs=[
            pl.BlockSpec((n, D_MODEL), lambda l, j: (0, 0)),
            pl.BlockSpec((None, D_MODEL, MOD_TILE), lambda l, j: (l, 0, j)),
            pl.BlockSpec((None, 1, MOD_TILE), lambda l, j: (l, 0, j)),
        ],
        out_specs=pl.BlockSpec((None, n, MOD_TILE), lambda l, j: (l, 0, j)),
        out_shape=jax.ShapeDtypeStruct((DEPTH, n, MOD_COLS), F32),
        compiler_params=pltpu.CompilerParams(vmem_limit_bytes=VMEM_LIMIT),
        name="adaln_mod",
    )(c_all, w_mod, b_mod.reshape(DEPTH, 1, MOD_COLS))


def _inproj_kernel(x_ref, sh_ref, sc_ref, g_ref, cos_ref, sin_ref, dq_ref, dk_ref, lng_ref, lnb_ref,
                   w_ref, *rest, n_cast, pack_parts):
    n_in = n_cast + (3 if pack_parts else 0)
    z_ref = rest[n_in]
    for src_ref, dst_ref in zip(rest[:n_cast], rest[n_in + 1:]):
        dst_ref[...] = src_ref[...].astype(dst_ref.dtype)
    if pack_parts:
        wg_ref, wu_ref, wd_ref = rest[n_cast:n_in]
        pack_ref = rest[-1]
        gr, dr, half = wg_ref.shape[0], wd_ref.shape[0], D_MODEL // 2
        part = pl.program_id(0) % pack_parts
        for q in range(pack_parts):
            @pl.when(part == q)
            def _(q=q):
                pack_ref[q * gr:(q + 1) * gr, 0:EXPERT_DIM] = wg_ref[...].astype(BF16)
                pack_ref[q * gr:(q + 1) * gr, EXPERT_DIM:] = wu_ref[...].astype(BF16)
                lo = D_MODEL + q * dr
                pack_ref[lo:lo + dr, :] = wd_ref[:, :half].astype(BF16)
                pack_ref[lo + EXPERT_DIM:lo + EXPERT_DIM + dr, :] = wd_ref[:, half:].astype(BF16)

    h = (_rms(x_ref[...]) * g_ref[...]) * (1.0 + sc_ref[...]) + sh_ref[...]
    h = h.astype(BF16)
    cos = cos_ref[...]
    sin = sin_ref[...]

    def proj(off, width=D_MODEL):
        return _dot(h, w_ref[:, off:off + width])

    def put(off, val):
        z_ref[:, off:off + val.shape[1]] = val.astype(z_ref.dtype)

    def plain(off):
        put(off, proj(off))

    def swish(off):
        g = proj(off)
        put(off, g * _sigmoid(g))

    def rotary(off, d_ref):
        half = R_QK_DIM // 2
        acc = proj(off)
        for hd in range(R_HEADS):
            x1 = acc[:, hd * R_QK_DIM:hd * R_QK_DIM + half]
            x2 = acc[:, hd * R_QK_DIM + half:(hd + 1) * R_QK_DIM]
            dec = d_ref[:, hd * LANES:(hd + 1) * LANES]
            put(off + hd * R_QK_DIM, (x1 * cos - x2 * sin) * dec)
            put(off + hd * R_QK_DIM + half, (x1 * sin + x2 * cos) * dec)

    put(OFF_U, _gelu_tanh(proj(OFF_U)))
    plain(OFF_VR)
    v = _gelu_tanh(proj(OFF_V))
    mu = jnp.mean(v, axis=-1, keepdims=True)
    vc = v - mu
    var = jnp.mean(vc * vc, axis=-1, keepdims=True)
    put(OFF_V, vc * lax.rsqrt(var + EPS) * lng_ref[...] + lnb_ref[...])
    plain(OFF_VR + D_MODEL)
    rotary(OFF_Q, dq_ref)
    put(OFF_GA, _sigmoid(proj(OFF_GA)))
    rotary(OFF_K, dk_ref)
    put(OFF_GB, _sigmoid(proj(OFF_GB)))
    swish(OFF_GR)
    swish(OFF_GR + D_MODEL)


def _inproj_call(x, mod, g1, cos, sin, dq, dk, lng, lnb, w_in, *, layer, per_row, z_dtype, cast=(),
                 experts=None):
    rows = x.shape[0]
    tile = dq.shape[0]
    steps = rows // tile
    cast_in, cast_specs_in, cast_specs_out, cast_shapes = [], [], [], []
    pack_parts = 0
    pack_in, pack_specs_in, pack_specs_out, pack_shapes = [], [], [], []
    if experts is not None:
        assert steps % N_EXPERTS == 0 and D_MODEL // 2 == 2 * EXPERT_DIM
        pack_parts = steps // N_EXPERTS
        gr, dr = D_MODEL // pack_parts, EXPERT_DIM // pack_parts
        assert gr % ROW_ALIGN == 0 and dr % ROW_ALIGN == 0
        part_map = lambda i: (layer, i // pack_parts, i % pack_parts, 0)
        pack_in = list(experts)
        pack_specs_in = [pl.BlockSpec((None, None, gr, EXPERT_DIM), part_map),
                         pl.BlockSpec((None, None, gr, EXPERT_DIM), part_map),
                         pl.BlockSpec((None, None, dr, D_MODEL), part_map)]
        packed = (N_EXPERTS, D_MODEL + 2 * EXPERT_DIM, 2 * EXPERT_DIM)
        pack_specs_out = [pl.BlockSpec((None,) + packed[1:], lambda i: (i // pack_parts, 0, 0))]
        pack_shapes = [jax.ShapeDtypeStruct(packed, BF16)]
    for w in cast:
        cols = w.shape[-1]
        flat = w.reshape(w.shape[0], -1, cols)
        per = flat.shape[1] // steps
        assert flat.shape[1] % steps == 0 and per % ROW_ALIGN == 0
        cast_in.append(flat)
        cast_specs_in.append(pl.BlockSpec((None, per, cols), lambda i: (layer, i, 0)))
        cast_specs_out.append(pl.BlockSpec((per, cols), lambda i: (i, 0)))
        cast_shapes.append(jax.ShapeDtypeStruct(flat.shape[1:], BF16))
    tiles_per_seq = None if per_row else (rows // mod.shape[0]) // tile
    if per_row:
        sh_spec = pl.BlockSpec((None, tile, D_MODEL), lambda i: (layer, i, 0))
        sc_spec = pl.BlockSpec((None, tile, D_MODEL), lambda i: (layer, i, 1))
        rope_spec = pl.BlockSpec((tile, LANES), lambda i: (0, 0))
    else:
        sh_spec = pl.BlockSpec((None, 1, D_MODEL), lambda i: (i // tiles_per_seq, 0, 0))
        sc_spec = pl.BlockSpec((None, 1, D_MODEL), lambda i: (i // tiles_per_seq, 0, 1))
        rope_spec = pl.BlockSpec((tile, LANES), lambda i: (i % tiles_per_seq, 0))
    const = lambda i: (0, 0)
    z, *casted = pl.pallas_call(
        functools.partial(_inproj_kernel, n_cast=len(cast), pack_parts=pack_parts),
        grid=(steps,),
        in_specs=[
            pl.BlockSpec((tile, D_MODEL), lambda i: (i, 0)),
            sh_spec, sc_spec,
            pl.BlockSpec((1, D_MODEL), const),
            rope_spec, rope_spec,
            pl.BlockSpec((tile, R_HEADS * LANES), const),
            pl.BlockSpec((tile, R_HEADS * LANES), const),
            pl.BlockSpec((1, D_MODEL), const),
            pl.BlockSpec((1, D_MODEL), const),
            pl.BlockSpec((None, D_MODEL, IN_COLS), lambda i: (layer, 0, 0), pipeline_mode=pl.Buffered(1)),
            *cast_specs_in, *pack_specs_in,
        ],
        out_specs=[pl.BlockSpec((tile, IN_COLS), lambda i: (i, 0)), *cast_specs_out, *pack_specs_out],
        out_shape=[jax.ShapeDtypeStruct((rows, IN_COLS), z_dtype), *cast_shapes, *pack_shapes],
        compiler_params=pltpu.CompilerParams(vmem_limit_bytes=VMEM_LIMIT),
        name="in_proj",
    )(x, mod, mod, g1, cos, sin, dq, dk, lng, lnb, w_in, *cast_in, *pack_in)
    return (z, *[c.reshape(w.shape[1:]) for c, w in zip(casted, cast)], *casted[len(cast):])


def _carried_state_step(cd_ref, z_q_ref, z_k_ref, z_v_ref, r0_ref, cross_ref, r_ref, t_s):
    nrows = z_q_ref.shape[0]
    q = z_q_ref[...].astype(BF16)
    k = z_k_ref[...]
    v = z_v_ref[...].astype(BF16)
    row_k = lax.broadcasted_iota(jnp.int32, (nrows, R_QK_DIM), 0)
    row_v = lax.broadcasted_iota(jnp.int32, (nrows, R_V_DIM), 0)
    for hd in range(R_HEADS):
        qh = q[:, hd * R_QK_DIM:(hd + 1) * R_QK_DIM]
        kh = k[:, hd * R_QK_DIM:(hd + 1) * R_QK_DIM]
        vh = v[:, hd * R_V_DIM:(hd + 1) * R_V_DIM]
        cross = jnp.zeros((nrows, R_V_DIM), F32)
        for s in range(nrows // t_s):
            r = r0_ref[s, hd]
            lo, hi = s * t_s, (s + 1) * t_s
            cross = jnp.where((row_v >= lo) & (row_v < hi), _dot(qh, r.astype(BF16)), cross)
            k_seq = jnp.where((row_k >= lo) & (row_k < hi), kh, 0.0).astype(BF16)
            upd = lax.dot_general(k_seq, vh, (((0,), (0,)), ((), ())), preferred_element_type=F32)
            r_ref[s, hd] = cd_ref[hd] * (r + upd)
        cross_ref[:, hd * R_V_DIM:(hd + 1) * R_V_DIM] = cross


def _mix_sample_kernel(cd_ref, z_ref, x_ref, gt_ref, ws_ref, bs_ref, mask_ref, wa_ref, wb_ref, wo_ref,
                       cross_ref, xo_ref):
    _mix_step(cd_ref, z_ref, x_ref, gt_ref, ws_ref, bs_ref, mask_ref, wa_ref, wb_ref, wo_ref, xo_ref,
              cross_ref=cross_ref)


def _mix_prompt_kernel(cd_ref, z_ref, x_ref, gt_ref, ws_ref, bs_ref, mask_ref, wa_ref, wb_ref, wo_ref,
                       cds_ref, zq_ref, zk_ref, zv_ref, r0_ref, *rest, t_s):
    xo_ref, r_ref, cross_ref, rs_ref = rest[-4:]
    _mix_step(cd_ref, z_ref, x_ref, gt_ref, ws_ref, bs_ref, mask_ref, wa_ref, wb_ref, wo_ref, xo_ref,
              r_ref=r_ref)
    _carried_state_step(cds_ref, zq_ref, zk_ref, zv_ref, r0_ref, cross_ref, rs_ref, t_s)


def _mix_step(cd_ref, z_ref, x_ref, gt_ref, ws_ref, bs_ref, mask_ref, wa_ref, wb_ref, wo_ref, xo_ref, *,
              cross_ref=None, r_ref=None):
    nsb, nrows = z_ref.shape[:2]
    per_sb = nrows // CHUNK
    nseq = nsb * per_sb
    carried_state = cross_ref is not None
    if not carried_state:
        @pl.when(pl.program_id(1) == 0)
        def _():
            r_ref[...] = jnp.zeros_like(r_ref)

    mask = mask_ref[...]
    gdim = D_MODEL // A_GROUPS
    a_rows, b_rows = [], []
    for sq in range(nseq):
        sb, chunk = divmod(sq, per_sb)
        rs = slice(chunk * CHUNK, (chunk + 1) * CHUNK)

        def sec(off, width, sb=sb, rs=rs):
            return z_ref[sb, rs, off:off + width]

        v = sec(OFF_V, D_MODEL).astype(BF16)
        mixed = jnp.concatenate(
            [_dot(ws_ref[g], v[:, g * gdim:(g + 1) * gdim]) for g in range(A_GROUPS)], axis=1)
        a_rows.append((sec(OFF_U, D_MODEL).astype(F32) * (mixed + bs_ref[...])).astype(BF16))

        b_parts = []
        for hd in range(R_HEADS):
            qh = sec(OFF_Q + hd * R_QK_DIM, R_QK_DIM).astype(BF16)
            kh = sec(OFF_K + hd * R_QK_DIM, R_QK_DIM).astype(BF16)
            vh = sec(OFF_VR + hd * R_V_DIM, R_V_DIM).astype(BF16)
            s = lax.dot_general(qh, kh, (((1,), (1,)), ((), ())), preferred_element_type=F32) * mask
            y = _dot(s.astype(BF16), vh)
            if carried_state:
                y = y + cross_ref[sb, rs, hd * R_V_DIM:(hd + 1) * R_V_DIM]
            else:
                r = r_ref[sq, hd]
                y = y + _dot(qh, r.astype(BF16))
                upd = lax.dot_general(kh, vh, (((0,), (0,)), ((), ())), preferred_element_type=F32)
                r_ref[sq, hd] = cd_ref[hd] * (r + upd)
            mu = jnp.mean(y, axis=-1, keepdims=True)
            yc = y - mu
            var = jnp.mean(yc * yc, axis=-1, keepdims=True)
            yn = yc * lax.rsqrt(var + EPS)
            b_parts.append((sec(OFF_GR + hd * R_V_DIM, R_V_DIM).astype(F32) * yn).astype(BF16))
        b_rows.append(jnp.concatenate(b_parts, axis=1))

    pa = _dot(jnp.concatenate(a_rows, axis=0), wa_ref[...])
    pb = _dot(jnp.concatenate(b_rows, axis=0), wb_ref[...])
    ga = z_ref[:, :, OFF_GA:OFF_GA + D_MODEL].reshape(nseq * CHUNK, D_MODEL).astype(F32)
    gb = z_ref[:, :, OFF_GB:OFF_GB + D_MODEL].reshape(nseq * CHUNK, D_MODEL).astype(F32)
    mix = _dot((ga * pa + gb * pb).astype(BF16), wo_ref[...])
    xo_ref[...] = x_ref[...] + gt_ref[...] * mix.reshape(nsb, nrows, D_MODEL)


def _mix_specs(sb, rb, gt_spec):
    row_map = lambda b, c: (b, c, 0)
    const2 = lambda b, c: (0, 0)
    return [
        pl.BlockSpec(memory_space=pltpu.SMEM),
        pl.BlockSpec((sb, rb, IN_COLS), row_map),
        pl.BlockSpec((sb, rb, D_MODEL), row_map),
        gt_spec,
        pl.BlockSpec((A_GROUPS, CHUNK, CHUNK), lambda b, c: (0, 0, 0)),
        pl.BlockSpec((CHUNK, D_MODEL), const2),
        pl.BlockSpec((CHUNK, CHUNK), const2),
        pl.BlockSpec((D_MODEL, D_MODEL), const2, pipeline_mode=pl.Buffered(1)),
        pl.BlockSpec((R_HEADS * R_V_DIM, D_MODEL), const2, pipeline_mode=pl.Buffered(1)),
        pl.BlockSpec((D_MODEL, D_MODEL), const2, pipeline_mode=pl.Buffered(1)),
    ]


def _mix_sample_call(cd, z, x, mod, ws, bs, mask, wa, wb, wo, cross, *, layer):
    rows = x.shape[0]
    rb = MIX_CHUNKS_F32 * CHUNK
    assert rows % rb == 0
    view = lambda a: a.reshape(1, rows, a.shape[-1])
    row_map = lambda b, c: (b, c, 0)
    gt_spec = pl.BlockSpec((None, rb, D_MODEL), lambda b, c: (layer, c, 2))
    in_specs = _mix_specs(1, rb, gt_spec)
    in_specs.append(pl.BlockSpec((1, rb, R_HEADS * R_V_DIM), row_map))
    out = pl.pallas_call(
        _mix_sample_kernel,
        grid=(1, rows // rb),
        in_specs=in_specs,
        out_specs=pl.BlockSpec((1, rb, D_MODEL), row_map),
        out_shape=jax.ShapeDtypeStruct((1, rows, D_MODEL), F32),
        compiler_params=pltpu.CompilerParams(
            dimension_semantics=("arbitrary", "arbitrary"), vmem_limit_bytes=VMEM_LIMIT),
        name="token_mix_sample",
    )(cd, view(z), view(x), mod, ws, bs, mask, wa, wb, wo,
      view(cross))
    return out.reshape(rows, D_MODEL)


def _mix_prompt_call(cd, z, x, mod, ws, bs, mask, wa, wb, wo, cd_s, z_s, state_s, *, nseq, nchunk, t_s,
                     layer, stacked_p=None, stacked_s=None):
    rows = x.shape[0]
    seq_len = nchunk * CHUNK
    steps = (nseq // MIX_SEQS) * nchunk
    rows_s = z_s.shape[0]
    ret_rows = rows_s // steps
    assert nseq % MIX_SEQS == 0 and rows_s % steps == 0 and ret_rows % t_s == 0 and ret_rows % 8 == 0
    seqs_s = ret_rows // t_s
    qk_w = R_HEADS * R_QK_DIM
    v_w = R_HEADS * R_V_DIM
    view = lambda a: a.reshape(nseq, seq_len, a.shape[-1])
    row_map = lambda b, c: (b, c, 0)
    step_of = lambda b, c: b * nchunk + c
    gt_spec = pl.BlockSpec((MIX_SEQS, 1, D_MODEL), lambda b, c: (b, 0, 2))
    state_p_spec = pl.BlockSpec((None, MIX_SEQS, R_HEADS, R_QK_DIM, R_V_DIM), lambda b, c: (layer, b, 0, 0, 0))
    state_s_spec = pl.BlockSpec((None, seqs_s, R_HEADS, R_QK_DIM, R_V_DIM),
                                lambda b, c: (layer, step_of(b, c), 0, 0, 0))
    in_specs = _mix_specs(MIX_SEQS, CHUNK, gt_spec) + [
        pl.BlockSpec(memory_space=pltpu.SMEM),
        pl.BlockSpec((ret_rows, qk_w), lambda b, c: (step_of(b, c), OFF_Q // qk_w)),
        pl.BlockSpec((ret_rows, qk_w), lambda b, c: (step_of(b, c), OFF_K // qk_w)),
        pl.BlockSpec((ret_rows, v_w), lambda b, c: (step_of(b, c), OFF_VR // v_w)),
        state_s_spec,
    ]
    args = [cd, view(z), view(x), mod, ws, bs, mask, wa, wb, wo, cd_s, z_s, z_s, z_s, state_s]
    aliases = {}
    for buf, out_idx in ((stacked_p, 1), (stacked_s, 3)):
        if buf is not None:
            aliases[len(args)] = out_idx
            in_specs.append(pl.BlockSpec(memory_space=pl.ANY))
            args.append(buf)
    x_out, r_p, cross, r_s = pl.pallas_call(
        functools.partial(_mix_prompt_kernel, t_s=t_s),
        grid=(nseq // MIX_SEQS, nchunk),
        in_specs=in_specs,
        out_specs=[
            pl.BlockSpec((MIX_SEQS, CHUNK, D_MODEL), row_map),
            state_p_spec,
            pl.BlockSpec((ret_rows, v_w), lambda b, c: (step_of(b, c), 0)),
            state_s_spec,
        ],
        out_shape=[
            jax.ShapeDtypeStruct((nseq, seq_len, D_MODEL), F32),
            jax.ShapeDtypeStruct((DEPTH, nseq, R_HEADS, R_QK_DIM, R_V_DIM), F32),
            jax.ShapeDtypeStruct((rows_s, v_w), F32),
            jax.ShapeDtypeStruct(state_s.shape, F32),
        ],
        input_output_aliases=aliases,
        compiler_params=pltpu.CompilerParams(
            dimension_semantics=("arbitrary", "arbitrary"), vmem_limit_bytes=VMEM_LIMIT),
        name="token_mix_prompt",
    )(*args)
    return x_out.reshape(rows, D_MODEL), r_p, cross, r_s


def _sorted_rows(block):
    need = 2 * block + N_EXPERTS * ROW_ALIGN + EXP_TILE
    return -(-need // GATHER_CHUNK) * GATHER_CHUNK


def _route_kernel(x_ref, sh_ref, sc_ref, g_ref, wr2_ref, br_ref, tri_ref, low_ref,
                  h_ref, route_ref, off_ref, nt_ref):
    blk = x_ref.shape[0]
    h = (_rms(x_ref[...]) * g_ref[...]) * (1.0 + sc_ref[...]) + sh_ref[...]
    h_hi, h_lo = _split_bf16(h)
    h_ref[...] = h_hi

    nt_dims = (((1,), (1,)), ((), ()))
    nr = ROUTER_ROWS
    r_hi = lax.dot_general(wr2_ref[...], h_hi, nt_dims, preferred_element_type=F32)
    r_lo = lax.dot_general(wr2_ref[0:nr, :], h_lo, nt_dims, preferred_element_type=F32)
    logits = r_hi[0:nr] + r_hi[nr:2 * nr] + r_lo + br_ref[:, :1]

    gl = logits[0:GROUP_ROWS]
    grow = lax.broadcasted_iota(jnp.int32, (GROUP_ROWS, blk), 0).astype(F32)
    gmax = jnp.max(gl, axis=0, keepdims=True)
    g_w = 1.0 / jnp.sum(jnp.exp(gl - gmax), axis=0, keepdims=True)
    g_idx = jnp.min(jnp.where(gl == gmax, grow, float(GROUP_ROWS)), axis=0, keepdims=True)

    el = logits[GROUP_ROWS:GROUP_ROWS + N_EXPERTS]
    erow_i = lax.broadcasted_iota(jnp.int32, (N_EXPERTS, blk), 0)
    erow = erow_i.astype(F32)
    group_shift = EXPERTS_PER_GROUP.bit_length() - 1
    el = jnp.where((erow_i >> group_shift).astype(F32) == g_idx, el, NEG)
    m1 = jnp.max(el, axis=0, keepdims=True)
    i1 = jnp.min(jnp.where(el == m1, erow, float(N_EXPERTS)), axis=0, keepdims=True)
    el2 = jnp.where(erow == i1, NEG, el)
    m2 = jnp.max(el2, axis=0, keepdims=True)
    i2 = jnp.min(jnp.where(el2 == m2, erow, float(N_EXPERTS)), axis=0, keepdims=True)
    t = jnp.exp(m2 - m1)
    w1 = g_w / (1.0 + t)
    w2 = g_w * t / (1.0 + t)

    s1 = erow == i1
    s2 = erow == i2
    onehot = jnp.where(s1 | s2, 1.0, 0.0)
    rank = _dot(onehot.astype(BF16), tri_ref[...])
    cnt = jnp.sum(onehot, axis=1, keepdims=True)
    units = jnp.floor((cnt + (ROW_ALIGN - 1.0)) * (1.0 / ROW_ALIGN))
    units = jnp.broadcast_to(units, (N_EXPERTS, LANES))
    off = float(ROW_ALIGN) * _dot(low_ref[...], units.astype(BF16))
    base = off[:, :1] + rank
    pos1 = jnp.sum(jnp.where(s1, base, 0.0), axis=0, keepdims=True)
    pos2 = jnp.sum(jnp.where(s2, base, 0.0), axis=0, keepdims=True)

    r8 = lax.broadcasted_iota(jnp.int32, (ROUTE_ROWS, blk), 0)
    route_ref[...] = jnp.where(r8 == 0, pos1, jnp.where(r8 == 1, pos2, jnp.where(r8 == 2, w1,
                               jnp.where(r8 == 3, w2, 0.0))))
    off_ref[...] = off.astype(jnp.int32)
    tiles = jnp.floor((cnt + (EXP_TILE - 1.0)) * (1.0 / EXP_TILE))
    nt_ref[...] = jnp.broadcast_to(tiles, (N_EXPERTS, LANES)).astype(jnp.int32)


def _experts_kernel(off_sm, nt_sm, x_ref, h_ref, route_ref, gt_ref, w_ref, fg_ref, o_ref,
                    p_scr, xs_scr, ys_scr, cw_scr, *, final_norm):
    i = pl.program_id(0)
    step = pl.program_id(1)
    nrows, blk = p_scr.shape

    @pl.when(step == 0)
    def _():
        pos1 = route_ref[0:1, :]
        pos2 = route_ref[1:2, :]
        w1 = route_ref[2:3, :]
        w2 = route_ref[3:4, :]
        row_iota = lax.broadcasted_iota(jnp.int32, (GATHER_CHUNK, blk), 0)

        def gather(c, carry):
            r0 = pl.multiple_of(c * GATHER_CHUNK, GATHER_CHUNK)
            prow = (row_iota + r0).astype(F32)
            m1 = prow == pos1
            m2 = prow == pos2
            perm = jnp.where(m1 | m2, 1.0, 0.0).astype(BF16)
            p_scr[pl.ds(r0, GATHER_CHUNK), :] = perm
            xs_scr[pl.ds(r0, GATHER_CHUNK), :] = _dot(perm, h_ref[...]).astype(BF16)
            cw = jnp.sum(jnp.where(m1, w1, 0.0) + jnp.where(m2, w2, 0.0), axis=1, keepdims=True)
            cw_scr[pl.ds(r0, GATHER_CHUNK), :] = jnp.broadcast_to(cw, (GATHER_CHUNK, LANES))
            ys_scr[pl.ds(r0, GATHER_CHUNK), :] = jnp.zeros((GATHER_CHUNK, D_MODEL), BF16)
            return carry

        lax.fori_loop(0, nrows // GATHER_CHUNK, gather, 0)

    half = D_MODEL // 2

    def tile(g, s):
        s = pl.multiple_of(s, ROW_ALIGN)
        rows = xs_scr[pl.ds(s, EXP_TILE), :]
        gu = _dot(rows, w_ref[g, 0:D_MODEL, :])
        gate = gu[:, :EXPERT_DIM]
        up = gu[:, EXPERT_DIM:]
        cw = cw_scr[pl.ds(s, EXP_TILE), :][:, :1]
        hid = (gate * _sigmoid(gate) * up * cw).astype(BF16)
        lo = _dot(hid, w_ref[g, D_MODEL:D_MODEL + EXPERT_DIM, :]).astype(BF16)
        hi = _dot(hid, w_ref[g, D_MODEL + EXPERT_DIM:, :]).astype(BF16)
        return s, lo, hi

    def put(s, lo, hi):
        ys_scr[pl.ds(s, EXP_TILE), 0:half] = lo
        ys_scr[pl.ds(s, EXP_TILE), half:D_MODEL] = hi

    starts = [off_sm[i, step * EXPERTS_PER_STEP + g] for g in range(EXPERTS_PER_STEP)]
    first = [tile(g, starts[g]) for g in range(EXPERTS_PER_STEP)]
    for g in range(EXPERTS_PER_STEP):
        put(*first[g])

        def more(t, carry, g=g):
            put(*tile(g, starts[g] + t * EXP_TILE))
            return carry

        lax.fori_loop(1, nt_sm[i, step * EXPERTS_PER_STEP + g], more, 0)

    @pl.when(step == N_EXPERTS // EXPERTS_PER_STEP - 1)
    def _():
        tn_dims = (((0,), (0,)), ((), ()))
        for r0 in range(0, blk, GATHER_CHUNK):
            rs = slice(r0, r0 + GATHER_CHUNK)
            moe = lax.dot_general(p_scr[:, rs], ys_scr[...], tn_dims, preferred_element_type=F32)
            gt = gt_ref[...] if gt_ref.shape[0] == 1 else gt_ref[rs, :]
            y = x_ref[rs, :] + gt * moe
            if final_norm:
                y = _rms(y) * fg_ref[...]
            o_ref[rs, :] = y


def _moe_call(x, mod, g2, wr2, br, w_exp, fg, *, layer, block, per_row, final_norm):
    rows = x.shape[0]
    nblk = rows // block
    nsorted = _sorted_rows(block)
    if per_row:
        mspec = lambda s: pl.BlockSpec((None, block, D_MODEL), lambda i, *_: (layer, i, s))
    else:
        blocks_per_seq = (rows // mod.shape[0]) // block
        mspec = lambda s: pl.BlockSpec((None, 1, D_MODEL), lambda i, *_: (i // blocks_per_seq, 0, s))
    const = lambda i: (0, 0)
    ridx = jnp.arange(block)
    tri = (ridx[:, None] < ridx[None, :]).astype(BF16)
    eidx = jnp.arange(N_EXPERTS)
    low = (eidx[None, :] < eidx[:, None]).astype(BF16)
    h, route, off, nt = pl.pallas_call(
        _route_kernel,
        grid=(nblk,),
        in_specs=[
            pl.BlockSpec((block, D_MODEL), lambda i: (i, 0)),
            mspec(3), mspec(4),
            pl.BlockSpec((1, D_MODEL), const),
            pl.BlockSpec((2 * ROUTER_ROWS, D_MODEL), const),
            pl.BlockSpec((ROUTER_ROWS, LANES), const),
            pl.BlockSpec((block, block), const),
            pl.BlockSpec((N_EXPERTS, N_EXPERTS), const),
        ],
        out_specs=[
            pl.BlockSpec((block, D_MODEL), lambda i: (i, 0)),
            pl.BlockSpec((None, ROUTE_ROWS, block), lambda i: (i, 0, 0)),
            pl.BlockSpec((None, N_EXPERTS, LANES), lambda i: (i, 0, 0)),
            pl.BlockSpec((None, N_EXPERTS, LANES), lambda i: (i, 0, 0)),
        ],
        out_shape=[
            jax.ShapeDtypeStruct((rows, D_MODEL), BF16),
            jax.ShapeDtypeStruct((nblk, ROUTE_ROWS, block), F32),
            jax.ShapeDtypeStruct((nblk, N_EXPERTS, LANES), jnp.int32),
            jax.ShapeDtypeStruct((nblk, N_EXPERTS, LANES), jnp.int32),
        ],
        compiler_params=pltpu.CompilerParams(vmem_limit_bytes=VMEM_LIMIT),
        name="moe_route",
    )(x, mod, mod, g2, wr2, br, tri, low)

    const2 = lambda i, e, *_: (0, 0)
    wmap = lambda i, s, *_: (s, 0, 0)
    return pl.pallas_call(
        functools.partial(_experts_kernel, final_norm=final_norm),
        grid_spec=pltpu.PrefetchScalarGridSpec(
            num_scalar_prefetch=2,
            grid=(nblk, N_EXPERTS // EXPERTS_PER_STEP),
            in_specs=[
                pl.BlockSpec((block, D_MODEL), lambda i, e, *_: (i, 0)),
                pl.BlockSpec((block, D_MODEL), lambda i, e, *_: (i, 0), pipeline_mode=pl.Buffered(1)),
                pl.BlockSpec((None, ROUTE_ROWS, block), lambda i, e, *_: (i, 0, 0)),
                mspec(5),
                pl.BlockSpec((EXPERTS_PER_STEP, D_MODEL + 2 * EXPERT_DIM, 2 * EXPERT_DIM), wmap),
                pl.BlockSpec((1, D_MODEL), const2),
            ],
            out_specs=pl.BlockSpec((block, D_MODEL), lambda i, e, *_: (i, 0)),
            scratch_shapes=[
                pltpu.VMEM((nsorted, block), BF16),
                pltpu.VMEM((nsorted, D_MODEL), BF16),
                pltpu.VMEM((nsorted, D_MODEL), BF16),
                pltpu.VMEM((nsorted, LANES), F32),
            ],
        ),
        out_shape=jax.ShapeDtypeStruct((rows, D_MODEL), F32),
        compiler_params=pltpu.CompilerParams(
            dimension_semantics=("arbitrary", "arbitrary"), vmem_limit_bytes=VMEM_LIMIT),
        name="moe_experts",
    )(off[:, :, 0], nt[:, :, 0], x, h, route, mod, w_exp, fg)


def _rope_tables(pos):
    half = R_QK_DIM // 2
    inv_freq = jnp.power(ROPE_BASE, -jnp.arange(half, dtype=F32) / half)
    ang = pos.astype(F32)[:, None] * inv_freq[None, :]
    return jnp.cos(ang), jnp.sin(ang)


def _decay_tables(chunk, rows):
    log_g = jnp.log1p(-jnp.power(2.0, -5.0 - jnp.arange(R_HEADS, dtype=F32)))
    t1 = (jnp.arange(rows) % chunk).astype(F32) + 1.0
    dq = jnp.exp(log_g[None, :] * t1[:, None])
    dk = jnp.exp(-log_g[None, :] * t1[:, None]) * (R_QK_DIM ** -0.5)
    cd = jnp.exp(log_g * chunk)
    rep = lambda a: jnp.repeat(a, LANES, axis=1)
    return rep(dq), rep(dk), cd


def kernel(x_prompt, x_sample, state_ret, c_prompt, c_sample, w_mod, b_mod, norm1_g, w_in, ln_v_g, ln_v_b,
           w_s, b_s, w_a_out, w_b_out, w_o, norm2_g, w_router_group, b_router_group, w_router_expert,
           b_router_expert, w_gate, w_up, w_down, final_g):
    n_p, t_p, _ = x_prompt.shape
    n_s, t_s, _ = x_sample.shape
    assert t_p % CHUNK == 0 and (n_s * t_s) % CHUNK == 0 and CHUNK % t_s == 0
    chunks_p = t_p // CHUNK

    mod = _mod_call(jnp.concatenate([c_prompt, c_sample], axis=0), w_mod, b_mod)

    cos_p, sin_p = _rope_tables(jnp.arange(t_p, dtype=jnp.int32))
    cos_s, sin_s = _rope_tables(PAST_LEN + jnp.arange(IN_TILE, dtype=jnp.int32) % t_s)
    dq_p, dk_p, cd_p = _decay_tables(CHUNK, IN_TILE)
    dq_s, dk_s, cd_s = _decay_tables(t_s, IN_TILE)

    idx = jnp.arange(CHUNK)
    causal = idx[:, None] >= idx[None, :]
    same_seq = (idx[:, None] // t_s) == (idx[None, :] // t_s)
    mask_p = causal.astype(F32)
    mask_s = (causal & same_seq).astype(F32)

    xp = x_prompt.reshape(n_p * t_p, D_MODEL)
    xs = x_sample.reshape(n_s * t_s, D_MODEL)
    fg = final_g.reshape(1, D_MODEL)
    w_in_b = w_in.astype(BF16)
    mod_s = jnp.broadcast_to(mod[:, n_p:, None, :], (DEPTH, n_s, t_s, MOD_COLS))
    mod_s = mod_s.reshape(DEPTH, n_s * t_s, MOD_COLS)
    r_prompt, r_sample, v_sample = None, None, []
    for l in range(DEPTH):
        mod_p = mod[l, :n_p].reshape(n_p, 1, MOD_COLS)
        g1 = norm1_g[l].reshape(1, D_MODEL)
        g2 = norm2_g[l].reshape(1, D_MODEL)
        lng = ln_v_g[l].reshape(1, D_MODEL)
        lnb = ln_v_b[l].reshape(1, D_MODEL)

        ws_p = jnp.where(causal[None], w_s[l], 0.0).astype(BF16)
        bs_p = jnp.repeat(b_s[l].T, D_MODEL // A_GROUPS, axis=1)
        blk = jnp.where(causal[:t_s, :t_s][None], w_s[l][:, :t_s, :t_s], 0.0)
        ws_s = jnp.where(same_seq[None], jnp.tile(blk, (1, CHUNK // t_s, CHUNK // t_s)), 0.0).astype(BF16)
        bs_s = jnp.tile(bs_p[:t_s], (CHUNK // t_s, 1))

        zp, wa, wb, wo, w_exp = _inproj_call(
            xp, mod_p, g1, cos_p, sin_p, dq_p, dk_p, lng, lnb, w_in_b, layer=l, per_row=False, z_dtype=BF16,
            cast=(w_a_out, w_b_out, w_o), experts=(w_gate, w_up, w_down))
        (zs,) = _inproj_call(xs, mod_s, g1, cos_s, sin_s, dq_s, dk_s, lng, lnb, w_in_b,
                             layer=l, per_row=True, z_dtype=F32)

        xp, r_prompt, cross, r_sample = _mix_prompt_call(
            cd_p, zp, xp, mod_p, ws_p, bs_p, mask_p, wa, wb, wo, cd_s, zs, state_ret,
            nseq=n_p, nchunk=chunks_p, t_s=t_s, layer=l, stacked_p=r_prompt, stacked_s=r_sample)
        xs = _mix_sample_call(cd_s, zs, xs, mod_s, ws_s, bs_s, mask_s, wa, wb, wo, cross, layer=l)

        tail = ROUTER_ROWS - GROUP_ROWS - N_EXPERTS
        wr = jnp.concatenate([jnp.pad(w_router_group[l].T, ((0, GROUP_ROWS - N_GROUPS), (0, 0))),
                              jnp.pad(w_router_expert[l].T, ((0, tail), (0, 0)))], axis=0)
        br = jnp.concatenate([jnp.pad(b_router_group[l], (0, GROUP_ROWS - N_GROUPS), constant_values=NEG),
                              jnp.pad(b_router_expert[l], (0, tail))])
        wr2 = jnp.concatenate(_split_bf16(wr), axis=0)
        br = jnp.broadcast_to(br[:, None], (ROUTER_ROWS, LANES))
        last = l == DEPTH - 1
        xp = _moe_call(xp, mod_p, g2, wr2, br, w_exp, fg,
                       layer=l, block=MOE_BLOCK, per_row=False, final_norm=last)
        xs = _moe_call(xs, mod_s, g2, wr2, br, w_exp, fg,
                       layer=l, block=n_s * t_s, per_row=True, final_norm=last)

        v_sample.append(zs[:, OFF_V:OFF_V + D_MODEL].reshape(n_s, t_s, D_MODEL))

    return (xp.reshape(n_p, t_p, D_MODEL), xs.reshape(n_s, t_s, D_MODEL),
            r_prompt, r_sample, jnp.stack(v_sample))
```

```python
import functools

import jax
import jax.numpy as jnp
from jax import lax
from jax.experimental import pallas as pl
from jax.experimental.pallas import tpu as pltpu

D_MODEL = 1024
DEPTH = 2
PAST_LEN = 16384
A_GROUPS = 8
CHUNK = 128
R_HEADS = 4
R_QK_DIM = 256
R_V_DIM = 512
ROPE_BASE = 10000.0
N_GROUPS = 4
EXPERTS_PER_GROUP = 8
N_EXPERTS = 32
EXPERT_DIM = 256
EPS = 1e-6
IN_COLS = 10 * D_MODEL
MOD_COLS = 6 * D_MODEL

OFF_U, OFF_V, OFF_Q, OFF_K, OFF_VR, OFF_GR, OFF_GA, OFF_GB = 0, 1024, 2048, 3072, 4096, 6144, 8192, 9216

LANES = 128
IN_TILE = 256
MIX_SEQS = 2
MIX_CHUNKS_F32 = 2
MOE_BLOCK = 1024
EXP_TILE = 128
ROW_ALIGN = 16
GATHER_CHUNK = 256
GROUP_ROWS = 8
ROUTE_ROWS = 8
ROUTER_ROWS = 48
EXPERTS_PER_STEP = 4
MOD_TILE = 1536
VMEM_LIMIT = 56 * 1024 * 1024
NEG = float("-inf")

BF16 = jnp.bfloat16
F32 = jnp.float32


def _dot(a, b):
    return jnp.dot(a, b, preferred_element_type=F32)


def _sigmoid(x):
    return 1.0 / (1.0 + jnp.exp(-x))


def _gelu_tanh(x):
    return x * (0.5 * (1.0 + jnp.tanh(0.7978845608028654 * (x + 0.044715 * (x * x * x)))))


def _rms(x):
    return x * lax.rsqrt(jnp.mean(x * x, axis=-1, keepdims=True) + EPS)


def _split_bf16(w):
    hi = w.astype(BF16)
    return hi, (w - hi.astype(F32)).astype(BF16)


def _dot_split(a, b):
    a_hi, a_lo = _split_bf16(a)
    b_hi, b_lo = _split_bf16(b)
    return _dot(a_hi, b_hi) + (_dot(a_lo, b_hi) + _dot(a_hi, b_lo))


def _mod_kernel(c_ref, w_ref, b_ref, o_ref):
    c = c_ref[...]
    o_ref[...] = _dot_split(c * _sigmoid(c), w_ref[...]) + b_ref[...]


def _mod_call(c_all, w_mod, b_mod):
    n = c_all.shape[0]
    return pl.pallas_call(
        _mod_kernel,
        grid=(DEPTH, MOD_COLS // MOD_TILE),
        in_specs=[
            pl.BlockSpec((n, D_MODEL), lambda l, j: (0, 0)),
            pl.BlockSpec((None, D_MODEL, MOD_TILE), lambda l, j: (l, 0, j)),
            pl.BlockSpec((None, 1, MOD_TILE), lambda l, j: (l, 0, j)),
        ],
        out_specs=pl.BlockSpec((None, n, MOD_TILE), lambda l, j: (l, 0, j)),
        out_shape=jax.ShapeDtypeStruct((DEPTH, n, MOD_COLS), F32),
        compiler_params=pltpu.CompilerParams(vmem_limit_bytes=VMEM_LIMIT),
        name="adaln_mod",
    )(c_all, w_mod, b_mod.reshape(DEPTH, 1, MOD_COLS))


def _inproj_kernel(x_ref, sh_ref, sc_ref, g_ref, cos_ref, sin_ref, dq_ref, dk_ref, lng_ref, lnb_ref,
                   w_ref, *rest, n_cast, pack_parts):
    n_in = n_cast + (3 if pack_parts else 0)
    z_ref = rest[n_in]
    for src_ref, dst_ref in zip(rest[:n_cast], rest[n_in + 1:]):
        dst_ref[...] = src_ref[...].astype(dst_ref.dtype)
    if pack_parts:
        wg_ref, wu_ref, wd_ref = rest[n_cast:n_in]
        pack_ref = rest[-1]
        gr, dr, half = wg_ref.shape[0], wd_ref.shape[0], D_MODEL // 2
        part = pl.program_id(0) % pack_parts
        for q in range(pack_parts):
            @pl.when(part == q)
            def _(q=q):
                pack_ref[q * gr:(q + 1) * gr, 0:EXPERT_DIM] = wg_ref[...].astype(BF16)
                pack_ref[q * gr:(q + 1) * gr, EXPERT_DIM:] = wu_ref[...].astype(BF16)
                lo = D_MODEL + q * dr
                pack_ref[lo:lo + dr, :] = wd_ref[:, :half].astype(BF16)
                pack_ref[lo + EXPERT_DIM:lo + EXPERT_DIM + dr, :] = wd_ref[:, half:].astype(BF16)

    h = (_rms(x_ref[...]) * g_ref[...]) * (1.0 + sc_ref[...]) + sh_ref[...]
    h = h.astype(BF16)
    cos = cos_ref[...]
    sin = sin_ref[...]

    def proj(off, width=D_MODEL):
        return _dot(h, w_ref[:, off:off + width])

    def put(off, val):
        z_ref[:, off:off + val.shape[1]] = val.astype(z_ref.dtype)

    def plain(off):
        put(off, proj(off))

    def swish(off):
        g = proj(off)
        put(off, g * _sigmoid(g))

    def rotary(off, d_ref):
        half = R_QK_DIM // 2
        acc = proj(off)
        for hd in range(R_HEADS):
            x1 = acc[:, hd * R_QK_DIM:hd * R_QK_DIM + half]
            x2 = acc[:, hd * R_QK_DIM + half:(hd + 1) * R_QK_DIM]
            dec = d_ref[:, hd * LANES:(hd + 1) * LANES]
            put(off + hd * R_QK_DIM, (x1 * cos - x2 * sin) * dec)
            put(off + hd * R_QK_DIM + half, (x1 * sin + x2 * cos) * dec)

    put(OFF_U, _gelu_tanh(proj(OFF_U)))
    plain(OFF_VR)
    v = _gelu_tanh(proj(OFF_V))
    mu = jnp.mean(v, axis=-1, keepdims=True)
    vc = v - mu
    var = jnp.mean(vc * vc, axis=-1, keepdims=True)
    put(OFF_V, vc * lax.rsqrt(var + EPS) * lng_ref[...] + lnb_ref[...])
    plain(OFF_VR + D_MODEL)
    rotary(OFF_Q, dq_ref)
    put(OFF_GA, _sigmoid(proj(OFF_GA)))
    rotary(OFF_K, dk_ref)
    put(OFF_GB, _sigmoid(proj(OFF_GB)))
    swish(OFF_GR)
    swish(OFF_GR + D_MODEL)


def _inproj_call(x, mod, g1, cos, sin, dq, dk, lng, lnb, w_in, *, layer, per_row, z_dtype, cast=(),
                 experts=None):
    rows = x.shape[0]
    tile = dq.shape[0]
    steps = rows // tile
    cast_in, cast_specs_in, cast_specs_out, cast_shapes = [], [], [], []
    pack_parts = 0
    pack_in, pack_specs_in, pack_specs_out, pack_shapes = [], [], [], []
    if experts is not None:
        assert steps % N_EXPERTS == 0 and D_MODEL // 2 == 2 * EXPERT_DIM
        pack_parts = steps // N_EXPERTS
        gr, dr = D_MODEL // pack_parts, EXPERT_DIM // pack_parts
        assert gr % ROW_ALIGN == 0 and dr % ROW_ALIGN == 0
        part_map = lambda i: (layer, i // pack_parts, i % pack_parts, 0)
        pack_in = list(experts)
        pack_specs_in = [pl.BlockSpec((None, None, gr, EXPERT_DIM), part_map),
                         pl.BlockSpec((None, None, gr, EXPERT_DIM), part_map),
                         pl.BlockSpec((None, None, dr, D_MODEL), part_map)]
        packed = (N_EXPERTS, D_MODEL + 2 * EXPERT_DIM, 2 * EXPERT_DIM)
        pack_specs_out = [pl.BlockSpec((None,) + packed[1:], lambda i: (i // pack_parts, 0, 0))]
        pack_shapes = [jax.ShapeDtypeStruct(packed, BF16)]
    for w in cast:
        cols = w.shape[-1]
        flat = w.reshape(w.shape[0], -1, cols)
        per = flat.shape[1] // steps
        assert flat.shape[1] % steps == 0 and per % ROW_ALIGN == 0
        cast_in.append(flat)
        cast_specs_in.append(pl.BlockSpec((None, per, cols), lambda i: (layer, i, 0)))
        cast_specs_out.append(pl.BlockSpec((per, cols), lambda i: (i, 0)))
        cast_shapes.append(jax.ShapeDtypeStruct(flat.shape[1:], BF16))
    tiles_per_seq = None if per_row else (rows // mod.shape[0]) // tile
    if per_row:
        sh_spec = pl.BlockSpec((None, tile, D_MODEL), lambda i: (layer, i, 0))
        sc_spec = pl.BlockSpec((None, tile, D_MODEL), lambda i: (layer, i, 1))
        rope_spec = pl.BlockSpec((tile, LANES), lambda i: (0, 0))
    else:
        sh_spec = pl.BlockSpec((None, 1, D_MODEL), lambda i: (i // tiles_per_seq, 0, 0))
        sc_spec = pl.BlockSpec((None, 1, D_MODEL), lambda i: (i // tiles_per_seq, 0, 1))
        rope_spec = pl.BlockSpec((tile, LANES), lambda i: (i % tiles_per_seq, 0))
    const = lambda i: (0, 0)
    z, *casted = pl.pallas_call(
        functools.partial(_inproj_kernel, n_cast=len(cast), pack_parts=pack_parts),
        grid=(steps,),
        in_specs=[
            pl.BlockSpec((tile, D_MODEL), lambda i: (i, 0)),
            sh_spec, sc_spec,
            pl.BlockSpec((1, D_MODEL), const),
            rope_spec, rope_spec,
            pl.BlockSpec((tile, R_HEADS * LANES), const),
            pl.BlockSpec((tile, R_HEADS * LANES), const),
            pl.BlockSpec((1, D_MODEL), const),
            pl.BlockSpec((1, D_MODEL), const),
            pl.BlockSpec((D_MODEL, IN_COLS), const, pipeline_mode=pl.Buffered(1)),
            *cast_specs_in, *pack_specs_in,
        ],
        out_specs=[pl.BlockSpec((tile, IN_COLS), lambda i: (i, 0)), *cast_specs_out, *pack_specs_out],
        out_shape=[jax.ShapeDtypeStruct((rows, IN_COLS), z_dtype), *cast_shapes, *pack_shapes],
        compiler_params=pltpu.CompilerParams(vmem_limit_bytes=VMEM_LIMIT),
        name="in_proj",
    )(x, mod, mod, g1, cos, sin, dq, dk, lng, lnb, w_in, *cast_in, *pack_in)
    return (z, *[c.reshape(w.shape[1:]) for c, w in zip(casted, cast)], *casted[len(cast):])


def _carried_state_step(cd_ref, z_q_ref, z_k_ref, z_v_ref, r0_ref, cross_ref, r_ref, t_s):
    nrows = z_q_ref.shape[0]
    q = z_q_ref[...].astype(BF16)
    k = z_k_ref[...]
    v = z_v_ref[...].astype(BF16)
    row_k = lax.broadcasted_iota(jnp.int32, (nrows, R_QK_DIM), 0)
    row_v = lax.broadcasted_iota(jnp.int32, (nrows, R_V_DIM), 0)
    for hd in range(R_HEADS):
        qh = q[:, hd * R_QK_DIM:(hd + 1) * R_QK_DIM]
        kh = k[:, hd * R_QK_DIM:(hd + 1) * R_QK_DIM]
        vh = v[:, hd * R_V_DIM:(hd + 1) * R_V_DIM]
        cross = jnp.zeros((nrows, R_V_DIM), F32)
        for s in range(nrows // t_s):
            r = r0_ref[s, hd]
            lo, hi = s * t_s, (s + 1) * t_s
            cross = jnp.where((row_v >= lo) & (row_v < hi), _dot(qh, r.astype(BF16)), cross)
            k_seq = jnp.where((row_k >= lo) & (row_k < hi), kh, 0.0).astype(BF16)
            upd = lax.dot_general(k_seq, vh, (((0,), (0,)), ((), ())), preferred_element_type=F32)
            r_ref[s, hd] = cd_ref[hd] * (r + upd)
        cross_ref[:, hd * R_V_DIM:(hd + 1) * R_V_DIM] = cross


def _mix_sample_kernel(cd_ref, z_ref, x_ref, gt_ref, ws_ref, bs_ref, mask_ref, wa_ref, wb_ref, wo_ref,
                       cross_ref, xo_ref):
    _mix_step(cd_ref, z_ref, x_ref, gt_ref, ws_ref, bs_ref, mask_ref, wa_ref, wb_ref, wo_ref, xo_ref,
              cross_ref=cross_ref)


def _mix_prompt_kernel(cd_ref, z_ref, x_ref, gt_ref, ws_ref, bs_ref, mask_ref, wa_ref, wb_ref, wo_ref,
                       cds_ref, zq_ref, zk_ref, zv_ref, r0_ref, *rest, t_s, n_cast):
    outs = rest[len(rest) - 4 - n_cast:]
    xo_ref, r_ref, cross_ref, rs_ref = outs[:4]
    for src_ref, dst_ref in zip(rest[:n_cast], outs[4:]):
        dst_ref[...] = src_ref[...].astype(dst_ref.dtype)
    _mix_step(cd_ref, z_ref, x_ref, gt_ref, ws_ref, bs_ref, mask_ref, wa_ref, wb_ref, wo_ref, xo_ref,
              r_ref=r_ref)
    _carried_state_step(cds_ref, zq_ref, zk_ref, zv_ref, r0_ref, cross_ref, rs_ref, t_s)


def _mix_step(cd_ref, z_ref, x_ref, gt_ref, ws_ref, bs_ref, mask_ref, wa_ref, wb_ref, wo_ref, xo_ref, *,
              cross_ref=None, r_ref=None):
    nsb, nrows = z_ref.shape[:2]
    per_sb = nrows // CHUNK
    nseq = nsb * per_sb
    carried_state = cross_ref is not None
    if not carried_state:
        @pl.when(pl.program_id(1) == 0)
        def _():
            r_ref[...] = jnp.zeros_like(r_ref)

    mask = mask_ref[...]
    gdim = D_MODEL // A_GROUPS
    a_rows, b_rows = [], []
    for sq in range(nseq):
        sb, chunk = divmod(sq, per_sb)
        rs = slice(chunk * CHUNK, (chunk + 1) * CHUNK)

        def sec(off, width, sb=sb, rs=rs):
            return z_ref[sb, rs, off:off + width]

        v = sec(OFF_V, D_MODEL).astype(BF16)
        mixed = jnp.concatenate(
            [_dot(ws_ref[g], v[:, g * gdim:(g + 1) * gdim]) for g in range(A_GROUPS)], axis=1)
        a_rows.append((sec(OFF_U, D_MODEL).astype(F32) * (mixed + bs_ref[...])).astype(BF16))

        b_parts = []
        for hd in range(R_HEADS):
            qh = sec(OFF_Q + hd * R_QK_DIM, R_QK_DIM).astype(BF16)
            kh = sec(OFF_K + hd * R_QK_DIM, R_QK_DIM).astype(BF16)
            vh = sec(OFF_VR + hd * R_V_DIM, R_V_DIM).astype(BF16)
            s = lax.dot_general(qh, kh, (((1,), (1,)), ((), ())), preferred_element_type=F32) * mask
            y = _dot(s.astype(BF16), vh)
            if carried_state:
                y = y + cross_ref[sb, rs, hd * R_V_DIM:(hd + 1) * R_V_DIM]
            else:
                r = r_ref[sq, hd]
                y = y + _dot(qh, r.astype(BF16))
                upd = lax.dot_general(kh, vh, (((0,), (0,)), ((), ())), preferred_element_type=F32)
                r_ref[sq, hd] = cd_ref[hd] * (r + upd)
            mu = jnp.mean(y, axis=-1, keepdims=True)
            yc = y - mu
            var = jnp.mean(yc * yc, axis=-1, keepdims=True)
            yn = yc * lax.rsqrt(var + EPS)
            b_parts.append((sec(OFF_GR + hd * R_V_DIM, R_V_DIM).astype(F32) * yn).astype(BF16))
        b_rows.append(jnp.concatenate(b_parts, axis=1))

    pa = _dot(jnp.concatenate(a_rows, axis=0), wa_ref[...])
    pb = _dot(jnp.concatenate(b_rows, axis=0), wb_ref[...])
    ga = z_ref[:, :, OFF_GA:OFF_GA + D_MODEL].reshape(nseq * CHUNK, D_MODEL).astype(F32)
    gb = z_ref[:, :, OFF_GB:OFF_GB + D_MODEL].reshape(nseq * CHUNK, D_MODEL).astype(F32)
    mix = _dot((ga * pa + gb * pb).astype(BF16), wo_ref[...])
    xo_ref[...] = x_ref[...] + gt_ref[...] * mix.reshape(nsb, nrows, D_MODEL)


def _mix_specs(sb, rb, gt_spec):
    row_map = lambda b, c: (b, c, 0)
    const2 = lambda b, c: (0, 0)
    return [
        pl.BlockSpec(memory_space=pltpu.SMEM),
        pl.BlockSpec((sb, rb, IN_COLS), row_map),
        pl.BlockSpec((sb, rb, D_MODEL), row_map),
        gt_spec,
        pl.BlockSpec((A_GROUPS, CHUNK, CHUNK), lambda b, c: (0, 0, 0)),
        pl.BlockSpec((CHUNK, D_MODEL), const2),
        pl.BlockSpec((CHUNK, CHUNK), const2),
        pl.BlockSpec((D_MODEL, D_MODEL), const2, pipeline_mode=pl.Buffered(1)),
        pl.BlockSpec((R_HEADS * R_V_DIM, D_MODEL), const2, pipeline_mode=pl.Buffered(1)),
        pl.BlockSpec((D_MODEL, D_MODEL), const2, pipeline_mode=pl.Buffered(1)),
    ]


def _mix_sample_call(cd, z, x, mod, ws, bs, mask, wa, wb, wo, cross, *, layer):
    rows = x.shape[0]
    rb = MIX_CHUNKS_F32 * CHUNK
    assert rows % rb == 0
    view = lambda a: a.reshape(1, rows, a.shape[-1])
    row_map = lambda b, c: (b, c, 0)
    gt_spec = pl.BlockSpec((None, rb, D_MODEL), lambda b, c: (layer, c, 2))
    in_specs = _mix_specs(1, rb, gt_spec)
    in_specs.append(pl.BlockSpec((1, rb, R_HEADS * R_V_DIM), row_map))
    out = pl.pallas_call(
        _mix_sample_kernel,
        grid=(1, rows // rb),
        in_specs=in_specs,
        out_specs=pl.BlockSpec((1, rb, D_MODEL), row_map),
        out_shape=jax.ShapeDtypeStruct((1, rows, D_MODEL), F32),
        compiler_params=pltpu.CompilerParams(
            dimension_semantics=("arbitrary", "arbitrary"), vmem_limit_bytes=VMEM_LIMIT),
        name="token_mix_sample",
    )(cd, view(z), view(x), mod, ws, bs, mask, wa, wb, wo,
      view(cross))
    return out.reshape(rows, D_MODEL)


def _mix_prompt_call(cd, z, x, mod, ws, bs, mask, wa, wb, wo, cd_s, z_s, state_s, *, nseq, nchunk, t_s,
                     layer, stacked_p=None, stacked_s=None, cast=(), cast_layer=0):
    rows = x.shape[0]
    seq_len = nchunk * CHUNK
    steps = (nseq // MIX_SEQS) * nchunk
    rows_s = z_s.shape[0]
    ret_rows = rows_s // steps
    assert nseq % MIX_SEQS == 0 and rows_s % steps == 0 and ret_rows % t_s == 0 and ret_rows % 8 == 0
    seqs_s = ret_rows // t_s
    qk_w = R_HEADS * R_QK_DIM
    v_w = R_HEADS * R_V_DIM
    view = lambda a: a.reshape(nseq, seq_len, a.shape[-1])
    row_map = lambda b, c: (b, c, 0)
    step_of = lambda b, c: b * nchunk + c
    gt_spec = pl.BlockSpec((MIX_SEQS, 1, D_MODEL), lambda b, c: (b, 0, 2))
    state_p_spec = pl.BlockSpec((None, MIX_SEQS, R_HEADS, R_QK_DIM, R_V_DIM), lambda b, c: (layer, b, 0, 0, 0))
    state_s_spec = pl.BlockSpec((None, seqs_s, R_HEADS, R_QK_DIM, R_V_DIM),
                                lambda b, c: (layer, step_of(b, c), 0, 0, 0))
    in_specs = _mix_specs(MIX_SEQS, CHUNK, gt_spec) + [
        pl.BlockSpec(memory_space=pltpu.SMEM),
        pl.BlockSpec((ret_rows, qk_w), lambda b, c: (step_of(b, c), OFF_Q // qk_w)),
        pl.BlockSpec((ret_rows, qk_w), lambda b, c: (step_of(b, c), OFF_K // qk_w)),
        pl.BlockSpec((ret_rows, v_w), lambda b, c: (step_of(b, c), OFF_VR // v_w)),
        state_s_spec,
    ]
    args = [cd, view(z), view(x), mod, ws, bs, mask, wa, wb, wo, cd_s, z_s, z_s, z_s, state_s]
    cast_specs_out, cast_shapes = [], []
    for w in cast:
        per = w.shape[1] // steps
        assert w.shape[1] % steps == 0 and per % ROW_ALIGN == 0
        in_specs.append(pl.BlockSpec((None, per, w.shape[2]), lambda b, c: (cast_layer, step_of(b, c), 0)))
        args.append(w)
        cast_specs_out.append(pl.BlockSpec((per, w.shape[2]), lambda b, c: (step_of(b, c), 0)))
        cast_shapes.append(jax.ShapeDtypeStruct(w.shape[1:], BF16))
    aliases = {}
    for buf, out_idx in ((stacked_p, 1), (stacked_s, 3)):
        if buf is not None:
            aliases[len(args)] = out_idx
            in_specs.append(pl.BlockSpec(memory_space=pl.ANY))
            args.append(buf)
    x_out, r_p, cross, r_s, *casted = pl.pallas_call(
        functools.partial(_mix_prompt_kernel, t_s=t_s, n_cast=len(cast)),
        grid=(nseq // MIX_SEQS, nchunk),
        in_specs=in_specs,
        out_specs=[
            pl.BlockSpec((MIX_SEQS, CHUNK, D_MODEL), row_map),
            state_p_spec,
            pl.BlockSpec((ret_rows, v_w), lambda b, c: (step_of(b, c), 0)),
            state_s_spec,
            *cast_specs_out,
        ],
        out_shape=[
            jax.ShapeDtypeStruct((nseq, seq_len, D_MODEL), F32),
            jax.ShapeDtypeStruct((DEPTH, nseq, R_HEADS, R_QK_DIM, R_V_DIM), F32),
            jax.ShapeDtypeStruct((rows_s, v_w), F32),
            jax.ShapeDtypeStruct(state_s.shape, F32),
            *cast_shapes,
        ],
        input_output_aliases=aliases,
        compiler_params=pltpu.CompilerParams(
            dimension_semantics=("arbitrary", "arbitrary"), vmem_limit_bytes=VMEM_LIMIT),
        name="token_mix_prompt",
    )(*args)
    return (x_out.reshape(rows, D_MODEL), r_p, cross, r_s, *casted)


def _sorted_rows(block):
    need = 2 * block + N_EXPERTS * ROW_ALIGN + EXP_TILE
    return -(-need // GATHER_CHUNK) * GATHER_CHUNK


def _route_kernel(x_ref, sh_ref, sc_ref, g_ref, wr2_ref, br_ref, tri_ref, low_ref,
                  h_ref, route_ref, off_ref, nt_ref):
    blk = x_ref.shape[0]
    h = (_rms(x_ref[...]) * g_ref[...]) * (1.0 + sc_ref[...]) + sh_ref[...]
    h_hi, h_lo = _split_bf16(h)
    h_ref[...] = h_hi

    nt_dims = (((1,), (1,)), ((), ()))
    nr = ROUTER_ROWS
    r_hi = lax.dot_general(wr2_ref[...], h_hi, nt_dims, preferred_element_type=F32)
    r_lo = lax.dot_general(wr2_ref[0:nr, :], h_lo, nt_dims, preferred_element_type=F32)
    logits = r_hi[0:nr] + r_hi[nr:2 * nr] + r_lo + br_ref[:, :1]

    gl = logits[0:GROUP_ROWS]
    grow = lax.broadcasted_iota(jnp.int32, (GROUP_ROWS, blk), 0).astype(F32)
    gmax = jnp.max(gl, axis=0, keepdims=True)
    g_w = 1.0 / jnp.sum(jnp.exp(gl - gmax), axis=0, keepdims=True)
    g_idx = jnp.min(jnp.where(gl == gmax, grow, float(GROUP_ROWS)), axis=0, keepdims=True)

    el = logits[GROUP_ROWS:GROUP_ROWS + N_EXPERTS]
    erow_i = lax.broadcasted_iota(jnp.int32, (N_EXPERTS, blk), 0)
    erow = erow_i.astype(F32)
    group_shift = EXPERTS_PER_GROUP.bit_length() - 1
    el = jnp.where((erow_i >> group_shift).astype(F32) == g_idx, el, NEG)
    m1 = jnp.max(el, axis=0, keepdims=True)
    i1 = jnp.min(jnp.where(el == m1, erow, float(N_EXPERTS)), axis=0, keepdims=True)
    el2 = jnp.where(erow == i1, NEG, el)
    m2 = jnp.max(el2, axis=0, keepdims=True)
    i2 = jnp.min(jnp.where(el2 == m2, erow, float(N_EXPERTS)), axis=0, keepdims=True)
    t = jnp.exp(m2 - m1)
    w1 = g_w / (1.0 + t)
    w2 = g_w * t / (1.0 + t)

    s1 = erow == i1
    s2 = erow == i2
    onehot = jnp.where(s1 | s2, 1.0, 0.0)
    rank = _dot(onehot.astype(BF16), tri_ref[...])
    cnt = jnp.sum(onehot, axis=1, keepdims=True)
    units = jnp.floor((cnt + (ROW_ALIGN - 1.0)) * (1.0 / ROW_ALIGN))
    units = jnp.broadcast_to(units, (N_EXPERTS, LANES))
    off = float(ROW_ALIGN) * _dot(low_ref[...], units.astype(BF16))
    base = off[:, :1] + rank
    pos1 = jnp.sum(jnp.where(s1, base, 0.0), axis=0, keepdims=True)
    pos2 = jnp.sum(jnp.where(s2, base, 0.0), axis=0, keepdims=True)

    r8 = lax.broadcasted_iota(jnp.int32, (ROUTE_ROWS, blk), 0)
    route_ref[...] = jnp.where(r8 == 0, pos1, jnp.where(r8 == 1, pos2, jnp.where(r8 == 2, w1,
                               jnp.where(r8 == 3, w2, 0.0))))
    off_ref[...] = off.astype(jnp.int32)
    tiles = jnp.floor((cnt + (EXP_TILE - 1.0)) * (1.0 / EXP_TILE))
    nt_ref[...] = jnp.broadcast_to(tiles, (N_EXPERTS, LANES)).astype(jnp.int32)


def _experts_kernel(off_sm, nt_sm, x_ref, h_ref, route_ref, gt_ref, w_ref, fg_ref, o_ref,
                    p_scr, xs_scr, ys_scr, cw_scr, *, final_norm):
    i = pl.program_id(0)
    step = pl.program_id(1)
    nrows, blk = p_scr.shape

    @pl.when(step == 0)
    def _():
        pos1 = route_ref[0:1, :]
        pos2 = route_ref[1:2, :]
        w1 = route_ref[2:3, :]
        w2 = route_ref[3:4, :]
        row_iota = lax.broadcasted_iota(jnp.int32, (GATHER_CHUNK, blk), 0)

        def gather(c, carry):
            r0 = pl.multiple_of(c * GATHER_CHUNK, GATHER_CHUNK)
            prow = (row_iota + r0).astype(F32)
            m1 = prow == pos1
            m2 = prow == pos2
            perm = jnp.where(m1 | m2, 1.0, 0.0).astype(BF16)
            p_scr[pl.ds(r0, GATHER_CHUNK), :] = perm
            xs_scr[pl.ds(r0, GATHER_CHUNK), :] = _dot(perm, h_ref[...]).astype(BF16)
            cw = jnp.sum(jnp.where(m1, w1, 0.0) + jnp.where(m2, w2, 0.0), axis=1, keepdims=True)
            cw_scr[pl.ds(r0, GATHER_CHUNK), :] = jnp.broadcast_to(cw, (GATHER_CHUNK, LANES))
            ys_scr[pl.ds(r0, GATHER_CHUNK), :] = jnp.zeros((GATHER_CHUNK, D_MODEL), BF16)
            return carry

        lax.fori_loop(0, nrows // GATHER_CHUNK, gather, 0)

    half = D_MODEL // 2

    def tile(g, s):
        s = pl.multiple_of(s, ROW_ALIGN)
        rows = xs_scr[pl.ds(s, EXP_TILE), :]
        gu = _dot(rows, w_ref[g, 0:D_MODEL, :])
        gate = gu[:, :EXPERT_DIM]
        up = gu[:, EXPERT_DIM:]
        cw = cw_scr[pl.ds(s, EXP_TILE), :][:, :1]
        hid = (gate * _sigmoid(gate) * up * cw).astype(BF16)
        lo = _dot(hid, w_ref[g, D_MODEL:D_MODEL + EXPERT_DIM, :]).astype(BF16)
        hi = _dot(hid, w_ref[g, D_MODEL + EXPERT_DIM:, :]).astype(BF16)
        return s, lo, hi

    def put(s, lo, hi):
        ys_scr[pl.ds(s, EXP_TILE), 0:half] = lo
        ys_scr[pl.ds(s, EXP_TILE), half:D_MODEL] = hi

    starts = [off_sm[i, step * EXPERTS_PER_STEP + g] for g in range(EXPERTS_PER_STEP)]
    first = [tile(g, starts[g]) for g in range(EXPERTS_PER_STEP)]
    for g in range(EXPERTS_PER_STEP):
        put(*first[g])

        def more(t, carry, g=g):
            put(*tile(g, starts[g] + t * EXP_TILE))
            return carry

        lax.fori_loop(1, nt_sm[i, step * EXPERTS_PER_STEP + g], more, 0)

    @pl.when(step == N_EXPERTS // EXPERTS_PER_STEP - 1)
    def _():
        tn_dims = (((0,), (0,)), ((), ()))
        for r0 in range(0, blk, GATHER_CHUNK):
            rs = slice(r0, r0 + GATHER_CHUNK)
            moe = lax.dot_general(p_scr[:, rs], ys_scr[...], tn_dims, preferred_element_type=F32)
            gt = gt_ref[...] if gt_ref.shape[0] == 1 else gt_ref[rs, :]
            y = x_ref[rs, :] + gt * moe
            if final_norm:
                y = _rms(y) * fg_ref[...]
            o_ref[rs, :] = y


def _moe_call(x, mod, g2, wr2, br, w_exp, fg, *, layer, block, per_row, final_norm):
    rows = x.shape[0]
    nblk = rows // block
    nsorted = _sorted_rows(block)
    if per_row:
        mspec = lambda s: pl.BlockSpec((None, block, D_MODEL), lambda i, *_: (layer, i, s))
    else:
        blocks_per_seq = (rows // mod.shape[0]) // block
        mspec = lambda s: pl.BlockSpec((None, 1, D_MODEL), lambda i, *_: (i // blocks_per_seq, 0, s))
    const = lambda i: (0, 0)
    ridx = jnp.arange(block)
    tri = (ridx[:, None] < ridx[None, :]).astype(BF16)
    eidx = jnp.arange(N_EXPERTS)
    low = (eidx[None, :] < eidx[:, None]).astype(BF16)
    h, route, off, nt = pl.pallas_call(
        _route_kernel,
        grid=(nblk,),
        in_specs=[
            pl.BlockSpec((block, D_MODEL), lambda i: (i, 0)),
            mspec(3), mspec(4),
            pl.BlockSpec((1, D_MODEL), const),
            pl.BlockSpec((2 * ROUTER_ROWS, D_MODEL), const),
            pl.BlockSpec((ROUTER_ROWS, LANES), const),
            pl.BlockSpec((block, block), const),
            pl.BlockSpec((N_EXPERTS, N_EXPERTS), const),
        ],
        out_specs=[
            pl.BlockSpec((block, D_MODEL), lambda i: (i, 0)),
            pl.BlockSpec((None, ROUTE_ROWS, block), lambda i: (i, 0, 0)),
            pl.BlockSpec((None, N_EXPERTS, LANES), lambda i: (i, 0, 0)),
            pl.BlockSpec((None, N_EXPERTS, LANES), lambda i: (i, 0, 0)),
        ],
        out_shape=[
            jax.ShapeDtypeStruct((rows, D_MODEL), BF16),
            jax.ShapeDtypeStruct((nblk, ROUTE_ROWS, block), F32),
            jax.ShapeDtypeStruct((nblk, N_EXPERTS, LANES), jnp.int32),
            jax.ShapeDtypeStruct((nblk, N_EXPERTS, LANES), jnp.int32),
        ],
        compiler_params=pltpu.CompilerParams(vmem_limit_bytes=VMEM_LIMIT),
        name="moe_route",
    )(x, mod, mod, g2, wr2, br, tri, low)

    const2 = lambda i, e, *_: (0, 0)
    wmap = lambda i, s, *_: (s, 0, 0)
    return pl.pallas_call(
        functools.partial(_experts_kernel, final_norm=final_norm),
        grid_spec=pltpu.PrefetchScalarGridSpec(
            num_scalar_prefetch=2,
            grid=(nblk, N_EXPERTS // EXPERTS_PER_STEP),
            in_specs=[
                pl.BlockSpec((block, D_MODEL), lambda i, e, *_: (i, 0)),
                pl.BlockSpec((block, D_MODEL), lambda i, e, *_: (i, 0), pipeline_mode=pl.Buffered(1)),
                pl.BlockSpec((None, ROUTE_ROWS, block), lambda i, e, *_: (i, 0, 0)),
                mspec(5),
                pl.BlockSpec((EXPERTS_PER_STEP, D_MODEL + 2 * EXPERT_DIM, 2 * EXPERT_DIM), wmap),
                pl.BlockSpec((1, D_MODEL), const2),
            ],
            out_specs=pl.BlockSpec((block, D_MODEL), lambda i, e, *_: (i, 0)),
            scratch_shapes=[
                pltpu.VMEM((nsorted, block), BF16),
                pltpu.VMEM((nsorted, D_MODEL), BF16),
                pltpu.VMEM((nsorted, D_MODEL), BF16),
                pltpu.VMEM((nsorted, LANES), F32),
            ],
        ),
        out_shape=jax.ShapeDtypeStruct((rows, D_MODEL), F32),
        compiler_params=pltpu.CompilerParams(
            dimension_semantics=("arbitrary", "arbitrary"), vmem_limit_bytes=VMEM_LIMIT),
        name="moe_experts",
    )(off[:, :, 0], nt[:, :, 0], x, h, route, mod, w_exp, fg)


def _rope_tables(pos):
    half = R_QK_DIM // 2
    inv_freq = jnp.power(ROPE_BASE, -jnp.arange(half, dtype=F32) / half)
    ang = pos.astype(F32)[:, None] * inv_freq[None, :]
    return jnp.cos(ang), jnp.sin(ang)


def _decay_tables(chunk, rows):
    log_g = jnp.log1p(-jnp.power(2.0, -5.0 - jnp.arange(R_HEADS, dtype=F32)))
    t1 = (jnp.arange(rows) % chunk).astype(F32) + 1.0
    dq = jnp.exp(log_g[None, :] * t1[:, None])
    dk = jnp.exp(-log_g[None, :] * t1[:, None]) * (R_QK_DIM ** -0.5)
    cd = jnp.exp(log_g * chunk)
    rep = lambda a: jnp.repeat(a, LANES, axis=1)
    return rep(dq), rep(dk), cd


def kernel(x_prompt, x_sample, state_ret, c_prompt, c_sample, w_mod, b_mod, norm1_g, w_in, ln_v_g, ln_v_b,
           w_s, b_s, w_a_out, w_b_out, w_o, norm2_g, w_router_group, b_router_group, w_router_expert,
           b_router_expert, w_gate, w_up, w_down, final_g):
    n_p, t_p, _ = x_prompt.shape
    n_s, t_s, _ = x_sample.shape
    assert t_p % CHUNK == 0 and (n_s * t_s) % CHUNK == 0 and CHUNK % t_s == 0
    chunks_p = t_p // CHUNK

    mod = _mod_call(jnp.concatenate([c_prompt, c_sample], axis=0), w_mod, b_mod)

    cos_p, sin_p = _rope_tables(jnp.arange(t_p, dtype=jnp.int32))
    cos_s, sin_s = _rope_tables(PAST_LEN + jnp.arange(IN_TILE, dtype=jnp.int32) % t_s)
    dq_p, dk_p, cd_p = _decay_tables(CHUNK, IN_TILE)
    dq_s, dk_s, cd_s = _decay_tables(t_s, IN_TILE)

    idx = jnp.arange(CHUNK)
    causal = idx[:, None] >= idx[None, :]
    same_seq = (idx[:, None] // t_s) == (idx[None, :] // t_s)
    mask_p = causal.astype(F32)
    mask_s = (causal & same_seq).astype(F32)

    xp = x_prompt.reshape(n_p * t_p, D_MODEL)
    xs = x_sample.reshape(n_s * t_s, D_MODEL)
    fg = final_g.reshape(1, D_MODEL)
    w_in_b = w_in[0].astype(BF16)
    mod_s = jnp.broadcast_to(mod[:, n_p:, None, :], (DEPTH, n_s, t_s, MOD_COLS))
    mod_s = mod_s.reshape(DEPTH, n_s * t_s, MOD_COLS)
    r_prompt, r_sample, v_sample = None, None, []
    for l in range(DEPTH):
        mod_p = mod[l, :n_p].reshape(n_p, 1, MOD_COLS)
        g1 = norm1_g[l].reshape(1, D_MODEL)
        g2 = norm2_g[l].reshape(1, D_MODEL)
        lng = ln_v_g[l].reshape(1, D_MODEL)
        lnb = ln_v_b[l].reshape(1, D_MODEL)

        ws_p = jnp.where(causal[None], w_s[l], 0.0).astype(BF16)
        bs_p = jnp.repeat(b_s[l].T, D_MODEL // A_GROUPS, axis=1)
        blk = jnp.where(causal[:t_s, :t_s][None], w_s[l][:, :t_s, :t_s], 0.0)
        ws_s = jnp.where(same_seq[None], jnp.tile(blk, (1, CHUNK // t_s, CHUNK // t_s)), 0.0).astype(BF16)
        bs_s = jnp.tile(bs_p[:t_s], (CHUNK // t_s, 1))

        zp, wa, wb, wo, w_exp = _inproj_call(
            xp, mod_p, g1, cos_p, sin_p, dq_p, dk_p, lng, lnb, w_in_b, layer=l, per_row=False, z_dtype=BF16,
            cast=(w_a_out, w_b_out, w_o), experts=(w_gate, w_up, w_down))
        (zs,) = _inproj_call(xs, mod_s, g1, cos_s, sin_s, dq_s, dk_s, lng, lnb, w_in_b,
                             layer=l, per_row=True, z_dtype=F32)

        more = l + 1 < DEPTH
        xp, r_prompt, cross, r_sample, *w_in_next = _mix_prompt_call(
            cd_p, zp, xp, mod_p, ws_p, bs_p, mask_p, wa, wb, wo, cd_s, zs, state_ret,
            nseq=n_p, nchunk=chunks_p, t_s=t_s, layer=l, stacked_p=r_prompt, stacked_s=r_sample,
            cast=(w_in,) if more else (), cast_layer=l + 1)
        if more:
            (w_in_b,) = w_in_next
        xs = _mix_sample_call(cd_s, zs, xs, mod_s, ws_s, bs_s, mask_s, wa, wb, wo, cross, layer=l)

        tail = ROUTER_ROWS - GROUP_ROWS - N_EXPERTS
        wr = jnp.concatenate([jnp.pad(w_router_group[l].T, ((0, GROUP_ROWS - N_GROUPS), (0, 0))),
                              jnp.pad(w_router_expert[l].T, ((0, tail), (0, 0)))], axis=0)
        br = jnp.concatenate([jnp.pad(b_router_group[l], (0, GROUP_ROWS - N_GROUPS), constant_values=NEG),
                              jnp.pad(b_router_expert[l], (0, tail))])
        wr2 = jnp.concatenate(_split_bf16(wr), axis=0)
        br = jnp.broadcast_to(br[:, None], (ROUTER_ROWS, LANES))
        last = l == DEPTH - 1
        xp = _moe_call(xp, mod_p, g2, wr2, br, w_exp, fg,
                       layer=l, block=MOE_BLOCK, per_row=False, final_norm=last)
        xs = _moe_call(xs, mod_s, g2, wr2, br, w_exp, fg,
                       layer=l, block=n_s * t_s, per_row=True, final_norm=last)

        v_sample.append(zs[:, OFF_V:OFF_V + D_MODEL].reshape(n_s, t_s, D_MODEL))

    return (xp.reshape(n_p, t_p, D_MODEL), xs.reshape(n_s, t_s, D_MODEL),
            r_prompt, r_sample, jnp.stack(v_sample))
```
